```python
import functools
import jax, jax.numpy as jnp
from jax import lax
import numpy as np

D_MODEL = 2048
BATCH = 8
SEQ = 2048
DEPTH = 1
DEC_BATCH = 32
DEC_SEQ = 4
PAST_LEN = 16384
PAGE_SIZE = 128

N_META = 16
ATTN_WIDTH = D_MODEL // 2
N_HEADS = 8
HEAD_DIM = ATTN_WIDTH // N_HEADS
N_KV_HEADS = 4
ROT_DIM = HEAD_DIM // 4
N_IDX_HEADS = 8
IDX_DIM = 64
IDX_ROT_DIM = IDX_DIM // 4
TOPK_MAX = 256
POOL_WIDTH = D_MODEL - ATTN_WIDTH
POOL_WINDOWS = (2, 4, 8, 16)
POOL_GROUP = POOL_WIDTH // len(POOL_WINDOWS)
POOL_CTX = max(POOL_WINDOWS) - 1
MIX_WIDTH = ATTN_WIDTH + POOL_WIDTH
D_FF = 5632
CONV_WIDTH = 3
ROPE_THETA = 500000.0
Q_BLOCK = 128
LN_EPS = 1e-5
DEEPNORM_ALPHA = (2.0 * DEPTH) ** 0.25
DEEPNORM_BETA = (8.0 * DEPTH) ** -0.25
IN_SIZES = (N_HEADS * HEAD_DIM, N_KV_HEADS * HEAD_DIM, N_KV_HEADS * HEAD_DIM,
            N_IDX_HEADS * IDX_DIM, IDX_DIM, N_IDX_HEADS, POOL_WIDTH)
IN_WIDTH = sum(IN_SIZES)
IN_OFFSETS = tuple(int(o) for o in np.cumsum(IN_SIZES)[:-1])

kernel_name = "hymba_dsa_pool_convffn_deepnorm_step"


def layer_norm(x, g, b):
    xf = x.astype(jnp.float32)
    mu = jnp.mean(xf, axis=-1, keepdims=True)
    var = jnp.mean(jnp.square(xf - mu), axis=-1, keepdims=True)
    y = (xf - mu) * lax.rsqrt(var + LN_EPS) * g.astype(jnp.float32) + b.astype(jnp.float32)
    return y.astype(x.dtype)


def rope_partial(x, pos, rot_dim):
    half = rot_dim // 2
    inv_freq = ROPE_THETA ** (-jnp.arange(half, dtype=jnp.float32) / half)
    ang = pos.astype(jnp.float32)[:, None] * inv_freq[None, :]
    cos = jnp.cos(ang)[:, None, :]
    sin = jnp.sin(ang)[:, None, :]
    xr = x[..., :rot_dim].astype(jnp.float32)
    x1, x2 = xr[..., :half], xr[..., half:]
    rot = jnp.concatenate([x1 * cos - x2 * sin, x2 * cos + x1 * sin], axis=-1).astype(x.dtype)
    return jnp.concatenate([rot, x[..., rot_dim:]], axis=-1)


def split_proj(x, w_in):
    h = jnp.einsum('btd,de->bte', x, w_in)
    B, T, _ = h.shape
    q, k, v, iq, ik, iw, p = jnp.split(h, IN_OFFSETS, axis=-1)
    return (q.reshape(B, T, N_HEADS, HEAD_DIM), k.reshape(B, T, N_KV_HEADS, HEAD_DIM),
            v.reshape(B, T, N_KV_HEADS, HEAD_DIM), iq.reshape(B, T, N_IDX_HEADS, IDX_DIM),
            ik, iw, p)


def index_scores(iq, iw, ik):
    s = jnp.einsum('bqhd,bsd->bqhs', iq, ik).astype(jnp.float32) * (IDX_DIM ** -0.5)
    w = iw.astype(jnp.float32) * (N_IDX_HEADS ** -0.5)
    return jnp.einsum('bqhs,bqh->bqs', jax.nn.relu(s), w)


def gather_rows(rows, idx):
    return jax.vmap(lambda r, i: r[i])(rows, idx)


def sparse_attend(q, k_sel, v_sel, valid):
    B, Q, H, Dh = q.shape
    kvh = k_sel.shape[3]
    qg = q.reshape(B, Q, kvh, H // kvh, Dh)
    s = jnp.einsum('bqgrd,bqkgd->bqgrk', qg, k_sel).astype(jnp.float32) * (Dh ** -0.5)
    s = jnp.where(valid[:, :, None, None, :], s, -jnp.inf)
    p = jax.nn.softmax(s, axis=-1).astype(v_sel.dtype)
    o = jnp.einsum('bqgrk,bqkgd->bqgrd', p, v_sel)
    return o.reshape(B, Q, H * Dh)


def prompt_attention(q, k, v, iq, ik, iw, topk):
    B, T = q.shape[:2]
    nb = -(-T // Q_BLOCK)
    pad = nb * Q_BLOCK - T
    key_pos = jnp.arange(T)

    def to_blocks(a):
        a = jnp.pad(a, [(0, 0), (0, pad)] + [(0, 0)] * (a.ndim - 2))
        return a.reshape((B, nb, Q_BLOCK) + a.shape[2:]).swapaxes(0, 1)

    def one_block(args):
        qb, iqb, iwb, pb = args
        score = index_scores(iqb, iwb, ik)
        score = jnp.where(key_pos[None, None, :] <= pb[None, :, None], score, -jnp.inf)
        _, idx = lax.top_k(score, topk)
        valid = idx <= pb[None, :, None]
        return sparse_attend(qb, gather_rows(k, idx), gather_rows(v, idx), valid)

    blocks_pos = jnp.arange(nb * Q_BLOCK).reshape(nb, Q_BLOCK)
    out = lax.map(one_block, (to_blocks(q), to_blocks(iq), to_blocks(iw), blocks_pos))
    return out.swapaxes(0, 1).reshape(B, nb * Q_BLOCK, -1)[:, :T]


def sample_attention(q, k, v, iq, ik, iw, pos, cache_k, cache_v, cache_idx_k, page_table, topk):
    DB, Q = q.shape[:2]
    past_ik = cache_idx_k[page_table].reshape(DB, PAST_LEN, IDX_DIM)
    ik_all = jnp.concatenate([past_ik, ik.astype(past_ik.dtype)], axis=1)
    key_pos = jnp.arange(PAST_LEN + Q)
    score = index_scores(iq, iw, ik_all)
    score = jnp.where(key_pos[None, None, :] <= pos[None, :, None], score, -jnp.inf)
    _, idx = lax.top_k(score, topk)
    valid = idx <= pos[None, :, None]
    in_past = idx < PAST_LEN
    ip = jnp.minimum(idx, PAST_LEN - 1)
    phys = jax.vmap(lambda pt, pg: pt[pg])(page_table, ip // PAGE_SIZE)
    off = ip % PAGE_SIZE
    inew = jnp.clip(idx - PAST_LEN, 0, Q - 1)
    sel = in_past[..., None, None]
    k_sel = jnp.where(sel, cache_k[phys, off], gather_rows(k, inew).astype(cache_k.dtype))
    v_sel = jnp.where(sel, cache_v[phys, off], gather_rows(v, inew).astype(cache_v.dtype))
    return sparse_attend(q.astype(k_sel.dtype), k_sel, v_sel, valid)


def multiscale_pool(p_ext, pos):
    Q = pos.shape[0]
    n_ctx = p_ext.shape[1] - Q
    B, _, C = p_ext.shape
    cs = jnp.concatenate([jnp.zeros((B, 1, C), jnp.float32),
                          jnp.cumsum(p_ext.astype(jnp.float32), axis=1)], axis=1)
    hi = cs[:, n_ctx + 1:]
    x_self = p_ext[:, n_ctx:].astype(jnp.float32)
    outs = []
    for g, w in enumerate(POOL_WINDOWS):
        c0, c1 = g * POOL_GROUP, (g + 1) * POOL_GROUP
        lo = cs[:, n_ctx + 1 - w: n_ctx + 1 - w + Q, c0:c1]
        cnt = jnp.minimum(w, pos + 1).astype(jnp.float32)[None, :, None]
        outs.append((hi[:, :, c0:c1] - lo) / cnt - x_self[:, :, c0:c1])
    return jnp.concatenate(outs, axis=-1).astype(p_ext.dtype)


def conv_ffn(x, u_prefix, w_up, conv_w, conv_b, w_down):
    u = jnp.einsum('btd,de->bte', x, w_up)
    u_ext = jnp.concatenate([u_prefix.astype(u.dtype), u], axis=1)
    Q = u.shape[1]
    c = conv_b
    for j in range(CONV_WIDTH):
        c = c + conv_w[j] * u_ext[:, j:j + Q]
    gate, val = jnp.split(c, 2, axis=-1)
    h = jax.nn.silu(gate) * val
    return jnp.einsum('btf,fd->btd', h, w_down), u_ext[:, -(CONV_WIDTH - 1):]


def trunk_layer(x, pos, pool_prefix, conv_prefix, attend, w_in, w_pool, pool_scale, w_out,
                ln1_g, ln1_b, w_up, conv_w, conv_b, w_down, ln2_g, ln2_b):
    q, k, v, iq, ik, iw, p = split_proj(x, w_in)
    q = rope_partial(q, pos, ROT_DIM)
    k = rope_partial(k, pos, ROT_DIM)
    iq = rope_partial(iq, pos, IDX_ROT_DIM)
    ik = rope_partial(ik[:, :, None, :], pos, IDX_ROT_DIM)[:, :, 0]
    a = attend(q, k, v, iq, ik, iw).astype(x.dtype)
    p_ext = jnp.concatenate([pool_prefix.astype(p.dtype), p], axis=1)
    pooled = multiscale_pool(p_ext, pos)
    B, Q, _ = pooled.shape
    m = jnp.einsum('bqgc,gce->bqge', pooled.reshape(B, Q, len(POOL_WINDOWS), POOL_GROUP),
                   w_pool).reshape(B, Q, POOL_WIDTH) * pool_scale
    mix = jnp.einsum('bqm,md->bqd', jnp.concatenate([a, m], axis=-1), w_out)
    x = layer_norm(DEEPNORM_ALPHA * x + mix, ln1_g, ln1_b)
    f, conv_state = conv_ffn(x, conv_prefix, w_up, conv_w, conv_b, w_down)
    x = layer_norm(DEEPNORM_ALPHA * x + f, ln2_g, ln2_b)
    return x, k, v, ik, p_ext[:, -POOL_CTX:], conv_state


def setup_inputs(seed: int = 0) -> dict:
    key = jax.random.key(seed)
    ks = jax.random.split(key, 24)
    n_pages = PAST_LEN // PAGE_SIZE
    n_used = DEC_BATCH * n_pages
    n_phys = n_used + max(1, n_used // 4)
    nrm = jax.random.normal
    f32 = jnp.float32
    page_table = jax.random.permutation(ks[0], n_phys)[:n_used].reshape(DEC_BATCH, n_pages).astype(jnp.int32)
    return {
        "x_prompt": nrm(ks[1], (BATCH, SEQ, D_MODEL), f32),
        "x_sample": nrm(ks[2], (DEC_BATCH, DEC_SEQ, D_MODEL), f32),
        "cache_k": nrm(ks[3], (DEPTH, n_phys, PAGE_SIZE, N_KV_HEADS, HEAD_DIM), f32),
        "cache_v": nrm(ks[4], (DEPTH, n_phys, PAGE_SIZE, N_KV_HEADS, HEAD_DIM), f32),
        "cache_idx_k": nrm(ks[5], (DEPTH, n_phys, PAGE_SIZE, IDX_DIM), f32),
        "state_pool": nrm(ks[6], (DEPTH, DEC_BATCH, POOL_CTX, POOL_WIDTH), f32),
        "state_conv": nrm(ks[7], (DEPTH, DEC_BATCH, CONV_WIDTH - 1, 2 * D_FF), f32),
        "page_table": page_table,
        "meta_tokens": nrm(ks[8], (N_META, D_MODEL), f32),
        "w_in": nrm(ks[9], (DEPTH, D_MODEL, IN_WIDTH), f32) * D_MODEL ** -0.5,
        "w_pool": nrm(ks[10], (DEPTH, len(POOL_WINDOWS), POOL_GROUP, POOL_GROUP), f32) * POOL_GROUP ** -0.5,
        "pool_scale": 1.0 + 0.02 * nrm(ks[11], (DEPTH, POOL_WIDTH), f32),
        "w_out": nrm(ks[12], (DEPTH, MIX_WIDTH, D_MODEL), f32) * (MIX_WIDTH ** -0.5) * DEEPNORM_BETA,
        "ln1_g": 1.0 + 0.02 * nrm(ks[13], (DEPTH, D_MODEL), f32),
        "ln1_b": 0.01 * nrm(ks[14], (DEPTH, D_MODEL), f32),
        "w_up": nrm(ks[15], (DEPTH, D_MODEL, 2 * D_FF), f32) * D_MODEL ** -0.5,
        "conv_w": nrm(ks[16], (DEPTH, CONV_WIDTH, 2 * D_FF), f32) * CONV_WIDTH ** -0.5,
        "conv_b": 0.01 * nrm(ks[17], (DEPTH, 2 * D_FF), f32),
        "w_down": nrm(ks[18], (DEPTH, D_FF, D_MODEL), f32) * (D_FF ** -0.5) * DEEPNORM_BETA,
        "ln2_g": 1.0 + 0.02 * nrm(ks[19], (DEPTH, D_MODEL), f32),
        "ln2_b": 0.01 * nrm(ks[20], (DEPTH, D_MODEL), f32),
    }


def reference(x_prompt, x_sample, cache_k, cache_v, cache_idx_k, state_pool, state_conv, page_table,
              meta_tokens, w_in, w_pool, pool_scale, w_out, ln1_g, ln1_b, w_up, conv_w, conv_b,
              w_down, ln2_g, ln2_b):
    B, S, D = x_prompt.shape
    DB, QS, _ = x_sample.shape
    T = N_META + S
    xp = jnp.concatenate([jnp.broadcast_to(meta_tokens.astype(x_prompt.dtype)[None], (B, N_META, D)),
                          x_prompt], axis=1)
    xs = x_sample
    pos_p = jnp.arange(T)
    pos_s = PAST_LEN + jnp.arange(QS)
    topk_p = min(TOPK_MAX, S // 4)
    topk_s = min(TOPK_MAX, (PAST_LEN + QS) // 4)
    kp_l, vp_l, ikp_l, poolp_l, convp_l = [], [], [], [], []
    ks_l, vs_l, iks_l, pools_l, convs_l = [], [], [], [], []
    for l in range(DEPTH):
        lw = (w_in[l], w_pool[l], pool_scale[l], w_out[l], ln1_g[l], ln1_b[l],
              w_up[l], conv_w[l], conv_b[l], w_down[l], ln2_g[l], ln2_b[l])
        attend_p = functools.partial(prompt_attention, topk=topk_p)
        attend_s = functools.partial(sample_attention, pos=pos_s, cache_k=cache_k[l], cache_v=cache_v[l],
                                     cache_idx_k=cache_idx_k[l], page_table=page_table, topk=topk_s)
        xp, kp, vp, ikp, poolp, convp = trunk_layer(
            xp, pos_p, jnp.zeros((B, POOL_CTX, POOL_WIDTH), xp.dtype),
            jnp.zeros((B, CONV_WIDTH - 1, 2 * D_FF), xp.dtype), attend_p, *lw)
        xs, ks_, vs_, iks, pools, convs = trunk_layer(
            xs, pos_s, state_pool[l], state_conv[l], attend_s, *lw)
        kp_l.append(kp); vp_l.append(vp); ikp_l.append(ikp); poolp_l.append(poolp); convp_l.append(convp)
        ks_l.append(ks_); vs_l.append(vs_); iks_l.append(iks); pools_l.append(pools); convs_l.append(convs)
    y_prompt = xp[:, N_META:]
    return (y_prompt, xs,
            jnp.stack(kp_l), jnp.stack(vp_l), jnp.stack(ikp_l), jnp.stack(poolp_l), jnp.stack(convp_l),
            jnp.stack(ks_l), jnp.stack(vs_l), jnp.stack(iks_l), jnp.stack(pools_l), jnp.stack(convs_l))
```

```python
import functools

import numpy as np
import jax
import jax.numpy as jnp
from jax import lax
from jax.experimental import pallas as pl
from jax.experimental.pallas import tpu as pltpu

F32 = jnp.float32
BF16 = jnp.bfloat16
I32 = jnp.int32

D_MODEL = 2048
BATCH = 8
SEQ = 2048
DEC_BATCH = 32
DEC_SEQ = 4
PAST_LEN = 16384
PAGE_SIZE = 128
N_META = 16
ATTN_WIDTH = 1024
N_HEADS = 8
HEAD_DIM = 128
N_KV_HEADS = 4
KV_WIDTH = N_KV_HEADS * HEAD_DIM
ROT_DIM = 32
N_IDX_HEADS = 8
IDX_DIM = 64
IDX_ROT_DIM = 16
TOPK = 256
POOL_WIDTH = 1024
POOL_WINDOWS = (2, 4, 8, 16)
POOL_GROUP = 256
POOL_CTX = 15
D_FF = 5632
ROPE_THETA = 500000.0
LN_EPS = 1e-5
DEEPNORM_ALPHA = 2.0 ** 0.25
IN_OFFSETS = (1024, 1536, 2048, 2560, 2624, 2632)

LANES = 128
N_PROMPT_ROWS = BATCH * SEQ
N_SAMPLE_ROWS = DEC_BATCH * DEC_SEQ
SMALL_ROWS = 256
META_ROW0 = N_SAMPLE_ROWS
T_PROMPT = N_META + SEQ
NK_PROMPT = 2176
NK_SAMPLE = PAST_LEN + LANES
N_PAGES = PAST_LEN // PAGE_SIZE

C_Q, C_P, C_K, C_V, C_IQ, C_IK, C_IW = 0, 1024, 2048, 2560, 3072, 3584, 3712
PROJ_WIDTH = 3840
TN = 256
ROPE_A_TILES = (0, 1, 2, 3, 8, 9)
ROPE_B_TILES = (12, 13)
ROPE_BH_TILE = 14

VMEM_LIMIT = 56 * 1024 * 1024
INT_MIN = -2 ** 31
NEG_INF = float("-inf")


def _cparams(sem, **kw):
    return pltpu.CompilerParams(dimension_semantics=sem, vmem_limit_bytes=VMEM_LIMIT, **kw)


def _rope_tables(pos, rot_dim, head_dim):
    half = rot_dim // 2
    inv_freq = ROPE_THETA ** (-jnp.arange(half, dtype=F32) / half)
    ang = pos.astype(F32)[:, None] * inv_freq[None, :]
    cos, sin = jnp.cos(ang), jnp.sin(ang)
    rows = pos.shape[0]
    zh = jnp.zeros((rows, half), F32)
    rest0 = jnp.zeros((rows, head_dim - rot_dim), F32)
    c = jnp.concatenate([cos, cos, rest0 + 1.0], axis=1)
    s1 = jnp.concatenate([-sin, zh, rest0], axis=1)
    s2 = jnp.concatenate([zh, sin, rest0], axis=1)
    reps = LANES // head_dim
    return tuple(jnp.tile(t, (1, reps)) for t in (c, s1, s2))


def _rope(x, c_ref, s1_ref, s2_ref, half):
    return (x * c_ref[...] + pltpu.roll(x, LANES - half, 1) * s1_ref[...]
            + pltpu.roll(x, half, 1) * s2_ref[...])


def _proj_kernel(x_ref, w_ref, ca, sa1, sa2, cb, sb1, sb2, o_ref, xb_ref):
    j = pl.program_id(1)

    @pl.when(j == 0)
    def _():
        xb_ref[...] = x_ref[...].astype(BF16)

    acc = jnp.dot(xb_ref[...], w_ref[...], preferred_element_type=F32)
    is_a = functools.reduce(jnp.logical_or, [j == t for t in ROPE_A_TILES])
    is_b = functools.reduce(jnp.logical_or, [j == t for t in ROPE_B_TILES])
    is_bh = j == ROPE_BH_TILE

    @pl.when(is_a)
    def _():
        o_ref[:, :LANES] = _rope(acc[:, :LANES], ca, sa1, sa2, ROT_DIM // 2)
        o_ref[:, LANES:] = _rope(acc[:, LANES:], ca, sa1, sa2, ROT_DIM // 2)

    @pl.when(is_b)
    def _():
        o_ref[:, :LANES] = _rope(acc[:, :LANES], cb, sb1, sb2, IDX_ROT_DIM // 2)
        o_ref[:, LANES:] = _rope(acc[:, LANES:], cb, sb1, sb2, IDX_ROT_DIM // 2)

    @pl.when(is_bh)
    def _():
        o_ref[:, :LANES] = _rope(acc[:, :LANES], cb, sb1, sb2, IDX_ROT_DIM // 2)
        o_ref[:, LANES:] = acc[:, LANES:]

    @pl.when(jnp.logical_not(is_a | is_b | is_bh))
    def _():
        o_ref[...] = acc


def _project(x, w_all, tabs, tab_blocks, tm):
    rows = x.shape[0]
    tab_spec = pl.BlockSpec((tm, LANES), lambda i, j: (i % tab_blocks, 0))
    return pl.pallas_call(
        _proj_kernel,
        grid=(rows // tm, PROJ_WIDTH // TN),
        in_specs=[pl.BlockSpec((tm, D_MODEL), lambda i, j: (i, 0)),
                  pl.BlockSpec((D_MODEL, TN), lambda i, j: (0, j))] + [tab_spec] * 6,
        out_specs=pl.BlockSpec((tm, TN), lambda i, j: (i, j)),
        out_shape=jax.ShapeDtypeStruct((rows, PROJ_WIDTH), F32),
        scratch_shapes=[pltpu.VMEM((tm, D_MODEL), BF16)],
        compiler_params=_cparams(("arbitrary", "arbitrary")),
        name="in_proj",
    )(x, w_all, *tabs)


def _mm_kernel(x_ref, w_ref, o_ref):
    o_ref[...] = jnp.dot(x_ref[...], w_ref[...], preferred_element_type=F32)


def _matmul(x, w, tm, name):
    rows, kdim = x.shape
    n = w.shape[1]
    return pl.pallas_call(
        _mm_kernel,
        grid=(rows // tm, n // TN),
        in_specs=[pl.BlockSpec((tm, kdim), lambda i, j: (i, 0)),
                  pl.BlockSpec((kdim, TN), lambda i, j: (0, j))],
        out_specs=pl.BlockSpec((tm, TN), lambda i, j: (i, j)),
        out_shape=jax.ShapeDtypeStruct((rows, n), F32),
        compiler_params=_cparams(("arbitrary", "arbitrary")),
        name=name,
    )(x, w)


def _sortable_key(score):
    score = jnp.where(score == 0.0, 0.0, score)
    bits = lax.bitcast_convert_type(score, I32)
    return bits ^ ((bits >> 31) & 0x7FFFFFFF)


def _select_topk(key_ref, col, valid, n_col_bits):
    rows = key_ref.shape[0]
    k_f = float(TOPK)

    def count(pred):
        return jnp.sum(jnp.where(pred, 1.0, 0.0), axis=1, keepdims=True)

    cnt0 = count(key_ref[...] >= 0)
    thr0 = jnp.where(cnt0 >= k_f, 0, INT_MIN).astype(I32)

    def value_step(i, thr):
        cand = thr + jnp.left_shift(jnp.int32(1), 30 - i)
        cnt = count(key_ref[...] >= cand)
        return jnp.where(cnt >= k_f, cand, thr)

    thr = lax.fori_loop(0, 31, value_step, thr0)
    need = k_f - count(key_ref[...] > thr)

    def col_step(i, cut):
        cand = cut + jnp.left_shift(jnp.int32(1), n_col_bits - 1 - i)
        cnt = count((key_ref[...] == thr) & (col < cand))
        return jnp.where(cnt < need, cand, cut)

    cut = lax.fori_loop(0, n_col_bits, col_step, jnp.zeros((rows, 1), I32))
    key = key_ref[...]
    return valid & ((key > thr) | ((key == thr) & (col <= cut)))


def _pattn_kernel(q_ref, iq_ref, misc_ref, kb_ref, vb_ref, ikb_ref, o_ref, key_scr, bias_scr,
                  *, blocks_per_batch, pos_base, nk):
    g = pl.program_id(0)
    pos0 = pos_base + LANES * (g % blocks_per_batch)
    tq = q_ref.shape[0]

    ik = ikb_ref[0]
    w = misc_ref[:, LANES:LANES + N_IDX_HEADS] * (N_IDX_HEADS ** -0.5)
    score = jnp.zeros((tq, nk), F32)
    for h in range(N_IDX_HEADS):
        iq_h = iq_ref[:, h * IDX_DIM:(h + 1) * IDX_DIM].astype(BF16)
        s_h = lax.dot_general(iq_h, ik, (((1,), (1,)), ((), ())), preferred_element_type=F32)
        score = score + jnp.maximum(s_h * (IDX_DIM ** -0.5), 0.0) * w[:, h:h + 1]

    qpos = pos0 + lax.broadcasted_iota(I32, (tq, nk), 0)
    col = lax.broadcasted_iota(I32, (tq, nk), 1)
    causal = col <= qpos
    key_scr[...] = jnp.where(causal, _sortable_key(score), INT_MIN)
    sel = _select_topk(key_scr, col, causal, 12)
    bias_scr[...] = jnp.where(sel, 0.0, NEG_INF)

    for h in range(N_HEADS):
        kv = h // (N_HEADS // N_KV_HEADS)
        q_h = q_ref[:, h * HEAD_DIM:(h + 1) * HEAD_DIM].astype(BF16)
        k_g = kb_ref[0, :, kv * HEAD_DIM:(kv + 1) * HEAD_DIM]
        s = lax.dot_general(q_h, k_g, (((1,), (1,)), ((), ())), preferred_element_type=F32)
        s = s * (HEAD_DIM ** -0.5) + bias_scr[...]
        m = jnp.max(s, axis=1, keepdims=True)
        p = jnp.exp(s - m)
        l = jnp.sum(p, axis=1, keepdims=True)
        v_g = vb_ref[0, :, kv * HEAD_DIM:(kv + 1) * HEAD_DIM]
        o = jnp.dot(p.astype(BF16), v_g, preferred_element_type=F32) / l
        o_ref[:, h * HEAD_DIM:(h + 1) * HEAD_DIM] = o.astype(o_ref.dtype)


def _prompt_attention(h, kb, vb, ikb, *, n_blocks, row_block0, blocks_per_batch, pos_base, nk):
    kern = functools.partial(_pattn_kernel, blocks_per_batch=blocks_per_batch, pos_base=pos_base, nk=nk)
    return pl.pallas_call(
        kern,
        grid=(n_blocks,),
        in_specs=[pl.BlockSpec((LANES, ATTN_WIDTH), lambda g: (row_block0 + g, C_Q // ATTN_WIDTH)),
                  pl.BlockSpec((LANES, 512), lambda g: (row_block0 + g, C_IQ // 512)),
                  pl.BlockSpec((LANES, 256), lambda g: (row_block0 + g, C_IK // 256)),
                  pl.BlockSpec((1, nk, KV_WIDTH), lambda g: (g // blocks_per_batch, 0, 0)),
                  pl.BlockSpec((1, nk, KV_WIDTH), lambda g: (g // blocks_per_batch, 0, 0)),
                  pl.BlockSpec((1, nk, IDX_DIM), lambda g: (g // blocks_per_batch, 0, 0))],
        out_specs=pl.BlockSpec((LANES, ATTN_WIDTH), lambda g: (g, 0)),
        out_shape=jax.ShapeDtypeStruct((n_blocks * LANES, ATTN_WIDTH), BF16),
        scratch_shapes=[pltpu.VMEM((LANES, nk), I32), pltpu.VMEM((LANES, nk), F32)],
        compiler_params=_cparams(("arbitrary",)),
        name="prompt_attention",
    )(h, h, h, kb, vb, ikb)


N_MROWS = 256


def _ssel_kernel(pt_ref, iq_ref, w_ref, iknew_ref, pthi_ref, ptlo_ref, cache_ref,
                 rows_ref, gbias_ref, nbias_ref, ikbuf, sem, score_scr, key_scr, sel_scr, m_scr):
    b = pl.program_id(0)

    def issue(p, carry):
        pltpu.make_async_copy(cache_ref.at[pt_ref[b, p]], ikbuf.at[p], sem).start()
        return carry

    lax.fori_loop(0, N_PAGES, issue, 0)
    pltpu.make_async_copy(cache_ref.at[pl.ds(0, N_PAGES)], ikbuf, sem).wait()

    iq = iq_ref[0].astype(BF16)
    w = w_ref[0] * (N_IDX_HEADS ** -0.5)

    def score_of(keys):
        s = lax.dot_general(iq, keys.astype(BF16), (((1,), (1,)), ((), ())), preferred_element_type=F32)
        r = jnp.maximum(s * (IDX_DIM ** -0.5), 0.0) * w
        acc = r[0:8]
        for h in range(1, N_IDX_HEADS):
            acc = acc + r[8 * h:8 * h + 8]
        return acc

    chunk_pages = 8
    for c in range(N_PAGES // chunk_pages):
        keys = ikbuf[c * chunk_pages:(c + 1) * chunk_pages].reshape(chunk_pages * PAGE_SIZE, IDX_DIM)
        score_scr[:, c * 1024:(c + 1) * 1024] = score_of(keys)
    score_scr[:, PAST_LEN:] = score_of(iknew_ref[0])

    qrow = lax.broadcasted_iota(I32, (8, NK_SAMPLE), 0)
    col = lax.broadcasted_iota(I32, (8, NK_SAMPLE), 1)
    valid = col <= PAST_LEN + qrow
    key_scr[...] = jnp.where(valid, _sortable_key(score_scr[...]), INT_MIN)
    sel = _select_topk(key_scr, col, valid, 15)
    sel_scr[...] = jnp.where(sel, 1.0, 0.0)

    nt = (((1,), (1,)), ((), ()))
    ones_b = jnp.ones((8, LANES), BF16)
    upper_pages = (lax.broadcasted_iota(I32, (N_MROWS, N_MROWS), 0)
                   < lax.broadcasted_iota(I32, (N_MROWS, N_MROWS), 1)).astype(BF16)
    upper_lanes = (lax.broadcasted_iota(I32, (LANES, LANES), 0)
                   < lax.broadcasted_iota(I32, (LANES, LANES), 1)).astype(BF16)
    lane_id = lax.broadcasted_iota(I32, (8, LANES), 1).astype(BF16)
    page_id = lax.broadcasted_iota(I32, (8, N_MROWS), 1).astype(BF16)
    jcol = lax.broadcasted_iota(I32, (TOPK, 1), 0).astype(F32)
    pt_hi = pthi_ref[0].astype(BF16)
    pt_lo = ptlo_ref[0].astype(BF16)

    rows_ref[...] = jnp.zeros(rows_ref.shape, I32)
    gbias_ref[...] = jnp.full(gbias_ref.shape, NEG_INF, F32)
    nbias_ref[...] = jnp.full(nbias_ref.shape, NEG_INF, F32)
    m_scr[...] = jnp.zeros(m_scr.shape, F32)
    for q in range(DEC_SEQ):
        for c in range(N_PAGES + 1):
            m_scr[c:c + 1, :] = sel_scr[q:q + 1, c * LANES:(c + 1) * LANES]
        m_b = m_scr[...].astype(BF16)
        cnt = lax.dot_general(ones_b, m_b, nt, preferred_element_type=F32)
        start = jnp.dot(cnt.astype(BF16), upper_pages, preferred_element_type=F32)
        cnt1, start1 = cnt[0:1], start[0:1]
        in_page = (start1 <= jcol) & (jcol < start1 + cnt1)
        a_b = jnp.where(in_page, 1.0, 0.0).astype(BF16)
        m_j = jnp.dot(a_b, m_b, preferred_element_type=F32)
        rank = jnp.dot(m_j.astype(BF16), upper_lanes, preferred_element_type=F32)
        start_j = jnp.sum(jnp.where(in_page, start1, 0.0), axis=1, keepdims=True)
        onehot = jnp.where((m_j > 0.5) & (rank == jcol - start_j), 1.0, 0.0).astype(BF16)
        off = lax.dot_general(lane_id, onehot, nt, preferred_element_type=F32)[0:1]
        page = lax.dot_general(page_id, a_b, nt, preferred_element_type=F32)[0:1]
        phys = (lax.dot_general(pt_hi, a_b, nt, preferred_element_type=F32)[0:1] * 64.0
                + lax.dot_general(pt_lo, a_b, nt, preferred_element_type=F32)[0:1])
        in_past = page < float(N_PAGES)
        row = jnp.where(in_past, phys * float(PAGE_SIZE) + off, 0.0).astype(I32)
        rows_ref[0, q:q + 1, :] = row
        gbias_ref[0, q:q + 1, :] = jnp.where(in_past, 0.0, NEG_INF)
        nbias_ref[0, q:q + 1, :] = jnp.where(sel_scr[q:q + 1, PAST_LEN:] > 0.5, 0.0, NEG_INF)


def _sample_select(page_table, iq_s, w_s, ik_new, pt_hi, pt_lo, cache_idx):
    grid_spec = pltpu.PrefetchScalarGridSpec(
        num_scalar_prefetch=1,
        grid=(DEC_BATCH,),
        in_specs=[pl.BlockSpec((1, 64, IDX_DIM), lambda b, pt: (b, 0, 0)),
                  pl.BlockSpec((1, 64, 1), lambda b, pt: (b, 0, 0)),
                  pl.BlockSpec((1, LANES, IDX_DIM), lambda b, pt: (b, 0, 0)),
                  pl.BlockSpec((1, 8, N_MROWS), lambda b, pt: (b, 0, 0)),
                  pl.BlockSpec((1, 8, N_MROWS), lambda b, pt: (b, 0, 0)),
                  pl.BlockSpec(memory_space=pl.ANY)],
        out_specs=[pl.BlockSpec((1, 8, TOPK), lambda b, pt: (b, 0, 0)),
                   pl.BlockSpec((1, 8, TOPK), lambda b, pt: (b, 0, 0)),
                   pl.BlockSpec((1, 8, LANES), lambda b, pt: (b, 0, 0))],
        scratch_shapes=[pltpu.VMEM((N_PAGES, PAGE_SIZE, IDX_DIM), F32),
                        pltpu.SemaphoreType.DMA(()),
                        pltpu.VMEM((8, NK_SAMPLE), F32),
                        pltpu.VMEM((8, NK_SAMPLE), I32),
                        pltpu.VMEM((8, NK_SAMPLE), F32),
                        pltpu.VMEM((N_MROWS, LANES), F32)])
    return pl.pallas_call(
        _ssel_kernel,
        grid_spec=grid_spec,
        out_shape=[jax.ShapeDtypeStruct((DEC_BATCH, 8, TOPK), I32),
                   jax.ShapeDtypeStruct((DEC_BATCH, 8, TOPK), F32),
                   jax.ShapeDtypeStruct((DEC_BATCH, 8, LANES), F32)],
        compiler_params=_cparams(("arbitrary",)),
        name="sample_select",
    )(page_table, iq_s, w_s, ik_new, pt_hi, pt_lo, cache_idx)


def _sattn_kernel(rows_ref, gbias_ref, nbias_ref, qbd_ref, knew_ref, vnew_ref, ck_ref, cv_ref, o_ref,
                  kbuf, vbuf, sem):
    b = pl.program_id(0)
    n_rows = DEC_SEQ * TOPK

    def issue(t, carry):
        q = t // TOPK
        r = t - q * TOPK
        row = rows_ref[(b * 8 + q) * TOPK + r]
        pltpu.make_async_copy(ck_ref.at[pl.ds(row, 1)], kbuf.at[pl.ds(t, 1)], sem.at[0]).start()
        pltpu.make_async_copy(cv_ref.at[pl.ds(row, 1)], vbuf.at[pl.ds(t, 1)], sem.at[1]).start()
        return carry

    lax.fori_loop(0, n_rows, issue, 0)
    pltpu.make_async_copy(ck_ref.at[pl.ds(0, n_rows)], kbuf, sem.at[0]).wait()
    pltpu.make_async_copy(cv_ref.at[pl.ds(0, n_rows)], vbuf, sem.at[1]).wait()

    nt = (((1,), (1,)), ((), ()))
    scale = HEAD_DIM ** -0.5
    k_new = knew_ref[0].astype(BF16)
    v_new = vnew_ref[0].astype(BF16)
    o_ref[...] = jnp.zeros(o_ref.shape, o_ref.dtype)
    for q in range(DEC_SEQ):
        q_bd = qbd_ref[0, q].astype(BF16)
        k_q = kbuf[q * TOPK:(q + 1) * TOPK, :].astype(BF16)
        v_q = vbuf[q * TOPK:(q + 1) * TOPK, :].astype(BF16)
        s = lax.dot_general(q_bd, k_q, nt, preferred_element_type=F32) * scale + gbias_ref[0, q:q + 1, :]
        sn = lax.dot_general(q_bd, k_new, nt, preferred_element_type=F32) * scale + nbias_ref[0, q:q + 1, :]
        m = jnp.maximum(jnp.max(s, axis=1, keepdims=True), jnp.max(sn, axis=1, keepdims=True))
        p = jnp.exp(s - m)
        pn = jnp.exp(sn - m)
        l = jnp.sum(p, axis=1, keepdims=True) + jnp.sum(pn, axis=1, keepdims=True)
        o = (jnp.dot(p.astype(BF16), v_q, preferred_element_type=F32)
             + jnp.dot(pn.astype(BF16), v_new, preferred_element_type=F32)) / l
        for h in range(N_HEADS):
            kv = h // (N_HEADS // N_KV_HEADS)
            o_ref[0, q:q + 1, h * HEAD_DIM:(h + 1) * HEAD_DIM] = o[h:h + 1, kv * HEAD_DIM:(kv + 1) * HEAD_DIM]


def _sample_attention(rows_flat, gbias, nbias, qbd, k_new, v_new, cache_k2d, cache_v2d):
    grid_spec = pltpu.PrefetchScalarGridSpec(
        num_scalar_prefetch=1,
        grid=(DEC_BATCH,),
        in_specs=[pl.BlockSpec((1, 8, TOPK), lambda b, r: (b, 0, 0)),
                  pl.BlockSpec((1, 8, LANES), lambda b, r: (b, 0, 0)),
                  pl.BlockSpec((1, DEC_SEQ, N_HEADS, KV_WIDTH), lambda b, r: (b, 0, 0, 0)),
                  pl.BlockSpec((1, LANES, KV_WIDTH), lambda b, r: (b, 0, 0)),
                  pl.BlockSpec((1, LANES, KV_WIDTH), lambda b, r: (b, 0, 0)),
                  pl.BlockSpec(memory_space=pl.ANY),
                  pl.BlockSpec(memory_space=pl.ANY)],
        out_specs=pl.BlockSpec((1, 8, ATTN_WIDTH), lambda b, r: (b, 0, 0)),
        scratch_shapes=[pltpu.VMEM((DEC_SEQ * TOPK, KV_WIDTH), F32),
                        pltpu.VMEM((DEC_SEQ * TOPK, KV_WIDTH), F32),
                        pltpu.SemaphoreType.DMA((2,))])
    return pl.pallas_call(
        _sattn_kernel,
        grid_spec=grid_spec,
        out_shape=jax.ShapeDtypeStruct((DEC_BATCH, 8, ATTN_WIDTH), F32),
        compiler_params=_cparams(("arbitrary",), disable_bounds_checks=True),
        name="sample_attention",
    )(rows_flat, gbias, nbias, qbd, k_new, v_new, cache_k2d, cache_v2d)


def _pool_kernel(p_ref, halo_ref, first_ref, cnt_ref, o_ref, ext_scr, *, tiles_per_batch):
    i = pl.program_id(0)
    tm = p_ref.shape[0]
    ext_scr[0:N_META, :] = jnp.where(i % tiles_per_batch == 0, first_ref[...], halo_ref[...])
    ext_scr[N_META:, :] = p_ref[...]
    for g, win in enumerate(POOL_WINDOWS):
        cols = slice(g * POOL_GROUP, (g + 1) * POOL_GROUP)
        x_self = ext_scr[N_META:N_META + tm, cols]
        acc = x_self
        for d in range(1, win):
            acc = acc + ext_scr[N_META - d:N_META - d + tm, cols]
        o_ref[:, cols] = (acc / cnt_ref[:, cols] - x_self).astype(o_ref.dtype)


def _pool(p_src, p_col_block, halo_src, first_src, first_block, cnt, *, rows, tm, tiles_per_batch):
    kern = functools.partial(_pool_kernel, tiles_per_batch=tiles_per_batch)
    halo_per_tile = tm // N_META
    cnt_rows = cnt.shape[0]
    cnt_map = (lambda i: (0, 0)) if cnt_rows == 1 else (lambda i: (i, 0))
    return pl.pallas_call(
        kern,
        grid=(rows // tm,),
        in_specs=[pl.BlockSpec((tm, POOL_WIDTH), lambda i: (i, p_col_block)),
                  pl.BlockSpec((N_META, POOL_WIDTH),
                               lambda i: (jnp.maximum(i * halo_per_tile - 1, 0), p_col_block)),
                  pl.BlockSpec((N_META, POOL_WIDTH), lambda i: first_block),
                  pl.BlockSpec((1 if cnt_rows == 1 else tm, POOL_WIDTH), cnt_map)],
        out_specs=pl.BlockSpec((tm, POOL_WIDTH), lambda i: (i, 0)),
        out_shape=jax.ShapeDtypeStruct((rows, POOL_WIDTH), BF16),
        scratch_shapes=[pltpu.VMEM((tm + N_META, POOL_WIDTH), F32)],
        compiler_params=_cparams(("arbitrary",)),
        name="pool",
    )(p_src, halo_src, first_src, cnt)


def _layer_norm(y, g_ref, b_ref):
    mu = jnp.mean(y, axis=-1, keepdims=True)
    var = jnp.mean(jnp.square(y - mu), axis=-1, keepdims=True)
    return (y - mu) * lax.rsqrt(var + LN_EPS) * g_ref[...] + b_ref[...]


def _mix_kernel(pooled_ref, a_ref, x_ref, wp_ref, ps_ref, wo_ref, g_ref, b_ref, x1_ref, x1b_ref):
    parts = []
    for g in range(len(POOL_WINDOWS)):
        cols = slice(g * POOL_GROUP, (g + 1) * POOL_GROUP)
        parts.append(jnp.dot(pooled_ref[:, cols], wp_ref[g], preferred_element_type=F32))
    m = jnp.concatenate(parts, axis=1) * ps_ref[...]
    mix = (jnp.dot(a_ref[...], wo_ref[:ATTN_WIDTH, :], preferred_element_type=F32)
           + jnp.dot(m.astype(BF16), wo_ref[ATTN_WIDTH:, :], preferred_element_type=F32))
    x1 = _layer_norm(DEEPNORM_ALPHA * x_ref[...] + mix, g_ref, b_ref)
    x1_ref[...] = x1
    x1b_ref[...] = x1.astype(BF16)


def _mix(pooled, a, x, w_pool, pool_scale, w_out, g1, b1, tm):
    rows = x.shape[0]
    row = lambda i: (i, 0)
    const2 = lambda i: (0, 0)
    return pl.pallas_call(
        _mix_kernel,
        grid=(rows // tm,),
        in_specs=[pl.BlockSpec((tm, POOL_WIDTH), row),
                  pl.BlockSpec((tm, ATTN_WIDTH), row),
                  pl.BlockSpec((tm, D_MODEL), row),
                  pl.BlockSpec((len(POOL_WINDOWS), POOL_GROUP, POOL_GROUP), lambda i: (0, 0, 0)),
                  pl.BlockSpec((1, POOL_WIDTH), const2),
                  pl.BlockSpec((D_MODEL, D_MODEL), const2),
                  pl.BlockSpec((1, D_MODEL), const2),
                  pl.BlockSpec((1, D_MODEL), const2)],
        out_specs=[pl.BlockSpec((tm, D_MODEL), row), pl.BlockSpec((tm, D_MODEL), row)],
        out_shape=[jax.ShapeDtypeStruct((rows, D_MODEL), F32), jax.ShapeDtypeStruct((rows, D_MODEL), BF16)],
        compiler_params=_cparams(("arbitrary",)),
        name="mix_ln1",
    )(pooled, a, x, w_pool, pool_scale, w_out, g1, b1)


TF = 512
N_FF_CHUNKS = D_FF // TF
CONV_HALO = 8


def _silu(x):
    return x / (1.0 + jnp.exp(-x))


def _ffn_accumulate(gate, val, wd_ref, x1_ref, g_ref, b_ref, o_ref, acc_ref):
    c = pl.program_id(1)
    h = (_silu(gate) * val).astype(BF16)
    part = jnp.dot(h, wd_ref[...], preferred_element_type=F32)

    @pl.when(c == 0)
    def _():
        acc_ref[...] = part

    @pl.when(c > 0)
    def _():
        acc_ref[...] += part

    @pl.when(c == N_FF_CHUNKS - 1)
    def _():
        o_ref[...] = _layer_norm(DEEPNORM_ALPHA * x1_ref[...] + acc_ref[...], g_ref, b_ref)


def _ffn_halo_kernel(ug_ref, uv_ref, hg_ref, hv_ref, fg_ref, fv_ref, cwg_ref, cwv_ref, cbg_ref, cbv_ref,
                     wd_ref, x1_ref, g_ref, b_ref, o_ref, acc_ref, ext_scr, *, tiles_per_batch):
    i = pl.program_id(0)
    tm = ug_ref.shape[0]
    first = i % tiles_per_batch == 0

    def conv(u_ref, h_ref, f_ref, cw_ref, cb_ref):
        ext_scr[0:CONV_HALO, :] = jnp.where(first, f_ref[...], h_ref[...])
        ext_scr[CONV_HALO:, :] = u_ref[...]
        return (cb_ref[...] + cw_ref[0:1, :] * ext_scr[CONV_HALO - 2:CONV_HALO - 2 + tm, :]
                + cw_ref[1:2, :] * ext_scr[CONV_HALO - 1:CONV_HALO - 1 + tm, :]
                + cw_ref[2:3, :] * ext_scr[CONV_HALO:CONV_HALO + tm, :])

    gate = conv(ug_ref, hg_ref, fg_ref, cwg_ref, cbg_ref)
    val = conv(uv_ref, hv_ref, fv_ref, cwv_ref, cbv_ref)
    _ffn_accumulate(gate, val, wd_ref, x1_ref, g_ref, b_ref, o_ref, acc_ref)


def _ffn_prev_kernel(ug_ref, uv_ref, p1g_ref, p1v_ref, p2g_ref, p2v_ref, cwg_ref, cwv_ref, cbg_ref, cbv_ref,
                     wd_ref, x1_ref, g_ref, b_ref, o_ref, acc_ref):
    def conv(u_ref, p1_ref, p2_ref, cw_ref, cb_ref):
        return (cb_ref[...] + cw_ref[0:1, :] * p2_ref[...] + cw_ref[1:2, :] * p1_ref[...]
                + cw_ref[2:3, :] * u_ref[...])

    gate = conv(ug_ref, p1g_ref, p2g_ref, cwg_ref, cbg_ref)
    val = conv(uv_ref, p1v_ref, p2v_ref, cwv_ref, cbv_ref)
    _ffn_accumulate(gate, val, wd_ref, x1_ref, g_ref, b_ref, o_ref, acc_ref)


def _ffn_common_specs(tm):
    gate_c = lambda i, c: (0, c)
    val_c = lambda i, c: (0, N_FF_CHUNKS + c)
    return [pl.BlockSpec((3, TF), gate_c), pl.BlockSpec((3, TF), val_c),
            pl.BlockSpec((1, TF), gate_c), pl.BlockSpec((1, TF), val_c),
            pl.BlockSpec((TF, D_MODEL), lambda i, c: (c, 0)),
            pl.BlockSpec((tm, D_MODEL), lambda i, c: (i, 0)),
            pl.BlockSpec((1, D_MODEL), lambda i, c: (0, 0)),
            pl.BlockSpec((1, D_MODEL), lambda i, c: (0, 0))]


def _ffn_tail_halo(u, u_first, first_row_block, conv_w, conv_b, w_down, x1, g2, b2, *, tm, tiles_per_batch):
    rows = u.shape[0]
    kern = functools.partial(_ffn_halo_kernel, tiles_per_batch=tiles_per_batch)
    halo_per_tile = tm // CONV_HALO
    halo_row = lambda i: jnp.maximum(i * halo_per_tile - 1, 0)
    return pl.pallas_call(
        kern,
        grid=(rows // tm, N_FF_CHUNKS),
        in_specs=[pl.BlockSpec((tm, TF), lambda i, c: (i, c)),
                  pl.BlockSpec((tm, TF), lambda i, c: (i, N_FF_CHUNKS + c)),
                  pl.BlockSpec((CONV_HALO, TF), lambda i, c: (halo_row(i), c)),
                  pl.BlockSpec((CONV_HALO, TF), lambda i, c: (halo_row(i), N_FF_CHUNKS + c)),
                  pl.BlockSpec((CONV_HALO, TF), lambda i, c: (first_row_block, c)),
                  pl.BlockSpec((CONV_HALO, TF), lambda i, c: (first_row_block, N_FF_CHUNKS + c))]
                 + _ffn_common_specs(tm),
        out_specs=pl.BlockSpec((tm, D_MODEL), lambda i, c: (i, 0)),
        out_shape=jax.ShapeDtypeStruct((rows, D_MODEL), F32),
        scratch_shapes=[pltpu.VMEM((tm, D_MODEL), F32), pltpu.VMEM((tm + CONV_HALO, TF), F32)],
        compiler_params=_cparams(("arbitrary", "arbitrary")),
        name="ffn_tail",
    )(u, u, u, u, u_first, u_first, conv_w, conv_w, conv_b, conv_b, w_down, x1, g2, b2)


def _ffn_tail_prev(u, prev1, prev2, conv_w, conv_b, w_down, x1, g2, b2, *, tm):
    rows = u.shape[0]
    gate_t = lambda i, c: (i, c)
    val_t = lambda i, c: (i, N_FF_CHUNKS + c)
    return pl.pallas_call(
        _ffn_prev_kernel,
        grid=(rows // tm, N_FF_CHUNKS),
        in_specs=[pl.BlockSpec((tm, TF), gate_t), pl.BlockSpec((tm, TF), val_t),
                  pl.BlockSpec((tm, TF), gate_t), pl.BlockSpec((tm, TF), val_t),
                  pl.BlockSpec((tm, TF), gate_t), pl.BlockSpec((tm, TF), val_t)]
                 + _ffn_common_specs(tm),
        out_specs=pl.BlockSpec((tm, D_MODEL), lambda i, c: (i, 0)),
        out_shape=jax.ShapeDtypeStruct((rows, D_MODEL), F32),
        scratch_shapes=[pltpu.VMEM((tm, D_MODEL), F32)],
        compiler_params=_cparams(("arbitrary", "arbitrary")),
        name="ffn_tail_small",
    )(u, u, prev1, prev1, prev2, prev2, conv_w, conv_w, conv_b, conv_b, w_down, x1, g2, b2)


def kernel(x_prompt, x_sample, cache_k, cache_v, cache_idx_k, state_pool, state_conv, page_table, meta_tokens,
           w_in, w_pool, pool_scale, w_out, ln1_g, ln1_b, w_up, conv_w, conv_b, w_down, ln2_g, ln2_b):
    n_phys = cache_k.shape[1]

    wq, wk, wv, wiq, wik, wiw, wp = jnp.split(w_in[0], IN_OFFSETS, axis=1)
    zcols = lambda n: jnp.zeros((D_MODEL, n), F32)
    w_all = jnp.concatenate([wq, wp, wk, wv, wiq, wik, zcols(64), wiw, zcols(120)], axis=1).astype(BF16)
    w_pool_b = w_pool[0].astype(BF16)
    w_out_b = w_out[0].astype(BF16)
    w_up_b = w_up[0].astype(BF16)
    w_down_b = w_down[0].astype(BF16)
    row2 = lambda a: a.reshape(1, -1)

    pos_prompt = N_META + jnp.arange(SEQ)
    pos_small = jnp.concatenate([jnp.tile(PAST_LEN + jnp.arange(DEC_SEQ), DEC_BATCH), jnp.arange(N_META),
                                 jnp.zeros((SMALL_ROWS - N_SAMPLE_ROWS - N_META,), I32)])
    tabs_p = _rope_tables(pos_prompt, ROT_DIM, HEAD_DIM) + _rope_tables(pos_prompt, IDX_ROT_DIM, IDX_DIM)
    tabs_s = _rope_tables(pos_small, ROT_DIM, HEAD_DIM) + _rope_tables(pos_small, IDX_ROT_DIM, IDX_DIM)

    xp = x_prompt.reshape(N_PROMPT_ROWS, D_MODEL)
    xs = jnp.concatenate([x_sample.reshape(N_SAMPLE_ROWS, D_MODEL), meta_tokens.astype(F32),
                          jnp.zeros((SMALL_ROWS - N_SAMPLE_ROWS - N_META, D_MODEL), F32)], axis=0)

    tm_p = 1024
    hp = _project(xp, w_all, tabs_p, SEQ // tm_p, tm_p)
    hs = _project(xs, w_all, tabs_s, 1, SMALL_ROWS)

    meta = slice(META_ROW0, META_ROW0 + N_META)

    def with_meta(c0, width):
        m = jnp.broadcast_to(hs[meta, c0:c0 + width][None], (BATCH, N_META, width))
        return jnp.concatenate([m, hp[:, c0:c0 + width].reshape(BATCH, SEQ, width)], axis=1)

    k_prompt = with_meta(C_K, KV_WIDTH)
    v_prompt = with_meta(C_V, KV_WIDTH)
    ik_prompt = with_meta(C_IK, IDX_DIM)
    pad_keys = lambda a: jnp.pad(a.astype(BF16), ((0, 0), (0, NK_PROMPT - T_PROMPT), (0, 0)))
    kb, vb, ikb = pad_keys(k_prompt), pad_keys(v_prompt), pad_keys(ik_prompt)

    a_p = _prompt_attention(hp, kb, vb, ikb, n_blocks=N_PROMPT_ROWS // LANES, row_block0=0,
                            blocks_per_batch=SEQ // LANES, pos_base=N_META, nk=NK_PROMPT)
    a_m = _prompt_attention(hs, kb, vb, ikb, n_blocks=1, row_block0=META_ROW0 // LANES,
                            blocks_per_batch=1, pos_base=0, nk=LANES)

    hs_s = hs[:N_SAMPLE_ROWS]
    iq_s = hs_s[:, C_IQ:C_IQ + N_IDX_HEADS * IDX_DIM].reshape(DEC_BATCH, DEC_SEQ, N_IDX_HEADS, IDX_DIM)
    iq_s = jnp.pad(iq_s.transpose(0, 2, 1, 3), ((0, 0), (0, 0), (0, 8 - DEC_SEQ), (0, 0)))
    iq_s = iq_s.reshape(DEC_BATCH, N_IDX_HEADS * 8, IDX_DIM)
    w_s = hs_s[:, C_IW:C_IW + N_IDX_HEADS].reshape(DEC_BATCH, DEC_SEQ, N_IDX_HEADS)
    w_s = jnp.pad(w_s.transpose(0, 2, 1), ((0, 0), (0, 0), (0, 8 - DEC_SEQ))).reshape(DEC_BATCH, 64, 1)

    def new_rows(c0, width):
        a = hs_s[:, c0:c0 + width].reshape(DEC_BATCH, DEC_SEQ, width)
        return jnp.pad(a, ((0, 0), (0, LANES - DEC_SEQ), (0, 0)))

    pt_pad = jnp.pad(page_table, ((0, 0), (0, N_MROWS - N_PAGES)))
    pt_hi = jnp.broadcast_to((pt_pad // 64).astype(F32)[:, None, :], (DEC_BATCH, 8, N_MROWS))
    pt_lo = jnp.broadcast_to((pt_pad % 64).astype(F32)[:, None, :], (DEC_BATCH, 8, N_MROWS))
    rows_sel, gbias, nbias = _sample_select(page_table, iq_s, w_s, new_rows(C_IK, IDX_DIM), pt_hi, pt_lo,
                                            cache_idx_k[0])
    q_s = hs_s[:, C_Q:C_Q + ATTN_WIDTH].reshape(DEC_BATCH, DEC_SEQ, N_HEADS, 1, HEAD_DIM)
    group_of_head = jnp.arange(N_HEADS) // (N_HEADS // N_KV_HEADS)
    head_mask = (group_of_head[:, None] == jnp.arange(N_KV_HEADS)[None, :]).astype(F32)
    qbd = (q_s * head_mask[None, None, :, :, None]).reshape(DEC_BATCH, DEC_SEQ, N_HEADS, KV_WIDTH)
    a_s8 = _sample_attention(rows_sel.reshape(-1), gbias, nbias, qbd, new_rows(C_K, KV_WIDTH),
                             new_rows(C_V, KV_WIDTH), cache_k[0].reshape(n_phys * PAGE_SIZE, KV_WIDTH),
                             cache_v[0].reshape(n_phys * PAGE_SIZE, KV_WIDTH))
    a_s = a_s8[:, :DEC_SEQ].reshape(N_SAMPLE_ROWS, ATTN_WIDTH).astype(BF16)
    a_small = jnp.concatenate([a_s, a_m[:N_META], jnp.zeros((SMALL_ROWS - N_SAMPLE_ROWS - N_META, ATTN_WIDTH),
                                                            BF16)], axis=0)

    win = jnp.repeat(jnp.asarray(POOL_WINDOWS, F32), POOL_GROUP)
    p_meta = hs[meta, C_P:C_P + POOL_WIDTH]
    tm_pool = 256
    pooled_p = _pool(hp, C_P // POOL_WIDTH, hp, hs, (META_ROW0 // N_META, C_P // POOL_WIDTH), row2(win),
                     rows=N_PROMPT_ROWS, tm=tm_pool, tiles_per_batch=SEQ // tm_pool)
    p_s = hs_s[:, C_P:C_P + POOL_WIDTH].reshape(DEC_BATCH, DEC_SEQ, POOL_WIDTH)
    grp = 24
    ext_s = jnp.concatenate([jnp.zeros((DEC_BATCH, 1, POOL_WIDTH), F32), state_pool[0], p_s,
                             jnp.zeros((DEC_BATCH, grp - 1 - POOL_CTX - DEC_SEQ, POOL_WIDTH), F32)], axis=1)
    ext_small = jnp.concatenate([ext_s.reshape(DEC_BATCH * grp, POOL_WIDTH),
                                 jnp.zeros((N_META, POOL_WIDTH), F32), p_meta], axis=0)
    n_ext = DEC_BATCH * grp + 2 * N_META
    cnt_meta = jnp.minimum(win[None, :], (jnp.arange(N_META, dtype=F32) + 1.0)[:, None])
    cnt_small = jnp.concatenate([jnp.broadcast_to(win[None], (n_ext - N_META, POOL_WIDTH)), cnt_meta], axis=0)
    zeros_halo = jnp.zeros((N_META, POOL_WIDTH), F32)
    pooled_ext = _pool(ext_small, 0, zeros_halo, zeros_halo, (0, 0), cnt_small,
                       rows=n_ext, tm=n_ext, tiles_per_batch=1)
    pooled_small = jnp.concatenate(
        [pooled_ext[:DEC_BATCH * grp].reshape(DEC_BATCH, grp, POOL_WIDTH)[:, 16:16 + DEC_SEQ].reshape(
            N_SAMPLE_ROWS, POOL_WIDTH),
         pooled_ext[n_ext - N_META:],
         jnp.zeros((SMALL_ROWS - N_SAMPLE_ROWS - N_META, POOL_WIDTH), BF16)], axis=0)

    mix_args = (w_pool_b, row2(pool_scale[0]), w_out_b, row2(ln1_g[0]), row2(ln1_b[0]))
    x1_p, x1b_p = _mix(pooled_p, a_p, xp, *mix_args, tm=256)
    x1_s, x1b_s = _mix(pooled_small, a_small, xs, *mix_args, tm=SMALL_ROWS)

    u_p = _matmul(x1b_p, w_up_b, 1024, "ffn_up")
    u_s = _matmul(x1b_s, w_up_b, SMALL_ROWS, "ffn_up_small")
    ffn_args = (conv_w[0], row2(conv_b[0]), w_down_b)
    ln2 = (row2(ln2_g[0]), row2(ln2_b[0]))
    tm_f = 512
    y_p = _ffn_tail_halo(u_p, u_s, (META_ROW0 + N_META) // CONV_HALO - 1, *ffn_args, x1_p, *ln2,
                         tm=tm_f, tiles_per_batch=SEQ // tm_f)
    u_s3 = u_s[:N_SAMPLE_ROWS].reshape(DEC_BATCH, DEC_SEQ, 2 * D_FF)
    ext_u = jnp.concatenate([state_conv[0], u_s3], axis=1)
    u_m = u_s[meta]
    ext_m = jnp.concatenate([jnp.zeros((2, 2 * D_FF), F32), u_m], axis=0)
    tail0 = jnp.zeros((SMALL_ROWS - N_SAMPLE_ROWS - N_META, 2 * D_FF), F32)
    prev1 = jnp.concatenate([ext_u[:, 1:1 + DEC_SEQ].reshape(N_SAMPLE_ROWS, -1), ext_m[1:1 + N_META], tail0], 0)
    prev2 = jnp.concatenate([ext_u[:, 0:DEC_SEQ].reshape(N_SAMPLE_ROWS, -1), ext_m[0:N_META], tail0], 0)
    y_s = _ffn_tail_prev(u_s, prev1, prev2, *ffn_args, x1_s, *ln2, tm=SMALL_ROWS)

    y_prompt = y_p.reshape(BATCH, SEQ, D_MODEL)
    y_sample = y_s[:N_SAMPLE_ROWS].reshape(DEC_BATCH, DEC_SEQ, D_MODEL)
    pool_prompt = hp[:, C_P:C_P + POOL_WIDTH].reshape(BATCH, SEQ, POOL_WIDTH)[:, SEQ - POOL_CTX:]
    conv_prompt = u_p.reshape(BATCH, SEQ, 2 * D_FF)[:, SEQ - 2:]
    k_sample = hs_s[:, C_K:C_K + KV_WIDTH].reshape(DEC_BATCH, DEC_SEQ, N_KV_HEADS, HEAD_DIM)
    v_sample = hs_s[:, C_V:C_V + KV_WIDTH].reshape(DEC_BATCH, DEC_SEQ, N_KV_HEADS, HEAD_DIM)
    ik_sample = hs_s[:, C_IK:C_IK + IDX_DIM].reshape(DEC_BATCH, DEC_SEQ, IDX_DIM)
    pool_sample = jnp.concatenate([state_pool[0], p_s], axis=1)[:, DEC_SEQ:]
    conv_sample = ext_u[:, DEC_SEQ:]
    return (y_prompt, y_sample,
            k_prompt.reshape(1, BATCH, T_PROMPT, N_KV_HEADS, HEAD_DIM),
            v_prompt.reshape(1, BATCH, T_PROMPT, N_KV_HEADS, HEAD_DIM),
            ik_prompt[None], pool_prompt[None], conv_prompt[None],
            k_sample[None], v_sample[None], ik_sample[None], pool_sample[None], conv_sample[None])
```

```python
import functools

import numpy as np
import jax
import jax.numpy as jnp
from jax import lax
from jax.experimental import pallas as pl
from jax.experimental.pallas import tpu as pltpu

F32 = jnp.float32
BF16 = jnp.bfloat16
I32 = jnp.int32

D_MODEL = 2048
BATCH = 8
SEQ = 2048
DEC_BATCH = 32
DEC_SEQ = 4
PAST_LEN = 16384
PAGE_SIZE = 128
N_META = 16
ATTN_WIDTH = 1024
N_HEADS = 8
HEAD_DIM = 128
N_KV_HEADS = 4
KV_WIDTH = N_KV_HEADS * HEAD_DIM
ROT_DIM = 32
N_IDX_HEADS = 8
IDX_DIM = 64
IDX_ROT_DIM = 16
TOPK = 256
POOL_WIDTH = 1024
POOL_WINDOWS = (2, 4, 8, 16)
POOL_GROUP = 256
POOL_CTX = 15
D_FF = 5632
ROPE_THETA = 500000.0
LN_EPS = 1e-5
DEEPNORM_ALPHA = 2.0 ** 0.25
IN_OFFSETS = (1024, 1536, 2048, 2560, 2624, 2632)

LANES = 128
N_PROMPT_ROWS = BATCH * SEQ
N_SAMPLE_ROWS = DEC_BATCH * DEC_SEQ
SMALL_ROWS = 256
META_ROW0 = N_SAMPLE_ROWS
T_PROMPT = N_META + SEQ
NK_PROMPT = 2176
NK_SAMPLE = PAST_LEN + LANES
N_PAGES = PAST_LEN // PAGE_SIZE

C_Q, C_P, C_K, C_V, C_IQ, C_IK, C_IW = 0, 1024, 2048, 2560, 3072, 3584, 3712
PROJ_WIDTH = 3840
TN = 256
ROPE_A_TILES = (0, 1, 2, 3, 8, 9)
ROPE_B_TILES = (12, 13)
ROPE_BH_TILE = 14

VMEM_LIMIT = 56 * 1024 * 1024
INT_MIN = -2 ** 31
NEG_INF = float("-inf")
LOG2_E = 1.4426950408889634


def _cparams(sem, **kw):
    return pltpu.CompilerParams(dimension_semantics=sem, vmem_limit_bytes=VMEM_LIMIT, **kw)


def _rope_tables(pos, rot_dim, head_dim):
    half = rot_dim // 2
    inv_freq = ROPE_THETA ** (-jnp.arange(half, dtype=F32) / half)
    ang = pos.astype(F32)[:, None] * inv_freq[None, :]
    cos, sin = jnp.cos(ang), jnp.sin(ang)
    rows = pos.shape[0]
    zh = jnp.zeros((rows, half), F32)
    rest0 = jnp.zeros((rows, head_dim - rot_dim), F32)
    c = jnp.concatenate([cos, cos, rest0 + 1.0], axis=1)
    s1 = jnp.concatenate([-sin, zh, rest0], axis=1)
    s2 = jnp.concatenate([zh, sin, rest0], axis=1)
    reps = LANES // head_dim
    return tuple(jnp.tile(t, (1, reps)) for t in (c, s1, s2))


def _rope(x, c_ref, s1_ref, s2_ref, half):
    return (x * c_ref[...] + pltpu.roll(x, LANES - half, 1) * s1_ref[...]
            + pltpu.roll(x, half, 1) * s2_ref[...])


def _proj_kernel(x_ref, w_ref, ca, sa1, sa2, cb, sb1, sb2, o_ref, xb_ref):
    j = pl.program_id(1)

    @pl.when(j == 0)
    def _():
        xb_ref[...] = x_ref[...].astype(BF16)

    acc = jnp.dot(xb_ref[...], w_ref[...], preferred_element_type=F32)
    is_a = functools.reduce(jnp.logical_or, [j == t for t in ROPE_A_TILES])
    is_b = functools.reduce(jnp.logical_or, [j == t for t in ROPE_B_TILES])
    is_bh = j == ROPE_BH_TILE

    @pl.when(is_a)
    def _():
        o_ref[:, :LANES] = _rope(acc[:, :LANES], ca, sa1, sa2, ROT_DIM // 2)
        o_ref[:, LANES:] = _rope(acc[:, LANES:], ca, sa1, sa2, ROT_DIM // 2)

    @pl.when(is_b)
    def _():
        o_ref[:, :LANES] = _rope(acc[:, :LANES], cb, sb1, sb2, IDX_ROT_DIM // 2)
        o_ref[:, LANES:] = _rope(acc[:, LANES:], cb, sb1, sb2, IDX_ROT_DIM // 2)

    @pl.when(is_bh)
    def _():
        o_ref[:, :LANES] = _rope(acc[:, :LANES], cb, sb1, sb2, IDX_ROT_DIM // 2)
        o_ref[:, LANES:] = acc[:, LANES:]

    @pl.when(jnp.logical_not(is_a | is_b | is_bh))
    def _():
        o_ref[...] = acc


def _project(x, w_all, tabs, tab_blocks, tm):
    rows = x.shape[0]
    tab_spec = pl.BlockSpec((tm, LANES), lambda i, j: (i % tab_blocks, 0))
    return pl.pallas_call(
        _proj_kernel,
        grid=(rows // tm, PROJ_WIDTH // TN),
        in_specs=[pl.BlockSpec((tm, D_MODEL), lambda i, j: (i, 0)),
                  pl.BlockSpec((D_MODEL, TN), lambda i, j: (0, j))] + [tab_spec] * 6,
        out_specs=pl.BlockSpec((tm, TN), lambda i, j: (i, j)),
        out_shape=jax.ShapeDtypeStruct((rows, PROJ_WIDTH), F32),
        scratch_shapes=[pltpu.VMEM((tm, D_MODEL), BF16)],
        compiler_params=_cparams(("arbitrary", "arbitrary")),
        name="in_proj",
    )(x, w_all, *tabs)


def _mm_kernel(x_ref, w_ref, o_ref):
    o_ref[...] = jnp.dot(x_ref[...], w_ref[...], preferred_element_type=F32)


def _matmul(x, w, tm, name):
    rows, kdim = x.shape
    n = w.shape[1]
    return pl.pallas_call(
        _mm_kernel,
        grid=(rows // tm, n // TN),
        in_specs=[pl.BlockSpec((tm, kdim), lambda i, j: (i, 0)),
                  pl.BlockSpec((kdim, TN), lambda i, j: (0, j))],
        out_specs=pl.BlockSpec((tm, TN), lambda i, j: (i, j)),
        out_shape=jax.ShapeDtypeStruct((rows, n), F32),
        compiler_params=_cparams(("arbitrary", "arbitrary")),
        name=name,
    )(x, w)


def _ordinal_to_float(key):
    bits = key ^ ((key >> 31) & 0x7FFFFFFF)
    return lax.bitcast_convert_type(bits, F32)


def _select_topk(sc_ref, col, valid, n_col_bits, cut_scr):
    rows = sc_ref.shape[0]
    k_f = float(TOPK)

    def count(pred):
        return jnp.sum(jnp.where(pred, 1.0, 0.0), axis=1, keepdims=True)

    few = count(valid) <= k_f
    key0 = jnp.where(count(sc_ref[...] >= 0.0) >= k_f, 0, INT_MIN).astype(I32)

    def value_step(i, key):
        cand = key + jnp.left_shift(jnp.int32(1), 30 - i)
        cnt = count(sc_ref[...] >= _ordinal_to_float(cand))
        return jnp.where(cnt >= k_f, cand, key)

    key = lax.fori_loop(0, 31, value_step, key0)
    thr = jnp.where(few, NEG_INF, _ordinal_to_float(key))
    need = k_f - count(sc_ref[...] > thr)
    surplus = jnp.where(few, 0.0, count(sc_ref[...] >= thr) - k_f)

    cut_scr[...] = jnp.full(cut_scr.shape, 2 ** n_col_bits - 1, I32)

    @pl.when(jnp.max(surplus) > 0.0)
    def _():
        def col_step(i, cut):
            cand = cut + jnp.left_shift(jnp.int32(1), n_col_bits - 1 - i)
            cnt = count((sc_ref[...] == thr) & (col < cand))
            return jnp.where(cnt < need, cand, cut)

        cut_scr[...] = lax.fori_loop(0, n_col_bits, col_step, jnp.zeros((rows, 1), I32))

    sc = sc_ref[...]
    return valid & ((sc > thr) | ((sc == thr) & (col <= cut_scr[...])))


def _pattn_kernel(q_ref, iq_ref, misc_ref, kb_ref, vb_ref, ikb_ref, o_ref, sc_scr, bias_scr, cut_scr,
                  *, pos_first, nk):
    pos0 = pos_first + LANES * pl.program_id(1)
    tq = q_ref.shape[0]

    ik = ikb_ref[0]
    w = misc_ref[:, LANES:LANES + N_IDX_HEADS] * (N_IDX_HEADS ** -0.5)
    score = jnp.zeros((tq, nk), F32)
    for h in range(N_IDX_HEADS):
        iq_h = iq_ref[:, h * IDX_DIM:(h + 1) * IDX_DIM].astype(BF16)
        s_h = lax.dot_general(iq_h, ik, (((1,), (1,)), ((), ())), preferred_element_type=F32)
        score = score + jnp.maximum(s_h * (IDX_DIM ** -0.5), 0.0) * w[:, h:h + 1]

    qpos = pos0 + lax.broadcasted_iota(I32, (tq, nk), 0)
    col = lax.broadcasted_iota(I32, (tq, nk), 1)
    causal = col <= qpos
    sc_scr[...] = jnp.where(causal, score, NEG_INF)
    sel = _select_topk(sc_scr, col, causal, 12, cut_scr)
    bias_scr[...] = jnp.where(sel, 0.0, NEG_INF)

    q_scale = (HEAD_DIM ** -0.5) * LOG2_E
    for h in range(N_HEADS):
        kv = h // (N_HEADS // N_KV_HEADS)
        q_h = (q_ref[:, h * HEAD_DIM:(h + 1) * HEAD_DIM] * q_scale).astype(BF16)
        k_g = kb_ref[0, :, kv * HEAD_DIM:(kv + 1) * HEAD_DIM]
        s = lax.dot_general(q_h, k_g, (((1,), (1,)), ((), ())), preferred_element_type=F32)
        s = s + bias_scr[...]
        m = jnp.max(s, axis=1, keepdims=True)
        p = jnp.exp2(s - m)
        l = jnp.sum(p, axis=1, keepdims=True)
        v_g = vb_ref[0, :, kv * HEAD_DIM:(kv + 1) * HEAD_DIM]
        o = jnp.dot(p.astype(BF16), v_g, preferred_element_type=F32) / l
        o_ref[0, :, h * HEAD_DIM:(h + 1) * HEAD_DIM] = o.astype(o_ref.dtype)


def _prompt_attention(h, kb, vb, ikb, *, n_batch, n_blocks, row_block0, blocks_per_batch, pos_first, nk):
    kern = functools.partial(_pattn_kernel, pos_first=pos_first, nk=nk)
    qrow = lambda b, j: row_block0 + b * blocks_per_batch + j
    return pl.pallas_call(
        kern,
        grid=(n_batch, n_blocks),
        in_specs=[pl.BlockSpec((LANES, ATTN_WIDTH), lambda b, j: (qrow(b, j), C_Q // ATTN_WIDTH)),
                  pl.BlockSpec((LANES, 512), lambda b, j: (qrow(b, j), C_IQ // 512)),
                  pl.BlockSpec((LANES, 256), lambda b, j: (qrow(b, j), C_IK // 256)),
                  pl.BlockSpec((1, nk, KV_WIDTH), lambda b, j: (b, 0, 0)),
                  pl.BlockSpec((1, nk, KV_WIDTH), lambda b, j: (b, 0, 0)),
                  pl.BlockSpec((1, nk, IDX_DIM), lambda b, j: (b, 0, 0))],
        out_specs=pl.BlockSpec((1, LANES, ATTN_WIDTH), lambda b, j: (b, j, 0)),
        out_shape=jax.ShapeDtypeStruct((n_batch, n_blocks * LANES, ATTN_WIDTH), BF16),
        scratch_shapes=[pltpu.VMEM((LANES, nk), F32), pltpu.VMEM((LANES, nk), F32),
                        pltpu.VMEM((LANES, 1), I32)],
        compiler_params=_cparams(("arbitrary", "arbitrary")),
        name="prompt_attention",
    )(h, h, h, kb, vb, ikb)


N_MROWS = 256


def _ssel_kernel(pt_ref, iq_ref, w_ref, iknew_ref, pthi_ref, ptlo_ref, cache_ref,
                 rows_ref, gbias_ref, nbias_ref, ikbuf, sem, score_scr, msc_scr, sel_scr, m_scr, cut_scr):
    b = pl.program_id(0)

    def issue(p, carry):
        pltpu.make_async_copy(cache_ref.at[pt_ref[b, p]], ikbuf.at[p], sem).start()
        return carry

    lax.fori_loop(0, N_PAGES, issue, 0)
    pltpu.make_async_copy(cache_ref.at[pl.ds(0, N_PAGES)], ikbuf, sem).wait()

    iq = iq_ref[0].astype(BF16)
    w = w_ref[0] * (N_IDX_HEADS ** -0.5)

    def score_of(keys):
        s = lax.dot_general(iq, keys.astype(BF16), (((1,), (1,)), ((), ())), preferred_element_type=F32)
        r = jnp.maximum(s * (IDX_DIM ** -0.5), 0.0) * w
        acc = r[0:8]
        for h in range(1, N_IDX_HEADS):
            acc = acc + r[8 * h:8 * h + 8]
        return acc

    chunk_pages = 8
    for c in range(N_PAGES // chunk_pages):
        keys = ikbuf[c * chunk_pages:(c + 1) * chunk_pages].reshape(chunk_pages * PAGE_SIZE, IDX_DIM)
        score_scr[:, c * 1024:(c + 1) * 1024] = score_of(keys)
    score_scr[:, PAST_LEN:] = score_of(iknew_ref[0])

    qrow = lax.broadcasted_iota(I32, (8, NK_SAMPLE), 0)
    col = lax.broadcasted_iota(I32, (8, NK_SAMPLE), 1)
    valid = col <= PAST_LEN + qrow
    msc_scr[...] = jnp.where(valid, score_scr[...], NEG_INF)
    sel = _select_topk(msc_scr, col, valid, 15, cut_scr)
    sel_scr[...] = jnp.where(sel, 1.0, 0.0)

    nt = (((1,), (1,)), ((), ()))
    ones_b = jnp.ones((8, LANES), BF16)
    upper_pages = (lax.broadcasted_iota(I32, (N_MROWS, N_MROWS), 0)
                   < lax.broadcasted_iota(I32, (N_MROWS, N_MROWS), 1)).astype(BF16)
    upper_lanes = (lax.broadcasted_iota(I32, (LANES, LANES), 0)
                   < lax.broadcasted_iota(I32, (LANES, LANES), 1)).astype(BF16)
    lane_id = lax.broadcasted_iota(I32, (8, LANES), 1).astype(BF16)
    page_id = lax.broadcasted_iota(I32, (8, N_MROWS), 1).astype(BF16)
    jcol = lax.broadcasted_iota(I32, (TOPK, 1), 0).astype(F32)
    pt_hi = pthi_ref[0].astype(BF16)
    pt_lo = ptlo_ref[0].astype(BF16)

    rows_ref[...] = jnp.zeros(rows_ref.shape, I32)
    gbias_ref[...] = jnp.full(gbias_ref.shape, NEG_INF, F32)
    nbias_ref[...] = jnp.full(nbias_ref.shape, NEG_INF, F32)
    m_scr[...] = jnp.zeros(m_scr.shape, F32)
    for q in range(DEC_SEQ):
        for c in range(N_PAGES + 1):
            m_scr[c:c + 1, :] = sel_scr[q:q + 1, c * LANES:(c + 1) * LANES]
        m_b = m_scr[...].astype(BF16)
        cnt = lax.dot_general(ones_b, m_b, nt, preferred_element_type=F32)
        start = jnp.dot(cnt.astype(BF16), upper_pages, preferred_element_type=F32)
        cnt1, start1 = cnt[0:1], start[0:1]
        in_page = (start1 <= jcol) & (jcol < start1 + cnt1)
        a_b = jnp.where(in_page, 1.0, 0.0).astype(BF16)
        m_j = jnp.dot(a_b, m_b, preferred_element_type=F32)
        rank = jnp.dot(m_j.astype(BF16), upper_lanes, preferred_element_type=F32)
        start_j = jnp.sum(jnp.where(in_page, start1, 0.0), axis=1, keepdims=True)
        onehot = jnp.where((m_j > 0.5) & (rank == jcol - start_j), 1.0, 0.0).astype(BF16)
        off = lax.dot_general(lane_id, onehot, nt, preferred_element_type=F32)[0:1]
        page = lax.dot_general(page_id, a_b, nt, preferred_element_type=F32)[0:1]
        phys = (lax.dot_general(pt_hi, a_b, nt, preferred_element_type=F32)[0:1] * 64.0
                + lax.dot_general(pt_lo, a_b, nt, preferred_element_type=F32)[0:1])
        in_past = page < float(N_PAGES)
        row = jnp.where(in_past, phys * float(PAGE_SIZE) + off, 0.0).astype(I32)
        rows_ref[0, q:q + 1, :] = row
        gbias_ref[0, q:q + 1, :] = jnp.where(in_past, 0.0, NEG_INF)
        nbias_ref[0, q:q + 1, :] = jnp.where(sel_scr[q:q + 1, PAST_LEN:] > 0.5, 0.0, NEG_INF)


def _sample_select(page_table, iq_s, w_s, ik_new, pt_hi, pt_lo, cache_idx):
    grid_spec = pltpu.PrefetchScalarGridSpec(
        num_scalar_prefetch=1,
        grid=(DEC_BATCH,),
        in_specs=[pl.BlockSpec((1, 64, IDX_DIM), lambda b, pt: (b, 0, 0)),
                  pl.BlockSpec((1, 64, 1), lambda b, pt: (b, 0, 0)),
                  pl.BlockSpec((1, LANES, IDX_DIM), lambda b, pt: (b, 0, 0)),
                  pl.BlockSpec((1, 8, N_MROWS), lambda b, pt: (b, 0, 0)),
                  pl.BlockSpec((1, 8, N_MROWS), lambda b, pt: (b, 0, 0)),
                  pl.BlockSpec(memory_space=pl.ANY)],
        out_specs=[pl.BlockSpec((1, 8, TOPK), lambda b, pt: (b, 0, 0)),
                   pl.BlockSpec((1, 8, TOPK), lambda b, pt: (b, 0, 0)),
                   pl.BlockSpec((1, 8, LANES), lambda b, pt: (b, 0, 0))],
        scratch_shapes=[pltpu.VMEM((N_PAGES, PAGE_SIZE, IDX_DIM), F32),
                        pltpu.SemaphoreType.DMA(()),
                        pltpu.VMEM((8, NK_SAMPLE), F32),
                        pltpu.VMEM((8, NK_SAMPLE), F32),
                        pltpu.VMEM((8, NK_SAMPLE), F32),
                        pltpu.VMEM((N_MROWS, LANES), F32),
                        pltpu.VMEM((8, 1), I32)])
    return pl.pallas_call(
        _ssel_kernel,
        grid_spec=grid_spec,
        out_shape=[jax.ShapeDtypeStruct((DEC_BATCH, 8, TOPK), I32),
                   jax.ShapeDtypeStruct((DEC_BATCH, 8, TOPK), F32),
                   jax.ShapeDtypeStruct((DEC_BATCH, 8, LANES), F32)],
        compiler_params=_cparams(("arbitrary",)),
        name="sample_select",
    )(page_table, iq_s, w_s, ik_new, pt_hi, pt_lo, cache_idx)


def _sattn_kernel(rows_ref, gbias_ref, nbias_ref, qbd_ref, knew_ref, vnew_ref, ck_ref, cv_ref, o_ref,
                  kbuf, vbuf, sem):
    b = pl.program_id(0)
    n_rows = DEC_SEQ * TOPK

    def issue(t, carry):
        q = t // TOPK
        r = t - q * TOPK
        row = rows_ref[(b * 8 + q) * TOPK + r]
        pltpu.make_async_copy(ck_ref.at[pl.ds(row, 1)], kbuf.at[pl.ds(t, 1)], sem.at[0]).start()
        pltpu.make_async_copy(cv_ref.at[pl.ds(row, 1)], vbuf.at[pl.ds(t, 1)], sem.at[1]).start()
        return carry

    lax.fori_loop(0, n_rows, issue, 0)
    pltpu.make_async_copy(ck_ref.at[pl.ds(0, n_rows)], kbuf, sem.at[0]).wait()
    pltpu.make_async_copy(cv_ref.at[pl.ds(0, n_rows)], vbuf, sem.at[1]).wait()

    nt = (((1,), (1,)), ((), ()))
    scale = HEAD_DIM ** -0.5
    k_new = knew_ref[0].astype(BF16)
    v_new = vnew_ref[0].astype(BF16)
    o_ref[...] = jnp.zeros(o_ref.shape, o_ref.dtype)
    heads_per_kv = N_HEADS // N_KV_HEADS
    for q in range(DEC_SEQ):
        q_bd = qbd_ref[0, q].astype(BF16)
        picks = pl.ds(q * TOPK, TOPK)
        s = gbias_ref[0, q:q + 1, :]
        for kv in range(N_KV_HEADS):
            cols = slice(kv * HEAD_DIM, (kv + 1) * HEAD_DIM)
            s = s + lax.dot_general(q_bd[:, cols], kbuf[picks, kv, :].astype(BF16), nt,
                                    preferred_element_type=F32) * scale
        sn = lax.dot_general(q_bd, k_new, nt, preferred_element_type=F32) * scale + nbias_ref[0, q:q + 1, :]
        m = jnp.maximum(jnp.max(s, axis=1, keepdims=True), jnp.max(sn, axis=1, keepdims=True))
        p = jnp.exp(s - m)
        pn = jnp.exp(sn - m)
        l = jnp.sum(p, axis=1, keepdims=True) + jnp.sum(pn, axis=1, keepdims=True)
        p_b, pn_b = p.astype(BF16), pn.astype(BF16)
        for kv in range(N_KV_HEADS):
            cols = slice(kv * HEAD_DIM, (kv + 1) * HEAD_DIM)
            o = (jnp.dot(p_b, vbuf[picks, kv, :].astype(BF16), preferred_element_type=F32)
                 + jnp.dot(pn_b, v_new[:, cols], preferred_element_type=F32)) / l
            for h in range(kv * heads_per_kv, (kv + 1) * heads_per_kv):
                o_ref[0, q:q + 1, h * HEAD_DIM:(h + 1) * HEAD_DIM] = o[h:h + 1, :]


def _sample_attention(rows_flat, gbias, nbias, qbd, k_new, v_new, cache_k3d, cache_v3d):
    grid_spec = pltpu.PrefetchScalarGridSpec(
        num_scalar_prefetch=1,
        grid=(DEC_BATCH,),
        in_specs=[pl.BlockSpec((1, 8, TOPK), lambda b, r: (b, 0, 0)),
                  pl.BlockSpec((1, 8, LANES), lambda b, r: (b, 0, 0)),
                  pl.BlockSpec((1, DEC_SEQ, N_HEADS, KV_WIDTH), lambda b, r: (b, 0, 0, 0)),
                  pl.BlockSpec((1, LANES, KV_WIDTH), lambda b, r: (b, 0, 0)),
                  pl.BlockSpec((1, LANES, KV_WIDTH), lambda b, r: (b, 0, 0)),
                  pl.BlockSpec(memory_space=pl.ANY),
                  pl.BlockSpec(memory_space=pl.ANY)],
        out_specs=pl.BlockSpec((1, 8, ATTN_WIDTH), lambda b, r: (b, 0, 0)),
        scratch_shapes=[pltpu.VMEM((DEC_SEQ * TOPK, N_KV_HEADS, HEAD_DIM), F32),
                        pltpu.VMEM((DEC_SEQ * TOPK, N_KV_HEADS, HEAD_DIM), F32),
                        pltpu.SemaphoreType.DMA((2,))])
    return pl.pallas_call(
        _sattn_kernel,
        grid_spec=grid_spec,
        out_shape=jax.ShapeDtypeStruct((DEC_BATCH, 8, ATTN_WIDTH), F32),
        compiler_params=_cparams(("arbitrary",), disable_bounds_checks=True),
        name="sample_attention",
    )(rows_flat, gbias, nbias, qbd, k_new, v_new, cache_k3d, cache_v3d)


def _pool_kernel(p_ref, halo_ref, first_ref, cnt_ref, o_ref, ext_scr, *, tiles_per_batch):
    i = pl.program_id(0)
    tm = p_ref.shape[0]
    ext_scr[0:N_META, :] = jnp.where(i % tiles_per_batch == 0, first_ref[...], halo_ref[...])
    ext_scr[N_META:, :] = p_ref[...]
    for g, win in enumerate(POOL_WINDOWS):
        cols = slice(g * POOL_GROUP, (g + 1) * POOL_GROUP)
        x_self = ext_scr[N_META:N_META + tm, cols]
        acc = x_self
        for d in range(1, win):
            acc = acc + ext_scr[N_META - d:N_META - d + tm, cols]
        o_ref[:, cols] = (acc / cnt_ref[:, cols] - x_self).astype(o_ref.dtype)


def _pool(p_src, p_col_block, halo_src, first_src, first_block, cnt, *, rows, tm, tiles_per_batch):
    kern = functools.partial(_pool_kernel, tiles_per_batch=tiles_per_batch)
    halo_per_tile = tm // N_META
    cnt_rows = cnt.shape[0]
    cnt_map = (lambda i: (0, 0)) if cnt_rows == 1 else (lambda i: (i, 0))
    return pl.pallas_call(
        kern,
        grid=(rows // tm,),
        in_specs=[pl.BlockSpec((tm, POOL_WIDTH), lambda i: (i, p_col_block)),
                  pl.BlockSpec((N_META, POOL_WIDTH),
                               lambda i: (jnp.maximum(i * halo_per_tile - 1, 0), p_col_block)),
                  pl.BlockSpec((N_META, POOL_WIDTH), lambda i: first_block),
                  pl.BlockSpec((1 if cnt_rows == 1 else tm, POOL_WIDTH), cnt_map)],
        out_specs=pl.BlockSpec((tm, POOL_WIDTH), lambda i: (i, 0)),
        out_shape=jax.ShapeDtypeStruct((rows, POOL_WIDTH), BF16),
        scratch_shapes=[pltpu.VMEM((tm + N_META, POOL_WIDTH), F32)],
        compiler_params=_cparams(("arbitrary",)),
        name="pool",
    )(p_src, halo_src, first_src, cnt)


def _layer_norm(y, g_ref, b_ref):
    mu = jnp.mean(y, axis=-1, keepdims=True)
    var = jnp.mean(jnp.square(y - mu), axis=-1, keepdims=True)
    return (y - mu) * lax.rsqrt(var + LN_EPS) * g_ref[...] + b_ref[...]


def _mix_kernel(pooled_ref, a_ref, x_ref, wp_ref, ps_ref, wo_ref, g_ref, b_ref, x1_ref, x1b_ref):
    parts = []
    for g in range(len(POOL_WINDOWS)):
        cols = slice(g * POOL_GROUP, (g + 1) * POOL_GROUP)
        parts.append(jnp.dot(pooled_ref[:, cols], wp_ref[g], preferred_element_type=F32))
    m = jnp.concatenate(parts, axis=1) * ps_ref[...]
    mix = (jnp.dot(a_ref[...], wo_ref[:ATTN_WIDTH, :], preferred_element_type=F32)
           + jnp.dot(m.astype(BF16), wo_ref[ATTN_WIDTH:, :], preferred_element_type=F32))
    x1 = _layer_norm(DEEPNORM_ALPHA * x_ref[...] + mix, g_ref, b_ref)
    x1_ref[...] = x1
    x1b_ref[...] = x1.astype(BF16)


def _mix(pooled, a, x, w_pool, pool_scale, w_out, g1, b1, tm):
    rows = x.shape[0]
    row = lambda i: (i, 0)
    const2 = lambda i: (0, 0)
    return pl.pallas_call(
        _mix_kernel,
        grid=(rows // tm,),
        in_specs=[pl.BlockSpec((tm, POOL_WIDTH), row),
                  pl.BlockSpec((tm, ATTN_WIDTH), row),
                  pl.BlockSpec((tm, D_MODEL), row),
                  pl.BlockSpec((len(POOL_WINDOWS), POOL_GROUP, POOL_GROUP), lambda i: (0, 0, 0)),
                  pl.BlockSpec((1, POOL_WIDTH), const2),
                  pl.BlockSpec((D_MODEL, D_MODEL), const2),
                  pl.BlockSpec((1, D_MODEL), const2),
                  pl.BlockSpec((1, D_MODEL), const2)],
        out_specs=[pl.BlockSpec((tm, D_MODEL), row), pl.BlockSpec((tm, D_MODEL), row)],
        out_shape=[jax.ShapeDtypeStruct((rows, D_MODEL), F32), jax.ShapeDtypeStruct((rows, D_MODEL), BF16)],
        compiler_params=_cparams(("arbitrary",)),
        name="mix_ln1",
    )(pooled, a, x, w_pool, pool_scale, w_out, g1, b1)


PA_GROUP = 4
TF = 512
TF_SUB = 256
N_FF_CHUNKS = D_FF // TF
CONV_HALO = 8


def _silu(x):
    return x / (1.0 + jnp.exp(-x))


def _ffn_accumulate(gate, val, wd_ref, x1_ref, g_ref, b_ref, o_ref, acc_ref):
    c = pl.program_id(1)
    h = (_silu(gate) * val).astype(BF16)
    part = jnp.dot(h, wd_ref[...], preferred_element_type=F32)

    @pl.when(c == 0)
    def _():
        acc_ref[...] = part

    @pl.when(c > 0)
    def _():
        acc_ref[...] += part

    @pl.when(c == N_FF_CHUNKS - 1)
    def _():
        o_ref[...] = _layer_norm(DEEPNORM_ALPHA * x1_ref[...] + acc_ref[...], g_ref, b_ref)


def _ffn_fused_kernel(x1_ref, wg_ref, wv_ref, fg_ref, fv_ref, cwg_ref, cwv_ref, cbg_ref, cbv_ref,
                      wd_ref, g_ref, b_ref, o_ref, tg_ref, tv_ref, xb_scr, acc_ref, carry_g, carry_v, ext_scr,
                      *, tiles_per_batch):
    i = pl.program_id(0)
    c = pl.program_id(1)
    tm = x1_ref.shape[0]
    first = i % tiles_per_batch == 0

    @pl.when(c == 0)
    def _():
        xb_scr[...] = x1_ref[...].astype(BF16)
        acc_ref[...] = jnp.zeros(acc_ref.shape, F32)

    @pl.when(i == 0)
    def _():
        carry_g[c] = jnp.zeros((CONV_HALO, TF), F32)
        carry_v[c] = jnp.zeros((CONV_HALO, TF), F32)

    part = None
    for s in range(TF // TF_SUB):
        cols = slice(s * TF_SUB, (s + 1) * TF_SUB)

        def conv(w_ref, f_ref, carry, tail_ref, cw_ref, cb_ref, ext):
            u = jnp.dot(xb_scr[...], w_ref[:, cols], preferred_element_type=F32)
            ext[0:CONV_HALO, :] = jnp.where(first, f_ref[:, cols], carry[c, :, cols])
            ext[CONV_HALO:, :] = u
            tail = u[tm - CONV_HALO:, :]
            carry[c, :, cols] = tail
            tail_ref[0, :, cols] = tail
            return (cb_ref[:, cols] + cw_ref[0:1, cols] * ext[CONV_HALO - 2:CONV_HALO - 2 + tm, :]
                    + cw_ref[1:2, cols] * ext[CONV_HALO - 1:CONV_HALO - 1 + tm, :]
                    + cw_ref[2:3, cols] * u)

        gate = conv(wg_ref, fg_ref, carry_g, tg_ref, cwg_ref, cbg_ref, ext_scr.at[2 * s])
        val = conv(wv_ref, fv_ref, carry_v, tv_ref, cwv_ref, cbv_ref, ext_scr.at[2 * s + 1])
        h = (_silu(gate) * val).astype(BF16)
        d = jnp.dot(h, wd_ref[cols, :], preferred_element_type=F32)
        part = d if part is None else part + d
    acc_ref[...] += part

    @pl.when(c == N_FF_CHUNKS - 1)
    def _():
        o_ref[...] = _layer_norm(DEEPNORM_ALPHA * x1_ref[...] + acc_ref[...], g_ref, b_ref)


def _ffn_prev_kernel(ug_ref, uv_ref, p1g_ref, p1v_ref, p2g_ref, p2v_ref, cwg_ref, cwv_ref, cbg_ref, cbv_ref,
                     wd_ref, x1_ref, g_ref, b_ref, o_ref, acc_ref):
    def conv(u_ref, p1_ref, p2_ref, cw_ref, cb_ref):
        return (cb_ref[...] + cw_ref[0:1, :] * p2_ref[...] + cw_ref[1:2, :] * p1_ref[...]
                + cw_ref[2:3, :] * u_ref[...])

    gate = conv(ug_ref, p1g_ref, p2g_ref, cwg_ref, cbg_ref)
    val = conv(uv_ref, p1v_ref, p2v_ref, cwv_ref, cbv_ref)
    _ffn_accumulate(gate, val, wd_ref, x1_ref, g_ref, b_ref, o_ref, acc_ref)


def _ffn_common_specs(tm):
    gate_c = lambda i, c: (0, c)
    val_c = lambda i, c: (0, N_FF_CHUNKS + c)
    return [pl.BlockSpec((3, TF), gate_c), pl.BlockSpec((3, TF), val_c),
            pl.BlockSpec((1, TF), gate_c), pl.BlockSpec((1, TF), val_c),
            pl.BlockSpec((TF, D_MODEL), lambda i, c: (c, 0)),
            pl.BlockSpec((tm, D_MODEL), lambda i, c: (i, 0)),
            pl.BlockSpec((1, D_MODEL), lambda i, c: (0, 0)),
            pl.BlockSpec((1, D_MODEL), lambda i, c: (0, 0))]


def _ffn_fused(x1, w_up, u_first, first_row_block, conv_w, conv_b, w_down, g2, b2, *, tm, tiles_per_batch):
    rows = x1.shape[0]
    kern = functools.partial(_ffn_fused_kernel, tiles_per_batch=tiles_per_batch)
    gate_c = lambda i, c: (0, c)
    val_c = lambda i, c: (0, N_FF_CHUNKS + c)
    const = lambda i, c: (0, 0)
    tail_spec = pl.BlockSpec((1, CONV_HALO, TF), lambda i, c: (i, 0, c))
    tail_shape = jax.ShapeDtypeStruct((rows // tm, CONV_HALO, D_FF), F32)
    y, tail_g, tail_v = pl.pallas_call(
        kern,
        grid=(rows // tm, N_FF_CHUNKS),
        in_specs=[pl.BlockSpec((tm, D_MODEL), lambda i, c: (i, 0)),
                  pl.BlockSpec((D_MODEL, TF), gate_c), pl.BlockSpec((D_MODEL, TF), val_c),
                  pl.BlockSpec((CONV_HALO, TF), lambda i, c: (first_row_block, c)),
                  pl.BlockSpec((CONV_HALO, TF), lambda i, c: (first_row_block, N_FF_CHUNKS + c)),
                  pl.BlockSpec((3, TF), gate_c), pl.BlockSpec((3, TF), val_c),
                  pl.BlockSpec((1, TF), gate_c), pl.BlockSpec((1, TF), val_c),
                  pl.BlockSpec((TF, D_MODEL), lambda i, c: (c, 0)),
                  pl.BlockSpec((1, D_MODEL), const), pl.BlockSpec((1, D_MODEL), const)],
        out_specs=[pl.BlockSpec((tm, D_MODEL), lambda i, c: (i, 0)), tail_spec, tail_spec],
        out_shape=[jax.ShapeDtypeStruct((rows, D_MODEL), F32), tail_shape, tail_shape],
        scratch_shapes=[pltpu.VMEM((tm, D_MODEL), BF16), pltpu.VMEM((tm, D_MODEL), F32),
                        pltpu.VMEM((N_FF_CHUNKS, CONV_HALO, TF), F32),
                        pltpu.VMEM((N_FF_CHUNKS, CONV_HALO, TF), F32),
                        pltpu.VMEM((2 * (TF // TF_SUB), tm + CONV_HALO, TF_SUB), F32)],
        compiler_params=_cparams(("arbitrary", "arbitrary")),
        name="ffn",
    )(x1, w_up, w_up, u_first, u_first, conv_w, conv_w, conv_b, conv_b, w_down, g2, b2)
    last = slice(tiles_per_batch - 1, None, tiles_per_batch)
    return y, tail_g[last], tail_v[last]


def _ffn_tail_prev(u, prev1, prev2, conv_w, conv_b, w_down, x1, g2, b2, *, tm):
    rows = u.shape[0]
    gate_t = lambda i, c: (i, c)
    val_t = lambda i, c: (i, N_FF_CHUNKS + c)
    return pl.pallas_call(
        _ffn_prev_kernel,
        grid=(rows // tm, N_FF_CHUNKS),
        in_specs=[pl.BlockSpec((tm, TF), gate_t), pl.BlockSpec((tm, TF), val_t),
                  pl.BlockSpec((tm, TF), gate_t), pl.BlockSpec((tm, TF), val_t),
                  pl.BlockSpec((tm, TF), gate_t), pl.BlockSpec((tm, TF), val_t)]
                 + _ffn_common_specs(tm),
        out_specs=pl.BlockSpec((tm, D_MODEL), lambda i, c: (i, 0)),
        out_shape=jax.ShapeDtypeStruct((rows, D_MODEL), F32),
        scratch_shapes=[pltpu.VMEM((tm, D_MODEL), F32)],
        compiler_params=_cparams(("arbitrary", "arbitrary")),
        name="ffn_tail_small",
    )(u, u, prev1, prev1, prev2, prev2, conv_w, conv_w, conv_b, conv_b, w_down, x1, g2, b2)


def kernel(x_prompt, x_sample, cache_k, cache_v, cache_idx_k, state_pool, state_conv, page_table, meta_tokens,
           w_in, w_pool, pool_scale, w_out, ln1_g, ln1_b, w_up, conv_w, conv_b, w_down, ln2_g, ln2_b):
    n_phys = cache_k.shape[1]

    wq, wk, wv, wiq, wik, wiw, wp = jnp.split(w_in[0], IN_OFFSETS, axis=1)
    zcols = lambda n: jnp.zeros((D_MODEL, n), F32)
    w_all = jnp.concatenate([wq, wp, wk, wv, wiq, wik, zcols(64), wiw, zcols(120)], axis=1).astype(BF16)
    w_pool_b = w_pool[0].astype(BF16)
    w_out_b = w_out[0].astype(BF16)
    w_up_b = w_up[0].astype(BF16)
    w_down_b = w_down[0].astype(BF16)
    row2 = lambda a: a.reshape(1, -1)

    pos_prompt = N_META + jnp.arange(SEQ)
    pos_small = jnp.concatenate([jnp.tile(PAST_LEN + jnp.arange(DEC_SEQ), DEC_BATCH), jnp.arange(N_META),
                                 jnp.zeros((SMALL_ROWS - N_SAMPLE_ROWS - N_META,), I32)])
    tabs_p = _rope_tables(pos_prompt, ROT_DIM, HEAD_DIM) + _rope_tables(pos_prompt, IDX_ROT_DIM, IDX_DIM)
    tabs_s = _rope_tables(pos_small, ROT_DIM, HEAD_DIM) + _rope_tables(pos_small, IDX_ROT_DIM, IDX_DIM)

    xp = x_prompt.reshape(N_PROMPT_ROWS, D_MODEL)
    xs = jnp.concatenate([x_sample.reshape(N_SAMPLE_ROWS, D_MODEL), meta_tokens.astype(F32),
                          jnp.zeros((SMALL_ROWS - N_SAMPLE_ROWS - N_META, D_MODEL), F32)], axis=0)

    tm_p = 1024
    hp = _project(xp, w_all, tabs_p, SEQ // tm_p, tm_p)
    hs = _project(xs, w_all, tabs_s, 1, SMALL_ROWS)

    meta = slice(META_ROW0, META_ROW0 + N_META)

    def with_meta(c0, width):
        m = jnp.broadcast_to(hs[meta, c0:c0 + width][None], (BATCH, N_META, width))
        return jnp.concatenate([m, hp[:, c0:c0 + width].reshape(BATCH, SEQ, width)], axis=1)

    k_prompt = with_meta(C_K, KV_WIDTH)
    v_prompt = with_meta(C_V, KV_WIDTH)
    ik_prompt = with_meta(C_IK, IDX_DIM)
    pad_keys = lambda a: jnp.pad(a.astype(BF16), ((0, 0), (0, NK_PROMPT - T_PROMPT), (0, 0)))
    kb, vb, ikb = pad_keys(k_prompt), pad_keys(v_prompt), pad_keys(ik_prompt)

    blocks_per_batch = SEQ // LANES
    a_groups = []
    for j0 in range(0, blocks_per_batch, PA_GROUP):
        nk = N_META + (j0 + PA_GROUP) * LANES
        nk = -(-nk // LANES) * LANES
        a_groups.append(_prompt_attention(hp, kb, vb, ikb, n_batch=BATCH, n_blocks=PA_GROUP, row_block0=j0,
                                          blocks_per_batch=blocks_per_batch, pos_first=N_META + j0 * LANES,
                                          nk=nk))
    a_p = jnp.concatenate(a_groups, axis=1).reshape(N_PROMPT_ROWS, ATTN_WIDTH)
    a_m = _prompt_attention(hs, kb, vb, ikb, n_batch=1, n_blocks=1, row_block0=META_ROW0 // LANES,
                            blocks_per_batch=1, pos_first=0, nk=LANES)[0]

    hs_s = hs[:N_SAMPLE_ROWS]
    iq_s = hs_s[:, C_IQ:C_IQ + N_IDX_HEADS * IDX_DIM].reshape(DEC_BATCH, DEC_SEQ, N_IDX_HEADS, IDX_DIM)
    iq_s = jnp.pad(iq_s.transpose(0, 2, 1, 3), ((0, 0), (0, 0), (0, 8 - DEC_SEQ), (0, 0)))
    iq_s = iq_s.reshape(DEC_BATCH, N_IDX_HEADS * 8, IDX_DIM)
    w_s = hs_s[:, C_IW:C_IW + N_IDX_HEADS].reshape(DEC_BATCH, DEC_SEQ, N_IDX_HEADS)
    w_s = jnp.pad(w_s.transpose(0, 2, 1), ((0, 0), (0, 0), (0, 8 - DEC_SEQ))).reshape(DEC_BATCH, 64, 1)

    def new_rows(c0, width):
        a = hs_s[:, c0:c0 + width].reshape(DEC_BATCH, DEC_SEQ, width)
        return jnp.pad(a, ((0, 0), (0, LANES - DEC_SEQ), (0, 0)))

    pt_pad = jnp.pad(page_table, ((0, 0), (0, N_MROWS - N_PAGES)))
    pt_hi = jnp.broadcast_to((pt_pad // 64).astype(F32)[:, None, :], (DEC_BATCH, 8, N_MROWS))
    pt_lo = jnp.broadcast_to((pt_pad % 64).astype(F32)[:, None, :], (DEC_BATCH, 8, N_MROWS))
    rows_sel, gbias, nbias = _sample_select(page_table, iq_s, w_s, new_rows(C_IK, IDX_DIM), pt_hi, pt_lo,
                                            cache_idx_k[0])
    q_s = hs_s[:, C_Q:C_Q + ATTN_WIDTH].reshape(DEC_BATCH, DEC_SEQ, N_HEADS, 1, HEAD_DIM)
    group_of_head = jnp.arange(N_HEADS) // (N_HEADS // N_KV_HEADS)
    head_mask = (group_of_head[:, None] == jnp.arange(N_KV_HEADS)[None, :]).astype(F32)
    qbd = (q_s * head_mask[None, None, :, :, None]).reshape(DEC_BATCH, DEC_SEQ, N_HEADS, KV_WIDTH)
    a_s8 = _sample_attention(rows_sel.reshape(-1), gbias, nbias, qbd, new_rows(C_K, KV_WIDTH),
                             new_rows(C_V, KV_WIDTH),
                             cache_k[0].reshape(n_phys * PAGE_SIZE, N_KV_HEADS, HEAD_DIM),
                             cache_v[0].reshape(n_phys * PAGE_SIZE, N_KV_HEADS, HEAD_DIM))
    a_s = a_s8[:, :DEC_SEQ].reshape(N_SAMPLE_ROWS, ATTN_WIDTH).astype(BF16)
    a_small = jnp.concatenate([a_s, a_m[:N_META], jnp.zeros((SMALL_ROWS - N_SAMPLE_ROWS - N_META, ATTN_WIDTH),
                                                            BF16)], axis=0)

    win = jnp.repeat(jnp.asarray(POOL_WINDOWS, F32), POOL_GROUP)
    p_meta = hs[meta, C_P:C_P + POOL_WIDTH]
    tm_pool = 256
    pooled_p = _pool(hp, C_P // POOL_WIDTH, hp, hs, (META_ROW0 // N_META, C_P // POOL_WIDTH), row2(win),
                     rows=N_PROMPT_ROWS, tm=tm_pool, tiles_per_batch=SEQ // tm_pool)
    p_s = hs_s[:, C_P:C_P + POOL_WIDTH].reshape(DEC_BATCH, DEC_SEQ, POOL_WIDTH)
    grp = 24
    ext_s = jnp.concatenate([jnp.zeros((DEC_BATCH, 1, POOL_WIDTH), F32), state_pool[0], p_s,
                             jnp.zeros((DEC_BATCH, grp - 1 - POOL_CTX - DEC_SEQ, POOL_WIDTH), F32)], axis=1)
    ext_small = jnp.concatenate([ext_s.reshape(DEC_BATCH * grp, POOL_WIDTH),
                                 jnp.zeros((N_META, POOL_WIDTH), F32), p_meta], axis=0)
    n_ext = DEC_BATCH * grp + 2 * N_META
    cnt_meta = jnp.minimum(win[None, :], (jnp.arange(N_META, dtype=F32) + 1.0)[:, None])
    cnt_small = jnp.concatenate([jnp.broadcast_to(win[None], (n_ext - N_META, POOL_WIDTH)), cnt_meta], axis=0)
    zeros_halo = jnp.zeros((N_META, POOL_WIDTH), F32)
    pooled_ext = _pool(ext_small, 0, zeros_halo, zeros_halo, (0, 0), cnt_small,
                       rows=n_ext, tm=n_ext, tiles_per_batch=1)
    pooled_small = jnp.concatenate(
        [pooled_ext[:DEC_BATCH * grp].reshape(DEC_BATCH, grp, POOL_WIDTH)[:, 16:16 + DEC_SEQ].reshape(
            N_SAMPLE_ROWS, POOL_WIDTH),
         pooled_ext[n_ext - N_META:],
         jnp.zeros((SMALL_ROWS - N_SAMPLE_ROWS - N_META, POOL_WIDTH), BF16)], axis=0)

    mix_args = (w_pool_b, row2(pool_scale[0]), w_out_b, row2(ln1_g[0]), row2(ln1_b[0]))
    x1_p, x1b_p = _mix(pooled_p, a_p, xp, *mix_args, tm=256)
    x1_s, x1b_s = _mix(pooled_small, a_small, xs, *mix_args, tm=SMALL_ROWS)

    u_s = _matmul(x1b_s, w_up_b, SMALL_ROWS, "ffn_up_small")
    ffn_args = (conv_w[0], row2(conv_b[0]), w_down_b)
    ln2 = (row2(ln2_g[0]), row2(ln2_b[0]))
    tm_f = 512
    y_p, tail_g, tail_v = _ffn_fused(x1_p, w_up_b, u_s, (META_ROW0 + N_META) // CONV_HALO - 1, *ffn_args, *ln2,
                                     tm=tm_f, tiles_per_batch=SEQ // tm_f)
    u_s3 = u_s[:N_SAMPLE_ROWS].reshape(DEC_BATCH, DEC_SEQ, 2 * D_FF)
    ext_u = jnp.concatenate([state_conv[0], u_s3], axis=1)
    u_m = u_s[meta]
    ext_m = jnp.concatenate([jnp.zeros((2, 2 * D_FF), F32), u_m], axis=0)
    tail0 = jnp.zeros((SMALL_ROWS - N_SAMPLE_ROWS - N_META, 2 * D_FF), F32)
    prev1 = jnp.concatenate([ext_u[:, 1:1 + DEC_SEQ].reshape(N_SAMPLE_ROWS, -1), ext_m[1:1 + N_META], tail0], 0)
    prev2 = jnp.concatenate([ext_u[:, 0:DEC_SEQ].reshape(N_SAMPLE_ROWS, -1), ext_m[0:N_META], tail0], 0)
    y_s = _ffn_tail_prev(u_s, prev1, prev2, *ffn_args, x1_s, *ln2, tm=SMALL_ROWS)

    y_prompt = y_p.reshape(BATCH, SEQ, D_MODEL)
    y_sample = y_s[:N_SAMPLE_ROWS].reshape(DEC_BATCH, DEC_SEQ, D_MODEL)
    pool_prompt = hp[:, C_P:C_P + POOL_WIDTH].reshape(BATCH, SEQ, POOL_WIDTH)[:, SEQ - POOL_CTX:]
    conv_prompt = jnp.concatenate([tail_g[:, CONV_HALO - 2:], tail_v[:, CONV_HALO - 2:]], axis=-1)
    k_sample = hs_s[:, C_K:C_K + KV_WIDTH].reshape(DEC_BATCH, DEC_SEQ, N_KV_HEADS, HEAD_DIM)
    v_sample = hs_s[:, C_V:C_V + KV_WIDTH].reshape(DEC_BATCH, DEC_SEQ, N_KV_HEADS, HEAD_DIM)
    ik_sample = hs_s[:, C_IK:C_IK + IDX_DIM].reshape(DEC_BATCH, DEC_SEQ, IDX_DIM)
    pool_sample = jnp.concatenate([state_pool[0], p_s], axis=1)[:, DEC_SEQ:]
    conv_sample = ext_u[:, DEC_SEQ:]
    return (y_prompt, y_sample,
            k_prompt.reshape(1, BATCH, T_PROMPT, N_KV_HEADS, HEAD_DIM),
            v_prompt.reshape(1, BATCH, T_PROMPT, N_KV_HEADS, HEAD_DIM),
            ik_prompt[None], pool_prompt[None], conv_prompt[None],
            k_sample[None], v_sample[None], ik_sample[None], pool_sample[None], conv_sample[None])
```

```python
import functools

import numpy as np
import jax
import jax.numpy as jnp
from jax import lax
from jax.experimental import pallas as pl
from jax.experimental.pallas import tpu as pltpu

F32 = jnp.float32
BF16 = jnp.bfloat16
I32 = jnp.int32

D_MODEL = 2048
BATCH = 8
SEQ = 2048
DEC_BATCH = 32
DEC_SEQ = 4
PAST_LEN = 16384
PAGE_SIZE = 128
N_META = 16
ATTN_WIDTH = 1024
N_HEADS = 8
HEAD_DIM = 128
N_KV_HEADS = 4
KV_WIDTH = N_KV_HEADS * HEAD_DIM
ROT_DIM = 32
N_IDX_HEADS = 8
IDX_DIM = 64
IDX_ROT_DIM = 16
TOPK = 256
POOL_WIDTH = 1024
POOL_WINDOWS = (2, 4, 8, 16)
POOL_GROUP = 256
POOL_CTX = 15
D_FF = 5632
ROPE_THETA = 500000.0
LN_EPS = 1e-5
DEEPNORM_ALPHA = 2.0 ** 0.25
IN_OFFSETS = (1024, 1536, 2048, 2560, 2624, 2632)

LANES = 128
N_PROMPT_ROWS = BATCH * SEQ
N_SAMPLE_ROWS = DEC_BATCH * DEC_SEQ
SMALL_ROWS = 256
META_ROW0 = N_SAMPLE_ROWS
T_PROMPT = N_META + SEQ
NK_PROMPT = 2176
NK_SAMPLE = PAST_LEN + LANES
N_PAGES = PAST_LEN // PAGE_SIZE

C_Q, C_P, C_K, C_V, C_IQ, C_IK, C_IW = 0, 1024, 2048, 2560, 3072, 3584, 3712
PROJ_WIDTH = 3840
TN = 256
ROPE_A_TILES = (0, 1, 2, 3, 8, 9)
ROPE_B_TILES = (12, 13)
ROPE_BH_TILE = 14

VMEM_LIMIT = 56 * 1024 * 1024
INT_MIN = -2 ** 31
NEG_INF = float("-inf")
LOG2_E = 1.4426950408889634


def _cparams(sem, **kw):
    return pltpu.CompilerParams(dimension_semantics=sem, vmem_limit_bytes=VMEM_LIMIT, **kw)


def _rope_tables(pos, rot_dim, head_dim):
    half = rot_dim // 2
    inv_freq = ROPE_THETA ** (-jnp.arange(half, dtype=F32) / half)
    ang = pos.astype(F32)[:, None] * inv_freq[None, :]
    cos, sin = jnp.cos(ang), jnp.sin(ang)
    rows = pos.shape[0]
    zh = jnp.zeros((rows, half), F32)
    rest0 = jnp.zeros((rows, head_dim - rot_dim), F32)
    c = jnp.concatenate([cos, cos, rest0 + 1.0], axis=1)
    s1 = jnp.concatenate([-sin, zh, rest0], axis=1)
    s2 = jnp.concatenate([zh, sin, rest0], axis=1)
    reps = LANES // head_dim
    return tuple(jnp.tile(t, (1, reps)) for t in (c, s1, s2))


def _rope(x, c_ref, s1_ref, s2_ref, half):
    return (x * c_ref[...] + pltpu.roll(x, LANES - half, 1) * s1_ref[...]
            + pltpu.roll(x, half, 1) * s2_ref[...])


def _proj_kernel(x_ref, w_ref, ca, sa1, sa2, cb, sb1, sb2, o_ref, xb_ref):
    j = pl.program_id(1)

    @pl.when(j == 0)
    def _():
        xb_ref[...] = x_ref[...].astype(BF16)

    acc = jnp.dot(xb_ref[...], w_ref[...], preferred_element_type=F32)
    is_a = functools.reduce(jnp.logical_or, [j == t for t in ROPE_A_TILES])
    is_b = functools.reduce(jnp.logical_or, [j == t for t in ROPE_B_TILES])
    is_bh = j == ROPE_BH_TILE

    @pl.when(is_a)
    def _():
        o_ref[:, :LANES] = _rope(acc[:, :LANES], ca, sa1, sa2, ROT_DIM // 2)
        o_ref[:, LANES:] = _rope(acc[:, LANES:], ca, sa1, sa2, ROT_DIM // 2)

    @pl.when(is_b)
    def _():
        o_ref[:, :LANES] = _rope(acc[:, :LANES], cb, sb1, sb2, IDX_ROT_DIM // 2)
        o_ref[:, LANES:] = _rope(acc[:, LANES:], cb, sb1, sb2, IDX_ROT_DIM // 2)

    @pl.when(is_bh)
    def _():
        o_ref[:, :LANES] = _rope(acc[:, :LANES], cb, sb1, sb2, IDX_ROT_DIM // 2)
        o_ref[:, LANES:] = acc[:, LANES:]

    @pl.when(jnp.logical_not(is_a | is_b | is_bh))
    def _():
        o_ref[...] = acc


def _project(x, w_all, tabs, tab_blocks, tm):
    rows = x.shape[0]
    tab_spec = pl.BlockSpec((tm, LANES), lambda i, j: (i % tab_blocks, 0))
    return pl.pallas_call(
        _proj_kernel,
        grid=(rows // tm, PROJ_WIDTH // TN),
        in_specs=[pl.BlockSpec((tm, D_MODEL), lambda i, j: (i, 0)),
                  pl.BlockSpec((D_MODEL, TN), lambda i, j: (0, j))] + [tab_spec] * 6,
        out_specs=pl.BlockSpec((tm, TN), lambda i, j: (i, j)),
        out_shape=jax.ShapeDtypeStruct((rows, PROJ_WIDTH), F32),
        scratch_shapes=[pltpu.VMEM((tm, D_MODEL), BF16)],
        compiler_params=_cparams(("arbitrary", "arbitrary")),
        name="in_proj",
    )(x, w_all, *tabs)


def _mm_kernel(x_ref, w_ref, o_ref):
    o_ref[...] = jnp.dot(x_ref[...], w_ref[...], preferred_element_type=F32)


def _matmul(x, w, tm, name):
    rows, kdim = x.shape
    n = w.shape[1]
    return pl.pallas_call(
        _mm_kernel,
        grid=(rows // tm, n // TN),
        in_specs=[pl.BlockSpec((tm, kdim), lambda i, j: (i, 0)),
                  pl.BlockSpec((kdim, TN), lambda i, j: (0, j))],
        out_specs=pl.BlockSpec((tm, TN), lambda i, j: (i, j)),
        out_shape=jax.ShapeDtypeStruct((rows, n), F32),
        compiler_params=_cparams(("arbitrary", "arbitrary")),
        name=name,
    )(x, w)


def _ordinal_to_float(key):
    bits = key ^ ((key >> 31) & 0x7FFFFFFF)
    return lax.bitcast_convert_type(bits, F32)


def _select_topk(sc_ref, col, valid, n_col_bits, cut_scr):
    rows = sc_ref.shape[0]
    k_f = float(TOPK)

    def count(pred):
        return jnp.sum(jnp.where(pred, 1.0, 0.0), axis=1, keepdims=True)

    few = count(valid) <= k_f
    key0 = jnp.where(count(sc_ref[...] >= 0.0) >= k_f, 0, INT_MIN).astype(I32)

    def value_step(i, key):
        cand = key + jnp.left_shift(jnp.int32(1), 30 - i)
        cnt = count(sc_ref[...] >= _ordinal_to_float(cand))
        return jnp.where(cnt >= k_f, cand, key)

    key = lax.fori_loop(0, 31, value_step, key0)
    thr = jnp.where(few, NEG_INF, _ordinal_to_float(key))
    need = k_f - count(sc_ref[...] > thr)
    surplus = jnp.where(few, 0.0, count(sc_ref[...] >= thr) - k_f)

    cut_scr[...] = jnp.full(cut_scr.shape, 2 ** n_col_bits - 1, I32)

    @pl.when(jnp.max(surplus) > 0.0)
    def _():
        def col_step(i, cut):
            cand = cut + jnp.left_shift(jnp.int32(1), n_col_bits - 1 - i)
            cnt = count((sc_ref[...] == thr) & (col < cand))
            return jnp.where(cnt < need, cand, cut)

        cut_scr[...] = lax.fori_loop(0, n_col_bits, col_step, jnp.zeros((rows, 1), I32))

    sc = sc_ref[...]
    return valid & ((sc > thr) | ((sc == thr) & (col <= cut_scr[...])))


def _select_topk_tiled(sc_ref, valid_fn, n_col_bits, cut_scr, out_ref):
    rows, n = sc_ref.shape
    k_f = float(TOPK)
    lane = lax.broadcasted_iota(I32, (rows, LANES), 1)
    tiles = [(t * LANES, slice(t * LANES, (t + 1) * LANES)) for t in range(n // LANES)]
    wide = lambda v: jnp.broadcast_to(v, (rows, LANES))

    def count(pred_fn):
        acc = jnp.zeros((rows, LANES), F32)
        for c0, cols in tiles:
            acc = acc + jnp.where(pred_fn(sc_ref[:, cols], lane + c0), 1.0, 0.0)
        return jnp.sum(acc, axis=1, keepdims=True)

    few = count(lambda sc, col: valid_fn(col)) <= k_f
    key0 = jnp.where(count(lambda sc, col: sc >= 0.0) >= k_f, 0, INT_MIN).astype(I32)

    def value_step(i, key):
        cand = wide(_ordinal_to_float(key + jnp.left_shift(jnp.int32(1), 30 - i)))
        cnt = count(lambda sc, col: sc >= cand)
        return jnp.where(cnt >= k_f, key + jnp.left_shift(jnp.int32(1), 30 - i), key)

    key = lax.fori_loop(0, 31, value_step, key0)
    thr = jnp.where(few, NEG_INF, _ordinal_to_float(key))
    thr_w = wide(thr)
    need = k_f - count(lambda sc, col: sc > thr_w)
    surplus = jnp.where(few, 0.0, count(lambda sc, col: sc >= thr_w) - k_f)

    cut_scr[...] = jnp.full(cut_scr.shape, 2 ** n_col_bits - 1, I32)

    @pl.when(jnp.max(surplus) > 0.0)
    def _():
        def col_step(i, cut):
            cand = cut + jnp.left_shift(jnp.int32(1), n_col_bits - 1 - i)
            cand_w = wide(cand)
            cnt = count(lambda sc, col: (sc == thr_w) & (col < cand_w))
            return jnp.where(cnt < need, cand, cut)

        cut_scr[...] = lax.fori_loop(0, n_col_bits, col_step, jnp.zeros((rows, 1), I32))

    cut_w = wide(cut_scr[...])
    for c0, cols in tiles:
        sc, col = sc_ref[:, cols], lane + c0
        keep = valid_fn(col) & ((sc > thr_w) | ((sc == thr_w) & (col <= cut_w)))
        out_ref[:, cols] = jnp.where(keep, 1.0, 0.0)


def _pattn_kernel(q_ref, iq_ref, misc_ref, kb_ref, vb_ref, ikb_ref, o_ref, sc_scr, bias_scr, cut_scr,
                  *, pos_first, nk):
    pos0 = pos_first + LANES * pl.program_id(1)
    tq = q_ref.shape[0]

    ik = ikb_ref[0]
    w = misc_ref[:, LANES:LANES + N_IDX_HEADS] * (N_IDX_HEADS ** -0.5)
    score = jnp.zeros((tq, nk), F32)
    for h in range(N_IDX_HEADS):
        iq_h = iq_ref[:, h * IDX_DIM:(h + 1) * IDX_DIM].astype(BF16)
        s_h = lax.dot_general(iq_h, ik, (((1,), (1,)), ((), ())), preferred_element_type=F32)
        score = score + jnp.maximum(s_h * (IDX_DIM ** -0.5), 0.0) * w[:, h:h + 1]

    qpos = pos0 + lax.broadcasted_iota(I32, (tq, nk), 0)
    col = lax.broadcasted_iota(I32, (tq, nk), 1)
    causal = col <= qpos
    sc_scr[...] = jnp.where(causal, score, NEG_INF)
    sel = _select_topk(sc_scr, col, causal, 12, cut_scr)
    bias_scr[...] = jnp.where(sel, 0.0, NEG_INF)

    q_scale = (HEAD_DIM ** -0.5) * LOG2_E
    for h in range(N_HEADS):
        kv = h // (N_HEADS // N_KV_HEADS)
        q_h = (q_ref[:, h * HEAD_DIM:(h + 1) * HEAD_DIM] * q_scale).astype(BF16)
        k_g = kb_ref[0, :, kv * HEAD_DIM:(kv + 1) * HEAD_DIM]
        s = lax.dot_general(q_h, k_g, (((1,), (1,)), ((), ())), preferred_element_type=F32)
        s = s + bias_scr[...]
        m = jnp.max(s, axis=1, keepdims=True)
        p = jnp.exp2(s - m)
        l = jnp.sum(p, axis=1, keepdims=True)
        v_g = vb_ref[0, :, kv * HEAD_DIM:(kv + 1) * HEAD_DIM]
        o = jnp.dot(p.astype(BF16), v_g, preferred_element_type=F32) / l
        o_ref[0, :, h * HEAD_DIM:(h + 1) * HEAD_DIM] = o.astype(o_ref.dtype)


def _prompt_attention(h, kb, vb, ikb, *, n_batch, n_blocks, row_block0, blocks_per_batch, pos_first, nk):
    kern = functools.partial(_pattn_kernel, pos_first=pos_first, nk=nk)
    qrow = lambda b, j: row_block0 + b * blocks_per_batch + j
    return pl.pallas_call(
        kern,
        grid=(n_batch, n_blocks),
        in_specs=[pl.BlockSpec((LANES, ATTN_WIDTH), lambda b, j: (qrow(b, j), C_Q // ATTN_WIDTH)),
                  pl.BlockSpec((LANES, 512), lambda b, j: (qrow(b, j), C_IQ // 512)),
                  pl.BlockSpec((LANES, 256), lambda b, j: (qrow(b, j), C_IK // 256)),
                  pl.BlockSpec((1, nk, KV_WIDTH), lambda b, j: (b, 0, 0)),
                  pl.BlockSpec((1, nk, KV_WIDTH), lambda b, j: (b, 0, 0)),
                  pl.BlockSpec((1, nk, IDX_DIM), lambda b, j: (b, 0, 0))],
        out_specs=pl.BlockSpec((1, LANES, ATTN_WIDTH), lambda b, j: (b, j, 0)),
        out_shape=jax.ShapeDtypeStruct((n_batch, n_blocks * LANES, ATTN_WIDTH), BF16),
        scratch_shapes=[pltpu.VMEM((LANES, nk), F32), pltpu.VMEM((LANES, nk), F32),
                        pltpu.VMEM((LANES, 1), I32)],
        compiler_params=_cparams(("arbitrary", "arbitrary")),
        name="prompt_attention",
    )(h, h, h, kb, vb, ikb)


KV_SHIFT = 2
N_MROWS = 256


SS_PAIR = 2
SS_STEPS = DEC_BATCH // SS_PAIR
SS_CHUNK = 2048


def _ssel_kernel(pt_ref, iq_ref, w_ref, iknew_ref, pthi_ref, ptlo_ref, cache_ref,
                 rows_ref, gbias_ref, nbias_ref, ikbuf, sem, sc_scr, sel_scr, m_scr, cut_scr):
    s = pl.program_id(0)

    def page_copy(step, slot, e, p):
        return pltpu.make_async_copy(cache_ref.at[pt_ref[SS_PAIR * step + e, p]],
                                     ikbuf.at[slot, e, :, pl.ds(pl.multiple_of(p * PAGE_SIZE, PAGE_SIZE), PAGE_SIZE)],
                                     sem.at[slot])

    def fetch(step, slot):
        def issue(p, carry):
            for e in range(SS_PAIR):
                page_copy(step, slot, e, p).start()
            return carry
        lax.fori_loop(0, N_PAGES, issue, 0)

    slot = s % 2

    @pl.when(s == 0)
    def _():
        fetch(0, 0)

    @pl.when(s + 1 < SS_STEPS)
    def _():
        fetch(s + 1, 1 - slot)

    pltpu.make_async_copy(ikbuf.at[slot], ikbuf.at[slot], sem.at[slot]).wait()

    w = w_ref[0] * (N_IDX_HEADS ** -0.5)
    iq = [iq_ref[0, e].astype(BF16) for e in range(SS_PAIR)]

    def score_of(keys_of):
        r = None
        for e in range(SS_PAIR):
            s_e = jnp.dot(iq[e], keys_of(e).astype(BF16), preferred_element_type=F32)
            r_e = jnp.maximum(s_e * (IDX_DIM ** -0.5), 0.0)
            r = r_e if r is None else r + r_e
        r = r * w
        acc = r[0:8]
        for h in range(1, N_IDX_HEADS):
            acc = acc + r[8 * h:8 * h + 8]
        return acc

    row0 = pl.multiple_of(s * 8, 8)
    for c in range(PAST_LEN // SS_CHUNK):
        cols = slice(c * SS_CHUNK, (c + 1) * SS_CHUNK)
        sc_scr[pl.ds(row0, 8), cols] = score_of(lambda e: ikbuf[slot, e, :, cols])
    sc_scr[pl.ds(row0, 8), PAST_LEN:] = score_of(lambda e: iknew_ref[0, e])

    @pl.when(s == SS_STEPS - 1)
    def _():
        _ssel_finish(pthi_ref, ptlo_ref, rows_ref, gbias_ref, nbias_ref, sc_scr, sel_scr, m_scr, cut_scr)


def _ssel_finish(pthi_ref, ptlo_ref, rows_ref, gbias_ref, nbias_ref, sc_scr, sel_scr, m_scr, cut_scr):
    n_q = sc_scr.shape[0]
    qpos = PAST_LEN + (lax.broadcasted_iota(I32, (n_q, LANES), 0) & (DEC_SEQ - 1))
    valid_fn = lambda col: col <= qpos
    lane = lax.broadcasted_iota(I32, (n_q, LANES), 1)
    for t in range(NK_SAMPLE // LANES):
        cols = slice(t * LANES, (t + 1) * LANES)
        sc_scr[:, cols] = jnp.where(valid_fn(lane + t * LANES), sc_scr[:, cols], NEG_INF)
    _select_topk_tiled(sc_scr, valid_fn, 15, cut_scr, sel_scr)

    nt = (((1,), (1,)), ((), ()))
    ones_b = jnp.ones((8, LANES), BF16)
    upper_pages = (lax.broadcasted_iota(I32, (N_MROWS, N_MROWS), 0)
                   < lax.broadcasted_iota(I32, (N_MROWS, N_MROWS), 1)).astype(BF16)
    upper_lanes = (lax.broadcasted_iota(I32, (LANES, LANES), 0)
                   < lax.broadcasted_iota(I32, (LANES, LANES), 1)).astype(BF16)
    lane_id = lax.broadcasted_iota(I32, (8, LANES), 1).astype(BF16)
    page_id = lax.broadcasted_iota(I32, (8, N_MROWS), 1).astype(BF16)
    jcol = lax.broadcasted_iota(I32, (TOPK, 1), 0).astype(F32)
    spread_picks = (lax.broadcasted_iota(I32, (TOPK, N_KV_HEADS * TOPK), 0)
                    == lax.broadcasted_iota(I32, (TOPK, N_KV_HEADS * TOPK), 1) >> KV_SHIFT).astype(BF16)
    spread_new = (lax.broadcasted_iota(I32, (LANES, LANES), 0)
                  == lax.broadcasted_iota(I32, (LANES, LANES), 1) >> KV_SHIFT).astype(BF16)
    m_scr[...] = jnp.zeros(m_scr.shape, F32)

    def compact(k, batch, new_sel):
        pt_hi = pthi_ref[batch].astype(BF16)
        pt_lo = ptlo_ref[batch].astype(BF16)
        m_b = m_scr[k].astype(BF16)
        cnt = lax.dot_general(ones_b, m_b, nt, preferred_element_type=F32)
        start = jnp.dot(cnt.astype(BF16), upper_pages, preferred_element_type=F32)
        cnt1, start1 = cnt[0:1], start[0:1]
        in_page = (start1 <= jcol) & (jcol < start1 + cnt1)
        a_b = jnp.where(in_page, 1.0, 0.0).astype(BF16)
        m_j = jnp.dot(a_b, m_b, preferred_element_type=F32)
        rank = jnp.dot(m_j.astype(BF16), upper_lanes, preferred_element_type=F32)
        start_j = jnp.sum(jnp.where(in_page, start1, 0.0), axis=1, keepdims=True)
        onehot = jnp.where((m_j > 0.5) & (rank == jcol - start_j), 1.0, 0.0).astype(BF16)
        off = lax.dot_general(lane_id, onehot, nt, preferred_element_type=F32)[0:1]
        page = lax.dot_general(page_id, a_b, nt, preferred_element_type=F32)[0:1]
        phys = (lax.dot_general(pt_hi, a_b, nt, preferred_element_type=F32)[0:1] * 64.0
                + lax.dot_general(pt_lo, a_b, nt, preferred_element_type=F32)[0:1])
        in_past = page < float(N_PAGES)
        row = jnp.where(in_past, phys * float(PAGE_SIZE) + off, 0.0).astype(I32)
        past8 = jnp.broadcast_to(jnp.where(in_past, 1.0, 0.0), (8, TOPK)).astype(BF16)
        past4 = jnp.dot(past8, spread_picks, preferred_element_type=F32)[0:1]
        new8 = jnp.broadcast_to(new_sel, (8, LANES)).astype(BF16)
        new4 = jnp.dot(new8, spread_new, preferred_element_type=F32)[0:1]
        return row, jnp.where(past4 > 0.5, 0.0, NEG_INF), jnp.where(new4 > 0.5, 0.0, NEG_INF)

    def compact_group(g, carry):
        r8 = pl.multiple_of(g * 8, 8)
        for c in range(N_PAGES + 1):
            tile = sel_scr[pl.ds(r8, 8), c * LANES:(c + 1) * LANES]
            for k in range(8):
                m_scr[k, c:c + 1, :] = tile[k:k + 1, :]
        new_tile = sel_scr[pl.ds(r8, 8), PAST_LEN:]
        rowid = lax.broadcasted_iota(I32, (8, 1), 0)
        rows_t = jnp.zeros((8, TOPK), I32)
        gb_t = jnp.zeros((8, N_KV_HEADS * TOPK), F32)
        nb_t = jnp.zeros((8, LANES), F32)
        for k in range(8):
            row, gb, nb = compact(k, g * (8 // DEC_SEQ) + k // DEC_SEQ, new_tile[k:k + 1, :])
            rows_t = jnp.where(rowid == k, row, rows_t)
            gb_t = jnp.where(rowid == k, gb, gb_t)
            nb_t = jnp.where(rowid == k, nb, nb_t)
        rows_ref[pl.ds(r8, 8), :] = rows_t
        gbias_ref[pl.ds(r8, 8), :] = gb_t
        nbias_ref[pl.ds(r8, 8), :] = nb_t
        return carry

    lax.fori_loop(0, n_q // 8, compact_group, 0)


def _sample_select(page_table, iq2, w2, ik_new_t, pt_hi, pt_lo, cache_idx_t):
    n_q = N_SAMPLE_ROWS
    whole = lambda shape: pl.BlockSpec(shape, lambda s, pt: (0,) * len(shape))
    grid_spec = pltpu.PrefetchScalarGridSpec(
        num_scalar_prefetch=1,
        grid=(SS_STEPS,),
        in_specs=[pl.BlockSpec((1, SS_PAIR, 64, IDX_DIM), lambda s, pt: (s, 0, 0, 0)),
                  pl.BlockSpec((1, 64, 1), lambda s, pt: (s, 0, 0)),
                  pl.BlockSpec((1, SS_PAIR, IDX_DIM, LANES), lambda s, pt: (s, 0, 0, 0)),
                  whole((DEC_BATCH, 8, N_MROWS)),
                  whole((DEC_BATCH, 8, N_MROWS)),
                  pl.BlockSpec(memory_space=pl.ANY)],
        out_specs=[whole((n_q, TOPK)), whole((n_q, N_KV_HEADS * TOPK)), whole((n_q, LANES))],
        scratch_shapes=[pltpu.VMEM((2, SS_PAIR, IDX_DIM, PAST_LEN), F32),
                        pltpu.SemaphoreType.DMA((2,)),
                        pltpu.VMEM((n_q, NK_SAMPLE), F32),
                        pltpu.VMEM((n_q, NK_SAMPLE), F32),
                        pltpu.VMEM((8, N_MROWS, LANES), F32),
                        pltpu.VMEM((n_q, 1), I32)])
    return pl.pallas_call(
        _ssel_kernel,
        grid_spec=grid_spec,
        out_shape=[jax.ShapeDtypeStruct((n_q, TOPK), I32),
                   jax.ShapeDtypeStruct((n_q, N_KV_HEADS * TOPK), F32),
                   jax.ShapeDtypeStruct((n_q, LANES), F32)],
        compiler_params=_cparams(("arbitrary",)),
        name="sample_select",
    )(page_table, iq2, w2, ik_new_t, pt_hi, pt_lo, cache_idx_t)


SA_COLS = N_KV_HEADS * TOPK


def _sattn_kernel(rows_ref, gbias_ref, nbias_ref, q_ref, knew_ref, vnew_ref, ck_ref, cv_ref, o_ref,
                  kbuf, vbuf, sem):
    b = pl.program_id(0)
    n_slabs = DEC_SEQ * TOPK

    def issue(t, carry):
        row = pl.multiple_of(rows_ref[b * n_slabs + t] * N_KV_HEADS, N_KV_HEADS)
        dst = pl.ds(pl.multiple_of(t * N_KV_HEADS, N_KV_HEADS), N_KV_HEADS)
        pltpu.make_async_copy(ck_ref.at[pl.ds(row, N_KV_HEADS)], kbuf.at[dst], sem.at[0]).start()
        pltpu.make_async_copy(cv_ref.at[pl.ds(row, N_KV_HEADS)], vbuf.at[dst], sem.at[1]).start()
        return carry

    lax.fori_loop(0, n_slabs, issue, 0)
    pltpu.make_async_copy(kbuf, kbuf, sem.at[0]).wait()
    pltpu.make_async_copy(vbuf, vbuf, sem.at[1]).wait()

    nt = (((1,), (1,)), ((), ()))
    scale = HEAD_DIM ** -0.5
    k_new = knew_ref[0].astype(BF16)
    v_new = vnew_ref[0].astype(BF16)
    heads_per_kv = N_HEADS // N_KV_HEADS

    def own_kv(width):
        head = lax.broadcasted_iota(I32, (N_HEADS, width), 0)
        col = lax.broadcasted_iota(I32, (N_HEADS, width), 1)
        return (col & (N_KV_HEADS - 1)) == (head >> (heads_per_kv.bit_length() - 1))

    for q in range(DEC_SEQ):
        q_h = q_ref[0, q].astype(BF16)
        rows = slice(q * SA_COLS, (q + 1) * SA_COLS)
        s = lax.dot_general(q_h, kbuf[rows, :].astype(BF16), nt, preferred_element_type=F32) * scale
        s = jnp.where(own_kv(SA_COLS), s + gbias_ref[0, q:q + 1, :], NEG_INF)
        sn = lax.dot_general(q_h, k_new, nt, preferred_element_type=F32) * scale
        sn = jnp.where(own_kv(LANES), sn + nbias_ref[0, q:q + 1, :], NEG_INF)
        m = jnp.maximum(jnp.max(s, axis=1, keepdims=True), jnp.max(sn, axis=1, keepdims=True))
        p = jnp.exp(s - m)
        pn = jnp.exp(sn - m)
        l = jnp.sum(p, axis=1, keepdims=True) + jnp.sum(pn, axis=1, keepdims=True)
        o = (jnp.dot(p.astype(BF16), vbuf[rows, :].astype(BF16), preferred_element_type=F32)
             + jnp.dot(pn.astype(BF16), v_new, preferred_element_type=F32)) / l
        o_ref[0, q] = o


def _sample_attention(rows_flat, gbias, nbias, q_s, k_new, v_new, cache_k2d, cache_v2d):
    grid_spec = pltpu.PrefetchScalarGridSpec(
        num_scalar_prefetch=1,
        grid=(DEC_BATCH,),
        in_specs=[pl.BlockSpec((1, DEC_SEQ, SA_COLS), lambda b, r: (b, 0, 0)),
                  pl.BlockSpec((1, DEC_SEQ, LANES), lambda b, r: (b, 0, 0)),
                  pl.BlockSpec((1, DEC_SEQ, N_HEADS, HEAD_DIM), lambda b, r: (b, 0, 0, 0)),
                  pl.BlockSpec((1, LANES, HEAD_DIM), lambda b, r: (b, 0, 0)),
                  pl.BlockSpec((1, LANES, HEAD_DIM), lambda b, r: (b, 0, 0)),
                  pl.BlockSpec(memory_space=pl.ANY),
                  pl.BlockSpec(memory_space=pl.ANY)],
        out_specs=pl.BlockSpec((1, DEC_SEQ, N_HEADS, HEAD_DIM), lambda b, r: (b, 0, 0, 0)),
        scratch_shapes=[pltpu.VMEM((DEC_SEQ * SA_COLS, HEAD_DIM), F32),
                        pltpu.VMEM((DEC_SEQ * SA_COLS, HEAD_DIM), F32),
                        pltpu.SemaphoreType.DMA((2,))])
    return pl.pallas_call(
        _sattn_kernel,
        grid_spec=grid_spec,
        out_shape=jax.ShapeDtypeStruct((DEC_BATCH, DEC_SEQ, N_HEADS, HEAD_DIM), F32),
        compiler_params=_cparams(("arbitrary",), disable_bounds_checks=True),
        name="sample_attention",
    )(rows_flat, gbias, nbias, q_s, k_new, v_new, cache_k2d, cache_v2d)


def _pool_kernel(p_ref, halo_ref, first_ref, cnt_ref, o_ref, ext_scr, *, tiles_per_batch):
    i = pl.program_id(0)
    tm = p_ref.shape[0]
    ext_scr[0:N_META, :] = jnp.where(i % tiles_per_batch == 0, first_ref[...], halo_ref[...])
    ext_scr[N_META:, :] = p_ref[...]
    for g, win in enumerate(POOL_WINDOWS):
        cols = slice(g * POOL_GROUP, (g + 1) * POOL_GROUP)
        x_self = ext_scr[N_META:N_META + tm, cols]
        acc = x_self
        for d in range(1, win):
            acc = acc + ext_scr[N_META - d:N_META - d + tm, cols]
        o_ref[:, cols] = (acc / cnt_ref[:, cols] - x_self).astype(o_ref.dtype)


def _pool(p_src, p_col_block, halo_src, first_src, first_block, cnt, *, rows, tm, tiles_per_batch):
    kern = functools.partial(_pool_kernel, tiles_per_batch=tiles_per_batch)
    halo_per_tile = tm // N_META
    cnt_rows = cnt.shape[0]
    cnt_map = (lambda i: (0, 0)) if cnt_rows == 1 else (lambda i: (i, 0))
    return pl.pallas_call(
        kern,
        grid=(rows // tm,),
        in_specs=[pl.BlockSpec((tm, POOL_WIDTH), lambda i: (i, p_col_block)),
                  pl.BlockSpec((N_META, POOL_WIDTH),
                               lambda i: (jnp.maximum(i * halo_per_tile - 1, 0), p_col_block)),
                  pl.BlockSpec((N_META, POOL_WIDTH), lambda i: first_block),
                  pl.BlockSpec((1 if cnt_rows == 1 else tm, POOL_WIDTH), cnt_map)],
        out_specs=pl.BlockSpec((tm, POOL_WIDTH), lambda i: (i, 0)),
        out_shape=jax.ShapeDtypeStruct((rows, POOL_WIDTH), BF16),
        scratch_shapes=[pltpu.VMEM((tm + N_META, POOL_WIDTH), F32)],
        compiler_params=_cparams(("arbitrary",)),
        name="pool",
    )(p_src, halo_src, first_src, cnt)


def _layer_norm(y, g_ref, b_ref):
    mu = jnp.mean(y, axis=-1, keepdims=True)
    var = jnp.mean(jnp.square(y - mu), axis=-1, keepdims=True)
    return (y - mu) * lax.rsqrt(var + LN_EPS) * g_ref[...] + b_ref[...]


def _mix_kernel(pooled_ref, a_ref, x_ref, wp_ref, ps_ref, wo_ref, g_ref, b_ref, x1_ref, x1b_ref):
    parts = []
    for g in range(len(POOL_WINDOWS)):
        cols = slice(g * POOL_GROUP, (g + 1) * POOL_GROUP)
        parts.append(jnp.dot(pooled_ref[:, cols], wp_ref[g], preferred_element_type=F32))
    m = jnp.concatenate(parts, axis=1) * ps_ref[...]
    mix = (jnp.dot(a_ref[...], wo_ref[:ATTN_WIDTH, :], preferred_element_type=F32)
           + jnp.dot(m.astype(BF16), wo_ref[ATTN_WIDTH:, :], preferred_element_type=F32))
    x1 = _layer_norm(DEEPNORM_ALPHA * x_ref[...] + mix, g_ref, b_ref)
    x1_ref[...] = x1
    x1b_ref[...] = x1.astype(BF16)


def _mix(pooled, a, x, w_pool, pool_scale, w_out, g1, b1, tm):
    rows = x.shape[0]
    row = lambda i: (i, 0)
    const2 = lambda i: (0, 0)
    return pl.pallas_call(
        _mix_kernel,
        grid=(rows // tm,),
        in_specs=[pl.BlockSpec((tm, POOL_WIDTH), row),
                  pl.BlockSpec((tm, ATTN_WIDTH), row),
                  pl.BlockSpec((tm, D_MODEL), row),
                  pl.BlockSpec((len(POOL_WINDOWS), POOL_GROUP, POOL_GROUP), lambda i: (0, 0, 0)),
                  pl.BlockSpec((1, POOL_WIDTH), const2),
                  pl.BlockSpec((D_MODEL, D_MODEL), const2),
                  pl.BlockSpec((1, D_MODEL), const2),
                  pl.BlockSpec((1, D_MODEL), const2)],
        out_specs=[pl.BlockSpec((tm, D_MODEL), row), pl.BlockSpec((tm, D_MODEL), row)],
        out_shape=[jax.ShapeDtypeStruct((rows, D_MODEL), F32), jax.ShapeDtypeStruct((rows, D_MODEL), BF16)],
        compiler_params=_cparams(("arbitrary",)),
        name="mix_ln1",
    )(pooled, a, x, w_pool, pool_scale, w_out, g1, b1)


PA_GROUP = 4
TF = 512
TF_SUB = 256
N_FF_CHUNKS = D_FF // TF
CONV_HALO = 8


def _silu(x):
    return x / (1.0 + jnp.exp(-x))


def _ffn_accumulate(gate, val, wd_ref, x1_ref, g_ref, b_ref, o_ref, acc_ref):
    c = pl.program_id(1)
    h = (_silu(gate) * val).astype(BF16)
    part = jnp.dot(h, wd_ref[...], preferred_element_type=F32)

    @pl.when(c == 0)
    def _():
        acc_ref[...] = part

    @pl.when(c > 0)
    def _():
        acc_ref[...] += part

    @pl.when(c == N_FF_CHUNKS - 1)
    def _():
        o_ref[...] = _layer_norm(DEEPNORM_ALPHA * x1_ref[...] + acc_ref[...], g_ref, b_ref)


def _ffn_fused_kernel(x1_ref, wg_ref, wv_ref, fg_ref, fv_ref, cwg_ref, cwv_ref, cbg_ref, cbv_ref,
                      wd_ref, g_ref, b_ref, o_ref, tg_ref, tv_ref, xb_scr, acc_ref, carry_g, carry_v, ext_scr,
                      *, tiles_per_batch):
    i = pl.program_id(0)
    c = pl.program_id(1)
    tm = x1_ref.shape[0]
    first = i % tiles_per_batch == 0

    @pl.when(c == 0)
    def _():
        xb_scr[...] = x1_ref[...].astype(BF16)
        acc_ref[...] = jnp.zeros(acc_ref.shape, F32)

    @pl.when(i == 0)
    def _():
        carry_g[c] = jnp.zeros((CONV_HALO, TF), F32)
        carry_v[c] = jnp.zeros((CONV_HALO, TF), F32)

    part = None
    for s in range(TF // TF_SUB):
        cols = slice(s * TF_SUB, (s + 1) * TF_SUB)

        def conv(w_ref, f_ref, carry, tail_ref, cw_ref, cb_ref, ext):
            u = jnp.dot(xb_scr[...], w_ref[:, cols], preferred_element_type=F32)
            ext[0:CONV_HALO, :] = jnp.where(first, f_ref[:, cols], carry[c, :, cols])
            ext[CONV_HALO:, :] = u
            tail = u[tm - CONV_HALO:, :]
            carry[c, :, cols] = tail
            tail_ref[0, :, cols] = tail
            return (cb_ref[:, cols] + cw_ref[0:1, cols] * ext[CONV_HALO - 2:CONV_HALO - 2 + tm, :]
                    + cw_ref[1:2, cols] * ext[CONV_HALO - 1:CONV_HALO - 1 + tm, :]
                    + cw_ref[2:3, cols] * u)

        gate = conv(wg_ref, fg_ref, carry_g, tg_ref, cwg_ref, cbg_ref, ext_scr.at[2 * s])
        val = conv(wv_ref, fv_ref, carry_v, tv_ref, cwv_ref, cbv_ref, ext_scr.at[2 * s + 1])
        h = (_silu(gate) * val).astype(BF16)
        d = jnp.dot(h, wd_ref[cols, :], preferred_element_type=F32)
        part = d if part is None else part + d
    acc_ref[...] += part

    @pl.when(c == N_FF_CHUNKS - 1)
    def _():
        o_ref[...] = _layer_norm(DEEPNORM_ALPHA * x1_ref[...] + acc_ref[...], g_ref, b_ref)


def _ffn_prev_kernel(ug_ref, uv_ref, p1g_ref, p1v_ref, p2g_ref, p2v_ref, cwg_ref, cwv_ref, cbg_ref, cbv_ref,
                     wd_ref, x1_ref, g_ref, b_ref, o_ref, acc_ref):
    def conv(u_ref, p1_ref, p2_ref, cw_ref, cb_ref):
        return (cb_ref[...] + cw_ref[0:1, :] * p2_ref[...] + cw_ref[1:2, :] * p1_ref[...]
                + cw_ref[2:3, :] * u_ref[...])

    gate = conv(ug_ref, p1g_ref, p2g_ref, cwg_ref, cbg_ref)
    val = conv(uv_ref, p1v_ref, p2v_ref, cwv_ref, cbv_ref)
    _ffn_accumulate(gate, val, wd_ref, x1_ref, g_ref, b_ref, o_ref, acc_ref)


def _ffn_common_specs(tm):
    gate_c = lambda i, c: (0, c)
    val_c = lambda i, c: (0, N_FF_CHUNKS + c)
    return [pl.BlockSpec((3, TF), gate_c), pl.BlockSpec((3, TF), val_c),
            pl.BlockSpec((1, TF), gate_c), pl.BlockSpec((1, TF), val_c),
            pl.BlockSpec((TF, D_MODEL), lambda i, c: (c, 0)),
            pl.BlockSpec((tm, D_MODEL), lambda i, c: (i, 0)),
            pl.BlockSpec((1, D_MODEL), lambda i, c: (0, 0)),
            pl.BlockSpec((1, D_MODEL), lambda i, c: (0, 0))]


def _ffn_fused(x1, w_up, u_first, first_row_block, conv_w, conv_b, w_down, g2, b2, *, tm, tiles_per_batch):
    rows = x1.shape[0]
    kern = functools.partial(_ffn_fused_kernel, tiles_per_batch=tiles_per_batch)
    gate_c = lambda i, c: (0, c)
    val_c = lambda i, c: (0, N_FF_CHUNKS + c)
    const = lambda i, c: (0, 0)
    tail_spec = pl.BlockSpec((1, CONV_HALO, TF), lambda i, c: (i, 0, c))
    tail_shape = jax.ShapeDtypeStruct((rows // tm, CONV_HALO, D_FF), F32)
    y, tail_g, tail_v = pl.pallas_call(
        kern,
        grid=(rows // tm, N_FF_CHUNKS),
        in_specs=[pl.BlockSpec((tm, D_MODEL), lambda i, c: (i, 0)),
                  pl.BlockSpec((D_MODEL, TF), gate_c), pl.BlockSpec((D_MODEL, TF), val_c),
                  pl.BlockSpec((CONV_HALO, TF), lambda i, c: (first_row_block, c)),
                  pl.BlockSpec((CONV_HALO, TF), lambda i, c: (first_row_block, N_FF_CHUNKS + c)),
                  pl.BlockSpec((3, TF), gate_c), pl.BlockSpec((3, TF), val_c),
                  pl.BlockSpec((1, TF), gate_c), pl.BlockSpec((1, TF), val_c),
                  pl.BlockSpec((TF, D_MODEL), lambda i, c: (c, 0)),
                  pl.BlockSpec((1, D_MODEL), const), pl.BlockSpec((1, D_MODEL), const)],
        out_specs=[pl.BlockSpec((tm, D_MODEL), lambda i, c: (i, 0)), tail_spec, tail_spec],
        out_shape=[jax.ShapeDtypeStruct((rows, D_MODEL), F32), tail_shape, tail_shape],
        scratch_shapes=[pltpu.VMEM((tm, D_MODEL), BF16), pltpu.VMEM((tm, D_MODEL), F32),
                        pltpu.VMEM((N_FF_CHUNKS, CONV_HALO, TF), F32),
                        pltpu.VMEM((N_FF_CHUNKS, CONV_HALO, TF), F32),
                        pltpu.VMEM((2 * (TF // TF_SUB), tm + CONV_HALO, TF_SUB), F32)],
        compiler_params=_cparams(("arbitrary", "arbitrary")),
        name="ffn",
    )(x1, w_up, w_up, u_first, u_first, conv_w, conv_w, conv_b, conv_b, w_down, g2, b2)
    last = slice(tiles_per_batch - 1, None, tiles_per_batch)
    return y, tail_g[last], tail_v[last]


def _ffn_tail_prev(u, prev1, prev2, conv_w, conv_b, w_down, x1, g2, b2, *, tm):
    rows = u.shape[0]
    gate_t = lambda i, c: (i, c)
    val_t = lambda i, c: (i, N_FF_CHUNKS + c)
    return pl.pallas_call(
        _ffn_prev_kernel,
        grid=(rows // tm, N_FF_CHUNKS),
        in_specs=[pl.BlockSpec((tm, TF), gate_t), pl.BlockSpec((tm, TF), val_t),
                  pl.BlockSpec((tm, TF), gate_t), pl.BlockSpec((tm, TF), val_t),
                  pl.BlockSpec((tm, TF), gate_t), pl.BlockSpec((tm, TF), val_t)]
                 + _ffn_common_specs(tm),
        out_specs=pl.BlockSpec((tm, D_MODEL), lambda i, c: (i, 0)),
        out_shape=jax.ShapeDtypeStruct((rows, D_MODEL), F32),
        scratch_shapes=[pltpu.VMEM((tm, D_MODEL), F32)],
        compiler_params=_cparams(("arbitrary", "arbitrary")),
        name="ffn_tail_small",
    )(u, u, prev1, prev1, prev2, prev2, conv_w, conv_w, conv_b, conv_b, w_down, x1, g2, b2)


def kernel(x_prompt, x_sample, cache_k, cache_v, cache_idx_k, state_pool, state_conv, page_table, meta_tokens,
           w_in, w_pool, pool_scale, w_out, ln1_g, ln1_b, w_up, conv_w, conv_b, w_down, ln2_g, ln2_b):
    n_phys = cache_k.shape[1]

    wq, wk, wv, wiq, wik, wiw, wp = jnp.split(w_in[0], IN_OFFSETS, axis=1)
    zcols = lambda n: jnp.zeros((D_MODEL, n), F32)
    w_all = jnp.concatenate([wq, wp, wk, wv, wiq, wik, zcols(64), wiw, zcols(120)], axis=1).astype(BF16)
    w_pool_b = w_pool[0].astype(BF16)
    w_out_b = w_out[0].astype(BF16)
    w_up_b = w_up[0].astype(BF16)
    w_down_b = w_down[0].astype(BF16)
    row2 = lambda a: a.reshape(1, -1)

    pos_prompt = N_META + jnp.arange(SEQ)
    pos_small = jnp.concatenate([jnp.tile(PAST_LEN + jnp.arange(DEC_SEQ), DEC_BATCH), jnp.arange(N_META),
                                 jnp.zeros((SMALL_ROWS - N_SAMPLE_ROWS - N_META,), I32)])
    tabs_p = _rope_tables(pos_prompt, ROT_DIM, HEAD_DIM) + _rope_tables(pos_prompt, IDX_ROT_DIM, IDX_DIM)
    tabs_s = _rope_tables(pos_small, ROT_DIM, HEAD_DIM) + _rope_tables(pos_small, IDX_ROT_DIM, IDX_DIM)

    xp = x_prompt.reshape(N_PROMPT_ROWS, D_MODEL)
    xs = jnp.concatenate([x_sample.reshape(N_SAMPLE_ROWS, D_MODEL), meta_tokens.astype(F32),
                          jnp.zeros((SMALL_ROWS - N_SAMPLE_ROWS - N_META, D_MODEL), F32)], axis=0)

    tm_p = 1024
    hp = _project(xp, w_all, tabs_p, SEQ // tm_p, tm_p)
    hs = _project(xs, w_all, tabs_s, 1, SMALL_ROWS)

    meta = slice(META_ROW0, META_ROW0 + N_META)

    def with_meta(c0, width):
        m = jnp.broadcast_to(hs[meta, c0:c0 + width][None], (BATCH, N_META, width))
        return jnp.concatenate([m, hp[:, c0:c0 + width].reshape(BATCH, SEQ, width)], axis=1)

    k_prompt = with_meta(C_K, KV_WIDTH)
    v_prompt = with_meta(C_V, KV_WIDTH)
    ik_prompt = with_meta(C_IK, IDX_DIM)
    pad_keys = lambda a: jnp.pad(a.astype(BF16), ((0, 0), (0, NK_PROMPT - T_PROMPT), (0, 0)))
    kb, vb, ikb = pad_keys(k_prompt), pad_keys(v_prompt), pad_keys(ik_prompt)

    blocks_per_batch = SEQ // LANES
    a_groups = []
    for j0 in range(0, blocks_per_batch, PA_GROUP):
        nk = N_META + (j0 + PA_GROUP) * LANES
        nk = -(-nk // LANES) * LANES
        a_groups.append(_prompt_attention(hp, kb, vb, ikb, n_batch=BATCH, n_blocks=PA_GROUP, row_block0=j0,
                                          blocks_per_batch=blocks_per_batch, pos_first=N_META + j0 * LANES,
                                          nk=nk))
    a_p = jnp.concatenate(a_groups, axis=1).reshape(N_PROMPT_ROWS, ATTN_WIDTH)
    a_m = _prompt_attention(hs, kb, vb, ikb, n_batch=1, n_blocks=1, row_block0=META_ROW0 // LANES,
                            blocks_per_batch=1, pos_first=0, nk=LANES)[0]

    hs_s = hs[:N_SAMPLE_ROWS]
    iq_s = hs_s[:, C_IQ:C_IQ + N_IDX_HEADS * IDX_DIM].reshape(SS_STEPS, SS_PAIR, DEC_SEQ, N_IDX_HEADS, IDX_DIM)
    iq_s = iq_s.transpose(0, 1, 3, 2, 4)
    iq2 = jnp.stack([jnp.pad(iq_s[:, e], ((0, 0), (0, 0), (DEC_SEQ * e, DEC_SEQ * (SS_PAIR - 1 - e)), (0, 0)))
                     for e in range(SS_PAIR)], axis=1).reshape(SS_STEPS, SS_PAIR, N_IDX_HEADS * 8, IDX_DIM)
    w_s = hs_s[:, C_IW:C_IW + N_IDX_HEADS].reshape(SS_STEPS, SS_PAIR * DEC_SEQ, N_IDX_HEADS)
    w2 = w_s.transpose(0, 2, 1).reshape(SS_STEPS, N_IDX_HEADS * 8, 1)
    ik_new_t = jnp.pad(hs_s[:, C_IK:C_IK + IDX_DIM].reshape(DEC_BATCH, DEC_SEQ, IDX_DIM).transpose(0, 2, 1),
                       ((0, 0), (0, 0), (0, LANES - DEC_SEQ))).reshape(SS_STEPS, SS_PAIR, IDX_DIM, LANES)

    pt_pad = jnp.pad(page_table, ((0, 0), (0, N_MROWS - N_PAGES)))
    pt_hi = jnp.broadcast_to((pt_pad // 64).astype(F32)[:, None, :], (DEC_BATCH, 8, N_MROWS))
    pt_lo = jnp.broadcast_to((pt_pad % 64).astype(F32)[:, None, :], (DEC_BATCH, 8, N_MROWS))
    rows_sel, gbias, nbias = _sample_select(page_table, iq2, w2, ik_new_t, pt_hi, pt_lo,
                                            jnp.swapaxes(cache_idx_k[0], 1, 2))

    def new_rows(c0):
        a = hs_s[:, c0:c0 + KV_WIDTH].reshape(DEC_BATCH, DEC_SEQ * N_KV_HEADS, HEAD_DIM)
        return jnp.pad(a, ((0, 0), (0, LANES - DEC_SEQ * N_KV_HEADS), (0, 0)))

    a_s4 = _sample_attention(rows_sel.reshape(-1), gbias.reshape(DEC_BATCH, DEC_SEQ, SA_COLS),
                             nbias.reshape(DEC_BATCH, DEC_SEQ, LANES),
                             hs_s[:, C_Q:C_Q + ATTN_WIDTH].reshape(DEC_BATCH, DEC_SEQ, N_HEADS, HEAD_DIM),
                             new_rows(C_K), new_rows(C_V),
                             cache_k[0].reshape(n_phys * PAGE_SIZE * N_KV_HEADS, HEAD_DIM),
                             cache_v[0].reshape(n_phys * PAGE_SIZE * N_KV_HEADS, HEAD_DIM))
    a_s = a_s4.reshape(N_SAMPLE_ROWS, ATTN_WIDTH).astype(BF16)
    a_small = jnp.concatenate([a_s, a_m[:N_META], jnp.zeros((SMALL_ROWS - N_SAMPLE_ROWS - N_META, ATTN_WIDTH),
                                                            BF16)], axis=0)

    win = jnp.repeat(jnp.asarray(POOL_WINDOWS, F32), POOL_GROUP)
    p_meta = hs[meta, C_P:C_P + POOL_WIDTH]
    tm_pool = 256
    pooled_p = _pool(hp, C_P // POOL_WIDTH, hp, hs, (META_ROW0 // N_META, C_P // POOL_WIDTH), row2(win),
                     rows=N_PROMPT_ROWS, tm=tm_pool, tiles_per_batch=SEQ // tm_pool)
    p_s = hs_s[:, C_P:C_P + POOL_WIDTH].reshape(DEC_BATCH, DEC_SEQ, POOL_WIDTH)
    grp = 24
    ext_s = jnp.concatenate([jnp.zeros((DEC_BATCH, 1, POOL_WIDTH), F32), state_pool[0], p_s,
                             jnp.zeros((DEC_BATCH, grp - 1 - POOL_CTX - DEC_SEQ, POOL_WIDTH), F32)], axis=1)
    ext_small = jnp.concatenate([ext_s.reshape(DEC_BATCH * grp, POOL_WIDTH),
                                 jnp.zeros((N_META, POOL_WIDTH), F32), p_meta], axis=0)
    n_ext = DEC_BATCH * grp + 2 * N_META
    cnt_meta = jnp.minimum(win[None, :], (jnp.arange(N_META, dtype=F32) + 1.0)[:, None])
    cnt_small = jnp.concatenate([jnp.broadcast_to(win[None], (n_ext - N_META, POOL_WIDTH)), cnt_meta], axis=0)
    zeros_halo = jnp.zeros((N_META, POOL_WIDTH), F32)
    pooled_ext = _pool(ext_small, 0, zeros_halo, zeros_halo, (0, 0), cnt_small,
                       rows=n_ext, tm=n_ext, tiles_per_batch=1)
    pooled_small = jnp.concatenate(
        [pooled_ext[:DEC_BATCH * grp].reshape(DEC_BATCH, grp, POOL_WIDTH)[:, 16:16 + DEC_SEQ].reshape(
            N_SAMPLE_ROWS, POOL_WIDTH),
         pooled_ext[n_ext - N_META:],
         jnp.zeros((SMALL_ROWS - N_SAMPLE_ROWS - N_META, POOL_WIDTH), BF16)], axis=0)

    mix_args = (w_pool_b, row2(pool_scale[0]), w_out_b, row2(ln1_g[0]), row2(ln1_b[0]))
    x1_p, x1b_p = _mix(pooled_p, a_p, xp, *mix_args, tm=256)
    x1_s, x1b_s = _mix(pooled_small, a_small, xs, *mix_args, tm=SMALL_ROWS)

    u_s = _matmul(x1b_s, w_up_b, SMALL_ROWS, "ffn_up_small")
    ffn_args = (conv_w[0], row2(conv_b[0]), w_down_b)
    ln2 = (row2(ln2_g[0]), row2(ln2_b[0]))
    tm_f = 512
    y_p, tail_g, tail_v = _ffn_fused(x1_p, w_up_b, u_s, (META_ROW0 + N_META) // CONV_HALO - 1, *ffn_args, *ln2,
                                     tm=tm_f, tiles_per_batch=SEQ // tm_f)
    u_s3 = u_s[:N_SAMPLE_ROWS].reshape(DEC_BATCH, DEC_SEQ, 2 * D_FF)
    ext_u = jnp.concatenate([state_conv[0], u_s3], axis=1)
    u_m = u_s[meta]
    ext_m = jnp.concatenate([jnp.zeros((2, 2 * D_FF), F32), u_m], axis=0)
    tail0 = jnp.zeros((SMALL_ROWS - N_SAMPLE_ROWS - N_META, 2 * D_FF), F32)
    prev1 = jnp.concatenate([ext_u[:, 1:1 + DEC_SEQ].reshape(N_SAMPLE_ROWS, -1), ext_m[1:1 + N_META], tail0], 0)
    prev2 = jnp.concatenate([ext_u[:, 0:DEC_SEQ].reshape(N_SAMPLE_ROWS, -1), ext_m[0:N_META], tail0], 0)
    y_s = _ffn_tail_prev(u_s, prev1, prev2, *ffn_args, x1_s, *ln2, tm=SMALL_ROWS)

    y_prompt = y_p.reshape(BATCH, SEQ, D_MODEL)
    y_sample = y_s[:N_SAMPLE_ROWS].reshape(DEC_BATCH, DEC_SEQ, D_MODEL)
    pool_prompt = hp[:, C_P:C_P + POOL_WIDTH].reshape(BATCH, SEQ, POOL_WIDTH)[:, SEQ - POOL_CTX:]
    conv_prompt = jnp.concatenate([tail_g[:, CONV_HALO - 2:], tail_v[:, CONV_HALO - 2:]], axis=-1)
    k_sample = hs_s[:, C_K:C_K + KV_WIDTH].reshape(DEC_BATCH, DEC_SEQ, N_KV_HEADS, HEAD_DIM)
    v_sample = hs_s[:, C_V:C_V + KV_WIDTH].reshape(DEC_BATCH, DEC_SEQ, N_KV_HEADS, HEAD_DIM)
    ik_sample = hs_s[:, C_IK:C_IK + IDX_DIM].reshape(DEC_BATCH, DEC_SEQ, IDX_DIM)
    pool_sample = jnp.concatenate([state_pool[0], p_s], axis=1)[:, DEC_SEQ:]
    conv_sample = ext_u[:, DEC_SEQ:]
    return (y_prompt, y_sample,
            k_prompt.reshape(1, BATCH, T_PROMPT, N_KV_HEADS, HEAD_DIM),
            v_prompt.reshape(1, BATCH, T_PROMPT, N_KV_HEADS, HEAD_DIM),
            ik_prompt[None], pool_prompt[None], conv_prompt[None],
            k_sample[None], v_sample[None], ik_sample[None], pool_sample[None], conv_sample[None])
```

```python
import functools

import numpy as np
import jax
import jax.numpy as jnp
from jax import lax
from jax.experimental import pallas as pl
from jax.experimental.pallas import tpu as pltpu

F32 = jnp.float32
BF16 = jnp.bfloat16
I32 = jnp.int32

D_MODEL = 2048
BATCH = 8
SEQ = 2048
DEC_BATCH = 32
DEC_SEQ = 4
PAST_LEN = 16384
PAGE_SIZE = 128
N_META = 16
ATTN_WIDTH = 1024
N_HEADS = 8
HEAD_DIM = 128
N_KV_HEADS = 4
KV_WIDTH = N_KV_HEADS * HEAD_DIM
ROT_DIM = 32
N_IDX_HEADS = 8
IDX_DIM = 64
IDX_ROT_DIM = 16
TOPK = 256
POOL_WIDTH = 1024
POOL_WINDOWS = (2, 4, 8, 16)
POOL_GROUP = 256
POOL_CTX = 15
D_FF = 5632
ROPE_THETA = 500000.0
LN_EPS = 1e-5
DEEPNORM_ALPHA = 2.0 ** 0.25
IN_OFFSETS = (1024, 1536, 2048, 2560, 2624, 2632)

LANES = 128
N_PROMPT_ROWS = BATCH * SEQ
N_SAMPLE_ROWS = DEC_BATCH * DEC_SEQ
SMALL_ROWS = 256
META_ROW0 = N_SAMPLE_ROWS
T_PROMPT = N_META + SEQ
NK_PROMPT = 2176
NK_SAMPLE = PAST_LEN + LANES
N_PAGES = PAST_LEN // PAGE_SIZE

C_Q, C_P, C_K, C_V, C_IQ, C_IK, C_IW = 0, 1024, 2048, 2560, 3072, 3584, 3712
PROJ_WIDTH = 3840
TN = 256
ROPE_A_TILES = (0, 1, 2, 3, 8, 9)
ROPE_B_TILES = (12, 13)
ROPE_BH_TILE = 14

VMEM_LIMIT = 56 * 1024 * 1024
INT_MIN = -2 ** 31
NEG_INF = float("-inf")
LOG2_E = 1.4426950408889634


def _cparams(sem, **kw):
    return pltpu.CompilerParams(dimension_semantics=sem, vmem_limit_bytes=VMEM_LIMIT, **kw)


def _rope_tables(pos, rot_dim, head_dim):
    half = rot_dim // 2
    inv_freq = ROPE_THETA ** (-jnp.arange(half, dtype=F32) / half)
    ang = pos.astype(F32)[:, None] * inv_freq[None, :]
    cos, sin = jnp.cos(ang), jnp.sin(ang)
    rows = pos.shape[0]
    zh = jnp.zeros((rows, half), F32)
    rest0 = jnp.zeros((rows, head_dim - rot_dim), F32)
    c = jnp.concatenate([cos, cos, rest0 + 1.0], axis=1)
    s1 = jnp.concatenate([-sin, zh, rest0], axis=1)
    s2 = jnp.concatenate([zh, sin, rest0], axis=1)
    reps = LANES // head_dim
    return tuple(jnp.tile(t, (1, reps)) for t in (c, s1, s2))


def _rope(x, c_ref, s1_ref, s2_ref, half):
    return (x * c_ref[...] + pltpu.roll(x, LANES - half, 1) * s1_ref[...]
            + pltpu.roll(x, half, 1) * s2_ref[...])


def _proj_kernel(x_ref, w_ref, ca, sa1, sa2, cb, sb1, sb2, o_ref, xb_ref):
    j = pl.program_id(1)

    @pl.when(j == 0)
    def _():
        xb_ref[...] = x_ref[...].astype(BF16)

    acc = jnp.dot(xb_ref[...], w_ref[...], preferred_element_type=F32)
    is_a = functools.reduce(jnp.logical_or, [j == t for t in ROPE_A_TILES])
    is_b = functools.reduce(jnp.logical_or, [j == t for t in ROPE_B_TILES])
    is_bh = j == ROPE_BH_TILE

    @pl.when(is_a)
    def _():
        o_ref[:, :LANES] = _rope(acc[:, :LANES], ca, sa1, sa2, ROT_DIM // 2)
        o_ref[:, LANES:] = _rope(acc[:, LANES:], ca, sa1, sa2, ROT_DIM // 2)

    @pl.when(is_b)
    def _():
        o_ref[:, :LANES] = _rope(acc[:, :LANES], cb, sb1, sb2, IDX_ROT_DIM // 2)
        o_ref[:, LANES:] = _rope(acc[:, LANES:], cb, sb1, sb2, IDX_ROT_DIM // 2)

    @pl.when(is_bh)
    def _():
        o_ref[:, :LANES] = _rope(acc[:, :LANES], cb, sb1, sb2, IDX_ROT_DIM // 2)
        o_ref[:, LANES:] = acc[:, LANES:]

    @pl.when(jnp.logical_not(is_a | is_b | is_bh))
    def _():
        o_ref[...] = acc


def _project(x, w_all, tabs, tab_blocks, tm):
    rows = x.shape[0]
    tab_spec = pl.BlockSpec((tm, LANES), lambda i, j: (i % tab_blocks, 0))
    return pl.pallas_call(
        _proj_kernel,
        grid=(rows // tm, PROJ_WIDTH // TN),
        in_specs=[pl.BlockSpec((tm, D_MODEL), lambda i, j: (i, 0)),
                  pl.BlockSpec((D_MODEL, TN), lambda i, j: (0, j))] + [tab_spec] * 6,
        out_specs=pl.BlockSpec((tm, TN), lambda i, j: (i, j)),
        out_shape=jax.ShapeDtypeStruct((rows, PROJ_WIDTH), F32),
        scratch_shapes=[pltpu.VMEM((tm, D_MODEL), BF16)],
        compiler_params=_cparams(("arbitrary", "arbitrary")),
        name="in_proj",
    )(x, w_all, *tabs)


def _mm_kernel(x_ref, w_ref, o_ref):
    o_ref[...] = jnp.dot(x_ref[...], w_ref[...], preferred_element_type=F32)


def _matmul(x, w, tm, name):
    rows, kdim = x.shape
    n = w.shape[1]
    return pl.pallas_call(
        _mm_kernel,
        grid=(rows // tm, n // TN),
        in_specs=[pl.BlockSpec((tm, kdim), lambda i, j: (i, 0)),
                  pl.BlockSpec((kdim, TN), lambda i, j: (0, j))],
        out_specs=pl.BlockSpec((tm, TN), lambda i, j: (i, j)),
        out_shape=jax.ShapeDtypeStruct((rows, n), F32),
        compiler_params=_cparams(("arbitrary", "arbitrary")),
        name=name,
    )(x, w)


def _ordinal_to_float(key):
    bits = key ^ ((key >> 31) & 0x7FFFFFFF)
    return lax.bitcast_convert_type(bits, F32)


def _select_topk(sc_ref, col, valid, n_col_bits, cut_scr):
    rows = sc_ref.shape[0]
    k_f = float(TOPK)

    def count(pred):
        return jnp.sum(jnp.where(pred, 1.0, 0.0), axis=1, keepdims=True)

    few = count(valid) <= k_f
    key0 = jnp.where(count(sc_ref[...] >= 0.0) >= k_f, 0, INT_MIN).astype(I32)

    def value_step(i, key):
        cand = key + jnp.left_shift(jnp.int32(1), 30 - i)
        cnt = count(sc_ref[...] >= _ordinal_to_float(cand))
        return jnp.where(cnt >= k_f, cand, key)

    key = lax.fori_loop(0, 31, value_step, key0)
    thr = jnp.where(few, NEG_INF, _ordinal_to_float(key))
    need = k_f - count(sc_ref[...] > thr)
    surplus = jnp.where(few, 0.0, count(sc_ref[...] >= thr) - k_f)

    cut_scr[...] = jnp.full(cut_scr.shape, 2 ** n_col_bits - 1, I32)

    @pl.when(jnp.max(surplus) > 0.0)
    def _():
        def col_step(i, cut):
            cand = cut + jnp.left_shift(jnp.int32(1), n_col_bits - 1 - i)
            cnt = count((sc_ref[...] == thr) & (col < cand))
            return jnp.where(cnt < need, cand, cut)

        cut_scr[...] = lax.fori_loop(0, n_col_bits, col_step, jnp.zeros((rows, 1), I32))

    sc = sc_ref[...]
    return valid & ((sc > thr) | ((sc == thr) & (col <= cut_scr[...])))


def _select_topk_tiled(sc_ref, valid_fn, n_col_bits, cut_scr, out_ref):
    rows, n = sc_ref.shape
    k_f = float(TOPK)
    lane = lax.broadcasted_iota(I32, (rows, LANES), 1)
    tiles = [(t * LANES, slice(t * LANES, (t + 1) * LANES)) for t in range(n // LANES)]
    wide = lambda v: jnp.broadcast_to(v, (rows, LANES))

    def count(pred_fn):
        acc = jnp.zeros((rows, LANES), F32)
        for c0, cols in tiles:
            acc = acc + jnp.where(pred_fn(sc_ref[:, cols], lane + c0), 1.0, 0.0)
        return jnp.sum(acc, axis=1, keepdims=True)

    few = count(lambda sc, col: valid_fn(col)) <= k_f
    key0 = jnp.where(count(lambda sc, col: sc >= 0.0) >= k_f, 0, INT_MIN).astype(I32)

    def value_step(i, key):
        cand = wide(_ordinal_to_float(key + jnp.left_shift(jnp.int32(1), 30 - i)))
        cnt = count(lambda sc, col: sc >= cand)
        return jnp.where(cnt >= k_f, key + jnp.left_shift(jnp.int32(1), 30 - i), key)

    key = lax.fori_loop(0, 31, value_step, key0)
    thr = jnp.where(few, NEG_INF, _ordinal_to_float(key))
    thr_w = wide(thr)
    need = k_f - count(lambda sc, col: sc > thr_w)
    surplus = jnp.where(few, 0.0, count(lambda sc, col: sc >= thr_w) - k_f)

    cut_scr[...] = jnp.full(cut_scr.shape, 2 ** n_col_bits - 1, I32)

    @pl.when(jnp.max(surplus) > 0.0)
    def _():
        def col_step(i, cut):
            cand = cut + jnp.left_shift(jnp.int32(1), n_col_bits - 1 - i)
            cand_w = wide(cand)
            cnt = count(lambda sc, col: (sc == thr_w) & (col < cand_w))
            return jnp.where(cnt < need, cand, cut)

        cut_scr[...] = lax.fori_loop(0, n_col_bits, col_step, jnp.zeros((rows, 1), I32))

    cut_w = wide(cut_scr[...])
    for c0, cols in tiles:
        sc, col = sc_ref[:, cols], lane + c0
        keep = valid_fn(col) & ((sc > thr_w) | ((sc == thr_w) & (col <= cut_w)))
        out_ref[:, cols] = jnp.where(keep, 1.0, 0.0)


def _pattn_kernel(q_ref, iq_ref, misc_ref, kb_ref, vb_ref, ikb_ref, o_ref, sc_scr, bias_scr, cut_scr,
                  *, pos_first, nk):
    tq = q_ref.shape[0]
    pos0 = pos_first + tq * pl.program_id(1)

    ik = ikb_ref[0]
    w = misc_ref[:, LANES:LANES + N_IDX_HEADS] * (N_IDX_HEADS ** -0.5)
    score = jnp.zeros((tq, nk), F32)
    for h in range(N_IDX_HEADS):
        iq_h = iq_ref[:, h * IDX_DIM:(h + 1) * IDX_DIM].astype(BF16)
        s_h = lax.dot_general(iq_h, ik, (((1,), (1,)), ((), ())), preferred_element_type=F32)
        score = score + jnp.maximum(s_h * (IDX_DIM ** -0.5), 0.0) * w[:, h:h + 1]

    qpos = pos0 + lax.broadcasted_iota(I32, (tq, nk), 0)
    col = lax.broadcasted_iota(I32, (tq, nk), 1)
    causal = col <= qpos
    sc_scr[...] = jnp.where(causal, score, NEG_INF)
    sel = _select_topk(sc_scr, col, causal, 12, cut_scr)
    bias_scr[...] = jnp.where(sel, 0.0, NEG_INF)

    q_scale = (HEAD_DIM ** -0.5) * LOG2_E
    for h in range(N_HEADS):
        kv = h // (N_HEADS // N_KV_HEADS)
        q_h = (q_ref[:, h * HEAD_DIM:(h + 1) * HEAD_DIM] * q_scale).astype(BF16)
        k_g = kb_ref[0, :, kv * HEAD_DIM:(kv + 1) * HEAD_DIM]
        s = lax.dot_general(q_h, k_g, (((1,), (1,)), ((), ())), preferred_element_type=F32)
        s = s + bias_scr[...]
        m = jnp.max(s, axis=1, keepdims=True)
        p = jnp.exp2(s - m)
        l = jnp.sum(p, axis=1, keepdims=True)
        v_g = vb_ref[0, :, kv * HEAD_DIM:(kv + 1) * HEAD_DIM]
        o = jnp.dot(p.astype(BF16), v_g, preferred_element_type=F32) / l
        o_ref[0, :, h * HEAD_DIM:(h + 1) * HEAD_DIM] = o.astype(o_ref.dtype)


def _prompt_attention(h, kb, vb, ikb, *, tq, n_batch, n_blocks, row_block0, blocks_per_batch, pos_first, nk):
    kern = functools.partial(_pattn_kernel, pos_first=pos_first, nk=nk)
    qrow = lambda b, j: row_block0 + b * blocks_per_batch + j
    return pl.pallas_call(
        kern,
        grid=(n_batch, n_blocks),
        in_specs=[pl.BlockSpec((tq, ATTN_WIDTH), lambda b, j: (qrow(b, j), C_Q // ATTN_WIDTH)),
                  pl.BlockSpec((tq, 512), lambda b, j: (qrow(b, j), C_IQ // 512)),
                  pl.BlockSpec((tq, 256), lambda b, j: (qrow(b, j), C_IK // 256)),
                  pl.BlockSpec((1, nk, KV_WIDTH), lambda b, j: (b, 0, 0)),
                  pl.BlockSpec((1, nk, KV_WIDTH), lambda b, j: (b, 0, 0)),
                  pl.BlockSpec((1, nk, IDX_DIM), lambda b, j: (b, 0, 0))],
        out_specs=pl.BlockSpec((1, tq, ATTN_WIDTH), lambda b, j: (b, j, 0)),
        out_shape=jax.ShapeDtypeStruct((n_batch, n_blocks * tq, ATTN_WIDTH), BF16),
        scratch_shapes=[pltpu.VMEM((tq, nk), F32), pltpu.VMEM((tq, nk), F32),
                        pltpu.VMEM((tq, 1), I32)],
        compiler_params=_cparams(("arbitrary", "arbitrary")),
        name="prompt_attention",
    )(h, h, h, kb, vb, ikb)


KV_SHIFT = 2
N_MROWS = 256


SS_PAIR = 2
SS_STEPS = DEC_BATCH // SS_PAIR
SS_CHUNK = 2048


def _ssel_kernel(pt_ref, iq_ref, w_ref, iknew_ref, pthi_ref, ptlo_ref, cache_ref,
                 rows_ref, gbias_ref, nbias_ref, ikbuf, sem, sc_scr, sel_scr, m_scr, cut_scr):
    s = pl.program_id(0)

    def page_copy(step, slot, e, p):
        return pltpu.make_async_copy(cache_ref.at[pt_ref[SS_PAIR * step + e, p]],
                                     ikbuf.at[slot, e, :, pl.ds(pl.multiple_of(p * PAGE_SIZE, PAGE_SIZE), PAGE_SIZE)],
                                     sem.at[slot])

    def fetch(step, slot):
        def issue(p, carry):
            for e in range(SS_PAIR):
                page_copy(step, slot, e, p).start()
            return carry
        lax.fori_loop(0, N_PAGES, issue, 0)

    slot = s % 2

    @pl.when(s == 0)
    def _():
        fetch(0, 0)

    @pl.when(s + 1 < SS_STEPS)
    def _():
        fetch(s + 1, 1 - slot)

    pltpu.make_async_copy(ikbuf.at[slot], ikbuf.at[slot], sem.at[slot]).wait()

    w = w_ref[0] * (N_IDX_HEADS ** -0.5)
    iq = [iq_ref[0, e].astype(BF16) for e in range(SS_PAIR)]

    def score_of(keys_of):
        r = None
        for e in range(SS_PAIR):
            s_e = jnp.dot(iq[e], keys_of(e).astype(BF16), preferred_element_type=F32)
            r_e = jnp.maximum(s_e * (IDX_DIM ** -0.5), 0.0)
            r = r_e if r is None else r + r_e
        r = r * w
        acc = r[0:8]
        for h in range(1, N_IDX_HEADS):
            acc = acc + r[8 * h:8 * h + 8]
        return acc

    row0 = pl.multiple_of(s * 8, 8)
    for c in range(PAST_LEN // SS_CHUNK):
        cols = slice(c * SS_CHUNK, (c + 1) * SS_CHUNK)
        sc_scr[pl.ds(row0, 8), cols] = score_of(lambda e: ikbuf[slot, e, :, cols])
    sc_scr[pl.ds(row0, 8), PAST_LEN:] = score_of(lambda e: iknew_ref[0, e])

    @pl.when(s == SS_STEPS - 1)
    def _():
        _ssel_finish(pthi_ref, ptlo_ref, rows_ref, gbias_ref, nbias_ref, sc_scr, sel_scr, m_scr, cut_scr)


def _ssel_finish(pthi_ref, ptlo_ref, rows_ref, gbias_ref, nbias_ref, sc_scr, sel_scr, m_scr, cut_scr):
    n_q = sc_scr.shape[0]
    qpos = PAST_LEN + (lax.broadcasted_iota(I32, (n_q, LANES), 0) & (DEC_SEQ - 1))
    valid_fn = lambda col: col <= qpos
    lane = lax.broadcasted_iota(I32, (n_q, LANES), 1)
    for t in range(NK_SAMPLE // LANES):
        cols = slice(t * LANES, (t + 1) * LANES)
        sc_scr[:, cols] = jnp.where(valid_fn(lane + t * LANES), sc_scr[:, cols], NEG_INF)
    _select_topk_tiled(sc_scr, valid_fn, 15, cut_scr, sel_scr)

    nt = (((1,), (1,)), ((), ()))
    ones_b = jnp.ones((8, LANES), BF16)
    upper_pages = (lax.broadcasted_iota(I32, (N_MROWS, N_MROWS), 0)
                   < lax.broadcasted_iota(I32, (N_MROWS, N_MROWS), 1)).astype(BF16)
    upper_lanes = (lax.broadcasted_iota(I32, (LANES, LANES), 0)
                   < lax.broadcasted_iota(I32, (LANES, LANES), 1)).astype(BF16)
    lane_id = lax.broadcasted_iota(I32, (8, LANES), 1).astype(BF16)
    page_id = lax.broadcasted_iota(I32, (8, N_MROWS), 1).astype(BF16)
    jcol = lax.broadcasted_iota(I32, (TOPK, 1), 0).astype(F32)
    spread_picks = (lax.broadcasted_iota(I32, (TOPK, N_KV_HEADS * TOPK), 0)
                    == lax.broadcasted_iota(I32, (TOPK, N_KV_HEADS * TOPK), 1) >> KV_SHIFT).astype(BF16)
    spread_new = (lax.broadcasted_iota(I32, (LANES, LANES), 0)
                  == lax.broadcasted_iota(I32, (LANES, LANES), 1) >> KV_SHIFT).astype(BF16)
    m_scr[...] = jnp.zeros(m_scr.shape, F32)

    def compact(k, batch, new_sel):
        pt_hi = pthi_ref[batch].astype(BF16)
        pt_lo = ptlo_ref[batch].astype(BF16)
        m_b = m_scr[k].astype(BF16)
        cnt = lax.dot_general(ones_b, m_b, nt, preferred_element_type=F32)
        start = jnp.dot(cnt.astype(BF16), upper_pages, preferred_element_type=F32)
        cnt1, start1 = cnt[0:1], start[0:1]
        in_page = (start1 <= jcol) & (jcol < start1 + cnt1)
        a_b = jnp.where(in_page, 1.0, 0.0).astype(BF16)
        m_j = jnp.dot(a_b, m_b, preferred_element_type=F32)
        rank = jnp.dot(m_j.astype(BF16), upper_lanes, preferred_element_type=F32)
        start_j = jnp.sum(jnp.where(in_page, start1, 0.0), axis=1, keepdims=True)
        onehot = jnp.where((m_j > 0.5) & (rank == jcol - start_j), 1.0, 0.0).astype(BF16)
        off = lax.dot_general(lane_id, onehot, nt, preferred_element_type=F32)[0:1]
        page = lax.dot_general(page_id, a_b, nt, preferred_element_type=F32)[0:1]
        phys = (lax.dot_general(pt_hi, a_b, nt, preferred_element_type=F32)[0:1] * 64.0
                + lax.dot_general(pt_lo, a_b, nt, preferred_element_type=F32)[0:1])
        in_past = page < float(N_PAGES)
        row = jnp.where(in_past, phys * float(PAGE_SIZE) + off, 0.0).astype(I32)
        past8 = jnp.broadcast_to(jnp.where(in_past, 1.0, 0.0), (8, TOPK)).astype(BF16)
        past4 = jnp.dot(past8, spread_picks, preferred_element_type=F32)[0:1]
        new8 = jnp.broadcast_to(new_sel, (8, LANES)).astype(BF16)
        new4 = jnp.dot(new8, spread_new, preferred_element_type=F32)[0:1]
        return row, jnp.where(past4 > 0.5, 0.0, NEG_INF), jnp.where(new4 > 0.5, 0.0, NEG_INF)

    def compact_group(g, carry):
        r8 = pl.multiple_of(g * 8, 8)
        for c in range(N_PAGES + 1):
            tile = sel_scr[pl.ds(r8, 8), c * LANES:(c + 1) * LANES]
            for k in range(8):
                m_scr[k, c:c + 1, :] = tile[k:k + 1, :]
        new_tile = sel_scr[pl.ds(r8, 8), PAST_LEN:]
        rowid = lax.broadcasted_iota(I32, (8, 1), 0)
        rows_t = jnp.zeros((8, TOPK), I32)
        gb_t = jnp.zeros((8, N_KV_HEADS * TOPK), F32)
        nb_t = jnp.zeros((8, LANES), F32)
        for k in range(8):
            row, gb, nb = compact(k, g * (8 // DEC_SEQ) + k // DEC_SEQ, new_tile[k:k + 1, :])
            rows_t = jnp.where(rowid == k, row, rows_t)
            gb_t = jnp.where(rowid == k, gb, gb_t)
            nb_t = jnp.where(rowid == k, nb, nb_t)
        rows_ref[pl.ds(r8, 8), :] = rows_t
        gbias_ref[pl.ds(r8, 8), :] = gb_t
        nbias_ref[pl.ds(r8, 8), :] = nb_t
        return carry

    lax.fori_loop(0, n_q // 8, compact_group, 0)


def _sample_select(page_table, iq2, w2, ik_new_t, pt_hi, pt_lo, cache_idx_t):
    n_q = N_SAMPLE_ROWS
    whole = lambda shape: pl.BlockSpec(shape, lambda s, pt: (0,) * len(shape))
    grid_spec = pltpu.PrefetchScalarGridSpec(
        num_scalar_prefetch=1,
        grid=(SS_STEPS,),
        in_specs=[pl.BlockSpec((1, SS_PAIR, 64, IDX_DIM), lambda s, pt: (s, 0, 0, 0)),
                  pl.BlockSpec((1, 64, 1), lambda s, pt: (s, 0, 0)),
                  pl.BlockSpec((1, SS_PAIR, IDX_DIM, LANES), lambda s, pt: (s, 0, 0, 0)),
                  whole((DEC_BATCH, 8, N_MROWS)),
                  whole((DEC_BATCH, 8, N_MROWS)),
                  pl.BlockSpec(memory_space=pl.ANY)],
        out_specs=[whole((n_q, TOPK)), whole((n_q, N_KV_HEADS * TOPK)), whole((n_q, LANES))],
        scratch_shapes=[pltpu.VMEM((2, SS_PAIR, IDX_DIM, PAST_LEN), F32),
                        pltpu.SemaphoreType.DMA((2,)),
                        pltpu.VMEM((n_q, NK_SAMPLE), F32),
                        pltpu.VMEM((n_q, NK_SAMPLE), F32),
                        pltpu.VMEM((8, N_MROWS, LANES), F32),
                        pltpu.VMEM((n_q, 1), I32)])
    return pl.pallas_call(
        _ssel_kernel,
        grid_spec=grid_spec,
        out_shape=[jax.ShapeDtypeStruct((n_q, TOPK), I32),
                   jax.ShapeDtypeStruct((n_q, N_KV_HEADS * TOPK), F32),
                   jax.ShapeDtypeStruct((n_q, LANES), F32)],
        compiler_params=_cparams(("arbitrary",)),
        name="sample_select",
    )(page_table, iq2, w2, ik_new_t, pt_hi, pt_lo, cache_idx_t)


SA_COLS = N_KV_HEADS * TOPK


def _sattn_kernel(rows_ref, gbias_ref, nbias_ref, q_ref, knew_ref, vnew_ref, ck_ref, cv_ref, o_ref,
                  kbuf, vbuf, sem):
    b = pl.program_id(0)
    n_slabs = DEC_SEQ * TOPK

    def issue(t, carry):
        row = pl.multiple_of(rows_ref[b * n_slabs + t] * N_KV_HEADS, N_KV_HEADS)
        dst = pl.ds(pl.multiple_of(t * N_KV_HEADS, N_KV_HEADS), N_KV_HEADS)
        pltpu.make_async_copy(ck_ref.at[pl.ds(row, N_KV_HEADS)], kbuf.at[dst], sem.at[0]).start()
        pltpu.make_async_copy(cv_ref.at[pl.ds(row, N_KV_HEADS)], vbuf.at[dst], sem.at[1]).start()
        return carry

    lax.fori_loop(0, n_slabs, issue, 0, unroll=8)
    pltpu.make_async_copy(kbuf, kbuf, sem.at[0]).wait()
    pltpu.make_async_copy(vbuf, vbuf, sem.at[1]).wait()

    nt = (((1,), (1,)), ((), ()))
    scale = HEAD_DIM ** -0.5
    k_new = knew_ref[0].astype(BF16)
    v_new = vnew_ref[0].astype(BF16)
    heads_per_kv = N_HEADS // N_KV_HEADS

    def own_kv(width):
        head = lax.broadcasted_iota(I32, (N_HEADS, width), 0)
        col = lax.broadcasted_iota(I32, (N_HEADS, width), 1)
        return (col & (N_KV_HEADS - 1)) == (head >> (heads_per_kv.bit_length() - 1))

    for q in range(DEC_SEQ):
        q_h = q_ref[0, q].astype(BF16)
        rows = slice(q * SA_COLS, (q + 1) * SA_COLS)
        s = lax.dot_general(q_h, kbuf[rows, :].astype(BF16), nt, preferred_element_type=F32) * scale
        s = jnp.where(own_kv(SA_COLS), s + gbias_ref[0, q:q + 1, :], NEG_INF)
        sn = lax.dot_general(q_h, k_new, nt, preferred_element_type=F32) * scale
        sn = jnp.where(own_kv(LANES), sn + nbias_ref[0, q:q + 1, :], NEG_INF)
        m = jnp.maximum(jnp.max(s, axis=1, keepdims=True), jnp.max(sn, axis=1, keepdims=True))
        p = jnp.exp(s - m)
        pn = jnp.exp(sn - m)
        l = jnp.sum(p, axis=1, keepdims=True) + jnp.sum(pn, axis=1, keepdims=True)
        o = (jnp.dot(p.astype(BF16), vbuf[rows, :].astype(BF16), preferred_element_type=F32)
             + jnp.dot(pn.astype(BF16), v_new, preferred_element_type=F32)) / l
        o_ref[0, q] = o


def _sample_attention(rows_flat, gbias, nbias, q_s, k_new, v_new, cache_k2d, cache_v2d):
    grid_spec = pltpu.PrefetchScalarGridSpec(
        num_scalar_prefetch=1,
        grid=(DEC_BATCH,),
        in_specs=[pl.BlockSpec((1, DEC_SEQ, SA_COLS), lambda b, r: (b, 0, 0)),
                  pl.BlockSpec((1, DEC_SEQ, LANES), lambda b, r: (b, 0, 0)),
                  pl.BlockSpec((1, DEC_SEQ, N_HEADS, HEAD_DIM), lambda b, r: (b, 0, 0, 0)),
                  pl.BlockSpec((1, LANES, HEAD_DIM), lambda b, r: (b, 0, 0)),
                  pl.BlockSpec((1, LANES, HEAD_DIM), lambda b, r: (b, 0, 0)),
                  pl.BlockSpec(memory_space=pl.ANY),
                  pl.BlockSpec(memory_space=pl.ANY)],
        out_specs=pl.BlockSpec((1, DEC_SEQ, N_HEADS, HEAD_DIM), lambda b, r: (b, 0, 0, 0)),
        scratch_shapes=[pltpu.VMEM((DEC_SEQ * SA_COLS, HEAD_DIM), F32),
                        pltpu.VMEM((DEC_SEQ * SA_COLS, HEAD_DIM), F32),
                        pltpu.SemaphoreType.DMA((2,))])
    return pl.pallas_call(
        _sattn_kernel,
        grid_spec=grid_spec,
        out_shape=jax.ShapeDtypeStruct((DEC_BATCH, DEC_SEQ, N_HEADS, HEAD_DIM), F32),
        compiler_params=_cparams(("arbitrary",), disable_bounds_checks=True),
        name="sample_attention",
    )(rows_flat, gbias, nbias, q_s, k_new, v_new, cache_k2d, cache_v2d)


def _pool_kernel(p_ref, halo_ref, first_ref, cnt_ref, o_ref, ext_scr, *, tiles_per_batch):
    i = pl.program_id(0)
    tm = p_ref.shape[0]
    ext_scr[0:N_META, :] = jnp.where(i % tiles_per_batch == 0, first_ref[...], halo_ref[...])
    ext_scr[N_META:, :] = p_ref[...]
    for g, win in enumerate(POOL_WINDOWS):
        cols = slice(g * POOL_GROUP, (g + 1) * POOL_GROUP)
        x_self = ext_scr[N_META:N_META + tm, cols]
        acc = x_self
        for d in range(1, win):
            acc = acc + ext_scr[N_META - d:N_META - d + tm, cols]
        o_ref[:, cols] = (acc / cnt_ref[:, cols] - x_self).astype(o_ref.dtype)


def _pool(p_src, p_col_block, halo_src, first_src, first_block, cnt, *, rows, tm, tiles_per_batch):
    kern = functools.partial(_pool_kernel, tiles_per_batch=tiles_per_batch)
    halo_per_tile = tm // N_META
    cnt_rows = cnt.shape[0]
    cnt_map = (lambda i: (0, 0)) if cnt_rows == 1 else (lambda i: (i, 0))
    return pl.pallas_call(
        kern,
        grid=(rows // tm,),
        in_specs=[pl.BlockSpec((tm, POOL_WIDTH), lambda i: (i, p_col_block)),
                  pl.BlockSpec((N_META, POOL_WIDTH),
                               lambda i: (jnp.maximum(i * halo_per_tile - 1, 0), p_col_block)),
                  pl.BlockSpec((N_META, POOL_WIDTH), lambda i: first_block),
                  pl.BlockSpec((1 if cnt_rows == 1 else tm, POOL_WIDTH), cnt_map)],
        out_specs=pl.BlockSpec((tm, POOL_WIDTH), lambda i: (i, 0)),
        out_shape=jax.ShapeDtypeStruct((rows, POOL_WIDTH), BF16),
        scratch_shapes=[pltpu.VMEM((tm + N_META, POOL_WIDTH), F32)],
        compiler_params=_cparams(("arbitrary",)),
        name="pool",
    )(p_src, halo_src, first_src, cnt)


def _layer_norm(y, g_ref, b_ref):
    mu = jnp.mean(y, axis=-1, keepdims=True)
    var = jnp.mean(jnp.square(y - mu), axis=-1, keepdims=True)
    return (y - mu) * lax.rsqrt(var + LN_EPS) * g_ref[...] + b_ref[...]


def _mix_kernel(pooled_ref, a_ref, x_ref, wp_ref, ps_ref, wo_ref, g_ref, b_ref, x1_ref, x1b_ref):
    parts = []
    for g in range(len(POOL_WINDOWS)):
        cols = slice(g * POOL_GROUP, (g + 1) * POOL_GROUP)
        parts.append(jnp.dot(pooled_ref[:, cols], wp_ref[g], preferred_element_type=F32))
    m = jnp.concatenate(parts, axis=1) * ps_ref[...]
    mix = (jnp.dot(a_ref[...], wo_ref[:ATTN_WIDTH, :], preferred_element_type=F32)
           + jnp.dot(m.astype(BF16), wo_ref[ATTN_WIDTH:, :], preferred_element_type=F32))
    x1 = _layer_norm(DEEPNORM_ALPHA * x_ref[...] + mix, g_ref, b_ref)
    x1_ref[...] = x1
    x1b_ref[...] = x1.astype(BF16)


def _mix(pooled, a, x, w_pool, pool_scale, w_out, g1, b1, tm):
    rows = x.shape[0]
    row = lambda i: (i, 0)
    const2 = lambda i: (0, 0)
    return pl.pallas_call(
        _mix_kernel,
        grid=(rows // tm,),
        in_specs=[pl.BlockSpec((tm, POOL_WIDTH), row),
                  pl.BlockSpec((tm, ATTN_WIDTH), row),
                  pl.BlockSpec((tm, D_MODEL), row),
                  pl.BlockSpec((len(POOL_WINDOWS), POOL_GROUP, POOL_GROUP), lambda i: (0, 0, 0)),
                  pl.BlockSpec((1, POOL_WIDTH), const2),
                  pl.BlockSpec((D_MODEL, D_MODEL), const2),
                  pl.BlockSpec((1, D_MODEL), const2),
                  pl.BlockSpec((1, D_MODEL), const2)],
        out_specs=[pl.BlockSpec((tm, D_MODEL), row), pl.BlockSpec((tm, D_MODEL), row)],
        out_shape=[jax.ShapeDtypeStruct((rows, D_MODEL), F32), jax.ShapeDtypeStruct((rows, D_MODEL), BF16)],
        compiler_params=_cparams(("arbitrary",)),
        name="mix_ln1",
    )(pooled, a, x, w_pool, pool_scale, w_out, g1, b1)


PA_TQ = 256
PA_GROUP = 1
TF = 512
TF_SUB = 256
N_FF_CHUNKS = D_FF // TF
CONV_HALO = 8


def _silu(x):
    return x * (0.5 * jnp.tanh(0.5 * x) + 0.5)


def _ffn_accumulate(gate, val, wd_ref, x1_ref, g_ref, b_ref, o_ref, acc_ref):
    c = pl.program_id(1)
    h = (_silu(gate) * val).astype(BF16)
    part = jnp.dot(h, wd_ref[...], preferred_element_type=F32)

    @pl.when(c == 0)
    def _():
        acc_ref[...] = part

    @pl.when(c > 0)
    def _():
        acc_ref[...] += part

    @pl.when(c == N_FF_CHUNKS - 1)
    def _():
        o_ref[...] = _layer_norm(DEEPNORM_ALPHA * x1_ref[...] + acc_ref[...], g_ref, b_ref)


def _ffn_fused_kernel(x1_ref, wg_ref, wv_ref, fg_ref, fv_ref, cwg_ref, cwv_ref, cbg_ref, cbv_ref,
                      wd_ref, g_ref, b_ref, o_ref, tg_ref, tv_ref, xb_scr, acc_ref, carry_g, carry_v, ext_scr,
                      *, tiles_per_batch):
    i = pl.program_id(0)
    c = pl.program_id(1)
    tm = x1_ref.shape[0]
    first = i % tiles_per_batch == 0

    @pl.when(c == 0)
    def _():
        xb_scr[...] = x1_ref[...].astype(BF16)
        acc_ref[...] = jnp.zeros(acc_ref.shape, F32)

    @pl.when(i == 0)
    def _():
        carry_g[c] = jnp.zeros((CONV_HALO, TF), F32)
        carry_v[c] = jnp.zeros((CONV_HALO, TF), F32)

    part = None
    for s in range(TF // TF_SUB):
        cols = slice(s * TF_SUB, (s + 1) * TF_SUB)

        def conv(w_ref, f_ref, carry, tail_ref, cw_ref, cb_ref, ext):
            u = jnp.dot(xb_scr[...], w_ref[:, cols], preferred_element_type=F32)
            ext[0:CONV_HALO, :] = jnp.where(first, f_ref[:, cols], carry[c, :, cols])
            ext[CONV_HALO:, :] = u
            tail = u[tm - CONV_HALO:, :]
            carry[c, :, cols] = tail
            tail_ref[0, :, cols] = tail
            return (cb_ref[:, cols] + cw_ref[0:1, cols] * ext[CONV_HALO - 2:CONV_HALO - 2 + tm, :]
                    + cw_ref[1:2, cols] * ext[CONV_HALO - 1:CONV_HALO - 1 + tm, :]
                    + cw_ref[2:3, cols] * u)

        gate = conv(wg_ref, fg_ref, carry_g, tg_ref, cwg_ref, cbg_ref, ext_scr.at[2 * s])
        val = conv(wv_ref, fv_ref, carry_v, tv_ref, cwv_ref, cbv_ref, ext_scr.at[2 * s + 1])
        h = (_silu(gate) * val).astype(BF16)
        d = jnp.dot(h, wd_ref[cols, :], preferred_element_type=F32)
        part = d if part is None else part + d
    acc_ref[...] += part

    @pl.when(c == N_FF_CHUNKS - 1)
    def _():
        o_ref[...] = _layer_norm(DEEPNORM_ALPHA * x1_ref[...] + acc_ref[...], g_ref, b_ref)


def _ffn_prev_kernel(ug_ref, uv_ref, p1g_ref, p1v_ref, p2g_ref, p2v_ref, cwg_ref, cwv_ref, cbg_ref, cbv_ref,
                     wd_ref, x1_ref, g_ref, b_ref, o_ref, acc_ref):
    def conv(u_ref, p1_ref, p2_ref, cw_ref, cb_ref):
        return (cb_ref[...] + cw_ref[0:1, :] * p2_ref[...] + cw_ref[1:2, :] * p1_ref[...]
                + cw_ref[2:3, :] * u_ref[...])

    gate = conv(ug_ref, p1g_ref, p2g_ref, cwg_ref, cbg_ref)
    val = conv(uv_ref, p1v_ref, p2v_ref, cwv_ref, cbv_ref)
    _ffn_accumulate(gate, val, wd_ref, x1_ref, g_ref, b_ref, o_ref, acc_ref)


def _ffn_common_specs(tm):
    gate_c = lambda i, c: (0, c)
    val_c = lambda i, c: (0, N_FF_CHUNKS + c)
    return [pl.BlockSpec((3, TF), gate_c), pl.BlockSpec((3, TF), val_c),
            pl.BlockSpec((1, TF), gate_c), pl.BlockSpec((1, TF), val_c),
            pl.BlockSpec((TF, D_MODEL), lambda i, c: (c, 0)),
            pl.BlockSpec((tm, D_MODEL), lambda i, c: (i, 0)),
            pl.BlockSpec((1, D_MODEL), lambda i, c: (0, 0)),
            pl.BlockSpec((1, D_MODEL), lambda i, c: (0, 0))]


def _ffn_fused(x1, w_up, u_first, first_row_block, conv_w, conv_b, w_down, g2, b2, *, tm, tiles_per_batch):
    rows = x1.shape[0]
    kern = functools.partial(_ffn_fused_kernel, tiles_per_batch=tiles_per_batch)
    gate_c = lambda i, c: (0, c)
    val_c = lambda i, c: (0, N_FF_CHUNKS + c)
    const = lambda i, c: (0, 0)
    tail_spec = pl.BlockSpec((1, CONV_HALO, TF), lambda i, c: (i, 0, c))
    tail_shape = jax.ShapeDtypeStruct((rows // tm, CONV_HALO, D_FF), F32)
    y, tail_g, tail_v = pl.pallas_call(
        kern,
        grid=(rows // tm, N_FF_CHUNKS),
        in_specs=[pl.BlockSpec((tm, D_MODEL), lambda i, c: (i, 0)),
                  pl.BlockSpec((D_MODEL, TF), gate_c), pl.BlockSpec((D_MODEL, TF), val_c),
                  pl.BlockSpec((CONV_HALO, TF), lambda i, c: (first_row_block, c)),
                  pl.BlockSpec((CONV_HALO, TF), lambda i, c: (first_row_block, N_FF_CHUNKS + c)),
                  pl.BlockSpec((3, TF), gate_c), pl.BlockSpec((3, TF), val_c),
                  pl.BlockSpec((1, TF), gate_c), pl.BlockSpec((1, TF), val_c),
                  pl.BlockSpec((TF, D_MODEL), lambda i, c: (c, 0)),
                  pl.BlockSpec((1, D_MODEL), const), pl.BlockSpec((1, D_MODEL), const)],
        out_specs=[pl.BlockSpec((tm, D_MODEL), lambda i, c: (i, 0)), tail_spec, tail_spec],
        out_shape=[jax.ShapeDtypeStruct((rows, D_MODEL), F32), tail_shape, tail_shape],
        scratch_shapes=[pltpu.VMEM((tm, D_MODEL), BF16), pltpu.VMEM((tm, D_MODEL), F32),
                        pltpu.VMEM((N_FF_CHUNKS, CONV_HALO, TF), F32),
                        pltpu.VMEM((N_FF_CHUNKS, CONV_HALO, TF), F32),
                        pltpu.VMEM((2 * (TF // TF_SUB), tm + CONV_HALO, TF_SUB), F32)],
        compiler_params=_cparams(("arbitrary", "arbitrary")),
        name="ffn",
    )(x1, w_up, w_up, u_first, u_first, conv_w, conv_w, conv_b, conv_b, w_down, g2, b2)
    last = slice(tiles_per_batch - 1, None, tiles_per_batch)
    return y, tail_g[last], tail_v[last]


def _ffn_tail_prev(u, prev1, prev2, conv_w, conv_b, w_down, x1, g2, b2, *, tm):
    rows = u.shape[0]
    gate_t = lambda i, c: (i, c)
    val_t = lambda i, c: (i, N_FF_CHUNKS + c)
    return pl.pallas_call(
        _ffn_prev_kernel,
        grid=(rows // tm, N_FF_CHUNKS),
        in_specs=[pl.BlockSpec((tm, TF), gate_t), pl.BlockSpec((tm, TF), val_t),
                  pl.BlockSpec((tm, TF), gate_t), pl.BlockSpec((tm, TF), val_t),
                  pl.BlockSpec((tm, TF), gate_t), pl.BlockSpec((tm, TF), val_t)]
                 + _ffn_common_specs(tm),
        out_specs=pl.BlockSpec((tm, D_MODEL), lambda i, c: (i, 0)),
        out_shape=jax.ShapeDtypeStruct((rows, D_MODEL), F32),
        scratch_shapes=[pltpu.VMEM((tm, D_MODEL), F32)],
        compiler_params=_cparams(("arbitrary", "arbitrary")),
        name="ffn_tail_small",
    )(u, u, prev1, prev1, prev2, prev2, conv_w, conv_w, conv_b, conv_b, w_down, x1, g2, b2)


def kernel(x_prompt, x_sample, cache_k, cache_v, cache_idx_k, state_pool, state_conv, page_table, meta_tokens,
           w_in, w_pool, pool_scale, w_out, ln1_g, ln1_b, w_up, conv_w, conv_b, w_down, ln2_g, ln2_b):
    n_phys = cache_k.shape[1]

    wq, wk, wv, wiq, wik, wiw, wp = jnp.split(w_in[0], IN_OFFSETS, axis=1)
    zcols = lambda n: jnp.zeros((D_MODEL, n), F32)
    w_all = jnp.concatenate([wq, wp, wk, wv, wiq, wik, zcols(64), wiw, zcols(120)], axis=1).astype(BF16)
    w_pool_b = w_pool[0].astype(BF16)
    w_out_b = w_out[0].astype(BF16)
    w_up_b = w_up[0].astype(BF16)
    w_down_b = w_down[0].astype(BF16)
    row2 = lambda a: a.reshape(1, -1)

    pos_prompt = N_META + jnp.arange(SEQ)
    pos_small = jnp.concatenate([jnp.tile(PAST_LEN + jnp.arange(DEC_SEQ), DEC_BATCH), jnp.arange(N_META),
                                 jnp.zeros((SMALL_ROWS - N_SAMPLE_ROWS - N_META,), I32)])
    tabs_p = _rope_tables(pos_prompt, ROT_DIM, HEAD_DIM) + _rope_tables(pos_prompt, IDX_ROT_DIM, IDX_DIM)
    tabs_s = _rope_tables(pos_small, ROT_DIM, HEAD_DIM) + _rope_tables(pos_small, IDX_ROT_DIM, IDX_DIM)

    xp = x_prompt.reshape(N_PROMPT_ROWS, D_MODEL)
    xs = jnp.concatenate([x_sample.reshape(N_SAMPLE_ROWS, D_MODEL), meta_tokens.astype(F32),
                          jnp.zeros((SMALL_ROWS - N_SAMPLE_ROWS - N_META, D_MODEL), F32)], axis=0)

    tm_p = 1024
    hp = _project(xp, w_all, tabs_p, SEQ // tm_p, tm_p)
    hs = _project(xs, w_all, tabs_s, 1, SMALL_ROWS)

    meta = slice(META_ROW0, META_ROW0 + N_META)

    def with_meta(c0, width):
        m = jnp.broadcast_to(hs[meta, c0:c0 + width][None], (BATCH, N_META, width))
        return jnp.concatenate([m, hp[:, c0:c0 + width].reshape(BATCH, SEQ, width)], axis=1)

    k_prompt = with_meta(C_K, KV_WIDTH)
    v_prompt = with_meta(C_V, KV_WIDTH)
    ik_prompt = with_meta(C_IK, IDX_DIM)
    pad_keys = lambda a: jnp.pad(a.astype(BF16), ((0, 0), (0, NK_PROMPT - T_PROMPT), (0, 0)))
    kb, vb, ikb = pad_keys(k_prompt), pad_keys(v_prompt), pad_keys(ik_prompt)

    blocks_per_batch = SEQ // PA_TQ
    a_groups = []
    for j0 in range(0, blocks_per_batch, PA_GROUP):
        nk = N_META + (j0 + PA_GROUP) * PA_TQ
        nk = -(-nk // LANES) * LANES
        a_groups.append(_prompt_attention(hp, kb, vb, ikb, tq=PA_TQ, n_batch=BATCH, n_blocks=PA_GROUP,
                                          row_block0=j0, blocks_per_batch=blocks_per_batch,
                                          pos_first=N_META + j0 * PA_TQ, nk=nk))
    a_p = jnp.concatenate(a_groups, axis=1).reshape(N_PROMPT_ROWS, ATTN_WIDTH)
    a_m = _prompt_attention(hs, kb, vb, ikb, tq=LANES, n_batch=1, n_blocks=1, row_block0=META_ROW0 // LANES,
                            blocks_per_batch=1, pos_first=0, nk=LANES)[0]

    hs_s = hs[:N_SAMPLE_ROWS]
    iq_s = hs_s[:, C_IQ:C_IQ + N_IDX_HEADS * IDX_DIM].reshape(SS_STEPS, SS_PAIR, DEC_SEQ, N_IDX_HEADS, IDX_DIM)
    iq_s = iq_s.transpose(0, 1, 3, 2, 4)
    iq2 = jnp.stack([jnp.pad(iq_s[:, e], ((0, 0), (0, 0), (DEC_SEQ * e, DEC_SEQ * (SS_PAIR - 1 - e)), (0, 0)))
                     for e in range(SS_PAIR)], axis=1).reshape(SS_STEPS, SS_PAIR, N_IDX_HEADS * 8, IDX_DIM)
    w_s = hs_s[:, C_IW:C_IW + N_IDX_HEADS].reshape(SS_STEPS, SS_PAIR * DEC_SEQ, N_IDX_HEADS)
    w2 = w_s.transpose(0, 2, 1).reshape(SS_STEPS, N_IDX_HEADS * 8, 1)
    ik_new_t = jnp.pad(hs_s[:, C_IK:C_IK + IDX_DIM].reshape(DEC_BATCH, DEC_SEQ, IDX_DIM).transpose(0, 2, 1),
                       ((0, 0), (0, 0), (0, LANES - DEC_SEQ))).reshape(SS_STEPS, SS_PAIR, IDX_DIM, LANES)

    pt_pad = jnp.pad(page_table, ((0, 0), (0, N_MROWS - N_PAGES)))
    pt_hi = jnp.broadcast_to((pt_pad // 64).astype(F32)[:, None, :], (DEC_BATCH, 8, N_MROWS))
    pt_lo = jnp.broadcast_to((pt_pad % 64).astype(F32)[:, None, :], (DEC_BATCH, 8, N_MROWS))
    rows_sel, gbias, nbias = _sample_select(page_table, iq2, w2, ik_new_t, pt_hi, pt_lo,
                                            jnp.swapaxes(cache_idx_k[0], 1, 2))

    def new_rows(c0):
        a = hs_s[:, c0:c0 + KV_WIDTH].reshape(DEC_BATCH, DEC_SEQ * N_KV_HEADS, HEAD_DIM)
        return jnp.pad(a, ((0, 0), (0, LANES - DEC_SEQ * N_KV_HEADS), (0, 0)))

    a_s4 = _sample_attention(rows_sel.reshape(-1), gbias.reshape(DEC_BATCH, DEC_SEQ, SA_COLS),
                             nbias.reshape(DEC_BATCH, DEC_SEQ, LANES),
                             hs_s[:, C_Q:C_Q + ATTN_WIDTH].reshape(DEC_BATCH, DEC_SEQ, N_HEADS, HEAD_DIM),
                             new_rows(C_K), new_rows(C_V),
                             cache_k[0].reshape(n_phys * PAGE_SIZE * N_KV_HEADS, HEAD_DIM),
                             cache_v[0].reshape(n_phys * PAGE_SIZE * N_KV_HEADS, HEAD_DIM))
    a_s = a_s4.reshape(N_SAMPLE_ROWS, ATTN_WIDTH).astype(BF16)
    a_small = jnp.concatenate([a_s, a_m[:N_META], jnp.zeros((SMALL_ROWS - N_SAMPLE_ROWS - N_META, ATTN_WIDTH),
                                                            BF16)], axis=0)

    win = jnp.repeat(jnp.asarray(POOL_WINDOWS, F32), POOL_GROUP)
    p_meta = hs[meta, C_P:C_P + POOL_WIDTH]
    tm_pool = 256
    pooled_p = _pool(hp, C_P // POOL_WIDTH, hp, hs, (META_ROW0 // N_META, C_P // POOL_WIDTH), row2(win),
                     rows=N_PROMPT_ROWS, tm=tm_pool, tiles_per_batch=SEQ // tm_pool)
    p_s = hs_s[:, C_P:C_P + POOL_WIDTH].reshape(DEC_BATCH, DEC_SEQ, POOL_WIDTH)
    grp = 24
    ext_s = jnp.concatenate([jnp.zeros((DEC_BATCH, 1, POOL_WIDTH), F32), state_pool[0], p_s,
                             jnp.zeros((DEC_BATCH, grp - 1 - POOL_CTX - DEC_SEQ, POOL_WIDTH), F32)], axis=1)
    ext_small = jnp.concatenate([ext_s.reshape(DEC_BATCH * grp, POOL_WIDTH),
                                 jnp.zeros((N_META, POOL_WIDTH), F32), p_meta], axis=0)
    n_ext = DEC_BATCH * grp + 2 * N_META
    cnt_meta = jnp.minimum(win[None, :], (jnp.arange(N_META, dtype=F32) + 1.0)[:, None])
    cnt_small = jnp.concatenate([jnp.broadcast_to(win[None], (n_ext - N_META, POOL_WIDTH)), cnt_meta], axis=0)
    zeros_halo = jnp.zeros((N_META, POOL_WIDTH), F32)
    pooled_ext = _pool(ext_small, 0, zeros_halo, zeros_halo, (0, 0), cnt_small,
                       rows=n_ext, tm=n_ext, tiles_per_batch=1)
    pooled_small = jnp.concatenate(
        [pooled_ext[:DEC_BATCH * grp].reshape(DEC_BATCH, grp, POOL_WIDTH)[:, 16:16 + DEC_SEQ].reshape(
            N_SAMPLE_ROWS, POOL_WIDTH),
         pooled_ext[n_ext - N_META:],
         jnp.zeros((SMALL_ROWS - N_SAMPLE_ROWS - N_META, POOL_WIDTH), BF16)], axis=0)

    mix_args = (w_pool_b, row2(pool_scale[0]), w_out_b, row2(ln1_g[0]), row2(ln1_b[0]))
    x1_p, x1b_p = _mix(pooled_p, a_p, xp, *mix_args, tm=256)
    x1_s, x1b_s = _mix(pooled_small, a_small, xs, *mix_args, tm=SMALL_ROWS)

    u_s = _matmul(x1b_s, w_up_b, SMALL_ROWS, "ffn_up_small")
    ffn_args = (conv_w[0], row2(conv_b[0]), w_down_b)
    ln2 = (row2(ln2_g[0]), row2(ln2_b[0]))
    tm_f = 512
    y_p, tail_g, tail_v = _ffn_fused(x1_p, w_up_b, u_s, (META_ROW0 + N_META) // CONV_HALO - 1, *ffn_args, *ln2,
                                     tm=tm_f, tiles_per_batch=SEQ // tm_f)
    u_s3 = u_s[:N_SAMPLE_ROWS].reshape(DEC_BATCH, DEC_SEQ, 2 * D_FF)
    ext_u = jnp.concatenate([state_conv[0], u_s3], axis=1)
    u_m = u_s[meta]
    ext_m = jnp.concatenate([jnp.zeros((2, 2 * D_FF), F32), u_m], axis=0)
    tail0 = jnp.zeros((SMALL_ROWS - N_SAMPLE_ROWS - N_META, 2 * D_FF), F32)
    prev1 = jnp.concatenate([ext_u[:, 1:1 + DEC_SEQ].reshape(N_SAMPLE_ROWS, -1), ext_m[1:1 + N_META], tail0], 0)
    prev2 = jnp.concatenate([ext_u[:, 0:DEC_SEQ].reshape(N_SAMPLE_ROWS, -1), ext_m[0:N_META], tail0], 0)
    y_s = _ffn_tail_prev(u_s, prev1, prev2, *ffn_args, x1_s, *ln2, tm=SMALL_ROWS)

    y_prompt = y_p.reshape(BATCH, SEQ, D_MODEL)
    y_sample = y_s[:N_SAMPLE_ROWS].reshape(DEC_BATCH, DEC_SEQ, D_MODEL)
    pool_prompt = hp.reshape(BATCH, SEQ, PROJ_WIDTH)[:, SEQ - POOL_CTX:, C_P:C_P + POOL_WIDTH]
    conv_prompt = jnp.concatenate([tail_g[:, CONV_HALO - 2:], tail_v[:, CONV_HALO - 2:]], axis=-1)
    k_sample = hs_s[:, C_K:C_K + KV_WIDTH].reshape(DEC_BATCH, DEC_SEQ, N_KV_HEADS, HEAD_DIM)
    v_sample = hs_s[:, C_V:C_V + KV_WIDTH].reshape(DEC_BATCH, DEC_SEQ, N_KV_HEADS, HEAD_DIM)
    ik_sample = hs_s[:, C_IK:C_IK + IDX_DIM].reshape(DEC_BATCH, DEC_SEQ, IDX_DIM)
    pool_sample = jnp.concatenate([state_pool[0], p_s], axis=1)[:, DEC_SEQ:]
    conv_sample = ext_u[:, DEC_SEQ:]
    return (y_prompt, y_sample,
            k_prompt.reshape(1, BATCH, T_PROMPT, N_KV_HEADS, HEAD_DIM),
            v_prompt.reshape(1, BATCH, T_PROMPT, N_KV_HEADS, HEAD_DIM),
            ik_prompt[None], pool_prompt[None], conv_prompt[None],
            k_sample[None], v_sample[None], ik_sample[None], pool_sample[None], conv_sample[None])
```

```python
import functools

import numpy as np
import jax
import jax.numpy as jnp
from jax import lax
from jax.experimental import pallas as pl
from jax.experimental.pallas import tpu as pltpu

F32 = jnp.float32
BF16 = jnp.bfloat16
I32 = jnp.int32

D_MODEL = 2048
BATCH = 8
SEQ = 2048
DEC_BATCH = 32
DEC_SEQ = 4
PAST_LEN = 16384
PAGE_SIZE = 128
N_META = 16
ATTN_WIDTH = 1024
N_HEADS = 8
HEAD_DIM = 128
N_KV_HEADS = 4
KV_WIDTH = N_KV_HEADS * HEAD_DIM
ROT_DIM = 32
N_IDX_HEADS = 8
IDX_DIM = 64
IDX_ROT_DIM = 16
TOPK = 256
POOL_WIDTH = 1024
POOL_WINDOWS = (2, 4, 8, 16)
POOL_GROUP = 256
POOL_CTX = 15
D_FF = 5632
ROPE_THETA = 500000.0
LN_EPS = 1e-5
DEEPNORM_ALPHA = 2.0 ** 0.25
IN_OFFSETS = (1024, 1536, 2048, 2560, 2624, 2632)

LANES = 128
N_PROMPT_ROWS = BATCH * SEQ
N_SAMPLE_ROWS = DEC_BATCH * DEC_SEQ
SMALL_ROWS = 256
META_ROW0 = N_SAMPLE_ROWS
T_PROMPT = N_META + SEQ
NK_PROMPT = 2176
NK_SAMPLE = PAST_LEN + LANES
N_PAGES = PAST_LEN // PAGE_SIZE

C_Q, C_P, C_K, C_V, C_IQ, C_IK, C_IW = 0, 1024, 2048, 2560, 3072, 3584, 3712
PROJ_WIDTH = 3840
TN = 256
ROPE_A_TILES = (0, 1, 2, 3, 8, 9)
ROPE_B_TILES = (12, 13)
ROPE_BH_TILE = 14

VMEM_LIMIT = 56 * 1024 * 1024
INT_MIN = -2 ** 31
NEG_INF = float("-inf")
LOG2_E = 1.4426950408889634


def _cparams(sem, **kw):
    return pltpu.CompilerParams(dimension_semantics=sem, vmem_limit_bytes=VMEM_LIMIT, **kw)


def _rope_tables(pos, rot_dim, head_dim):
    half = rot_dim // 2
    inv_freq = ROPE_THETA ** (-np.arange(half, dtype=np.float64) / half)
    ang = np.asarray(pos, np.float64)[:, None] * inv_freq[None, :]
    cos, sin = np.cos(ang), np.sin(ang)
    rows = len(pos)
    zh = np.zeros((rows, half))
    rest0 = np.zeros((rows, head_dim - rot_dim))
    c = np.concatenate([cos, cos, rest0 + 1.0], axis=1)
    s1 = np.concatenate([-sin, zh, rest0], axis=1)
    s2 = np.concatenate([zh, sin, rest0], axis=1)
    reps = LANES // head_dim
    return tuple(jnp.asarray(np.tile(t, (1, reps)), F32) for t in (c, s1, s2))


def _rope(x, c_ref, s1_ref, s2_ref, half):
    return (x * c_ref[...] + pltpu.roll(x, LANES - half, 1) * s1_ref[...]
            + pltpu.roll(x, half, 1) * s2_ref[...])


def _proj_kernel(x_ref, w_ref, ca, sa1, sa2, cb, sb1, sb2, o_ref, xb_ref):
    j = pl.program_id(1)

    @pl.when(j == 0)
    def _():
        xb_ref[...] = x_ref[...].astype(BF16)

    acc = jnp.dot(xb_ref[...], w_ref[...], preferred_element_type=F32)
    is_a = functools.reduce(jnp.logical_or, [j == t for t in ROPE_A_TILES])
    is_b = functools.reduce(jnp.logical_or, [j == t for t in ROPE_B_TILES])
    is_bh = j == ROPE_BH_TILE

    @pl.when(is_a)
    def _():
        o_ref[:, :LANES] = _rope(acc[:, :LANES], ca, sa1, sa2, ROT_DIM // 2)
        o_ref[:, LANES:] = _rope(acc[:, LANES:], ca, sa1, sa2, ROT_DIM // 2)

    @pl.when(is_b)
    def _():
        o_ref[:, :LANES] = _rope(acc[:, :LANES], cb, sb1, sb2, IDX_ROT_DIM // 2)
        o_ref[:, LANES:] = _rope(acc[:, LANES:], cb, sb1, sb2, IDX_ROT_DIM // 2)

    @pl.when(is_bh)
    def _():
        o_ref[:, :LANES] = _rope(acc[:, :LANES], cb, sb1, sb2, IDX_ROT_DIM // 2)
        o_ref[:, LANES:] = acc[:, LANES:]

    @pl.when(jnp.logical_not(is_a | is_b | is_bh))
    def _():
        o_ref[...] = acc


def _project(x, w_all, tabs, tab_blocks, tm):
    rows = x.shape[0]
    tab_spec = pl.BlockSpec((tm, LANES), lambda i, j: (i % tab_blocks, 0))
    return pl.pallas_call(
        _proj_kernel,
        grid=(rows // tm, PROJ_WIDTH // TN),
        in_specs=[pl.BlockSpec((tm, D_MODEL), lambda i, j: (i, 0)),
                  pl.BlockSpec((D_MODEL, TN), lambda i, j: (0, j))] + [tab_spec] * 6,
        out_specs=pl.BlockSpec((tm, TN), lambda i, j: (i, j)),
        out_shape=jax.ShapeDtypeStruct((rows, PROJ_WIDTH), F32),
        scratch_shapes=[pltpu.VMEM((tm, D_MODEL), BF16)],
        compiler_params=_cparams(("arbitrary", "arbitrary")),
        name="in_proj",
    )(x, w_all, *tabs)


def _mm_kernel(x_ref, w_ref, o_ref):
    o_ref[...] = jnp.dot(x_ref[...], w_ref[...], preferred_element_type=F32)


def _matmul(x, w, tm, name):
    rows, kdim = x.shape
    n = w.shape[1]
    return pl.pallas_call(
        _mm_kernel,
        grid=(rows // tm, n // TN),
        in_specs=[pl.BlockSpec((tm, kdim), lambda i, j: (i, 0)),
                  pl.BlockSpec((kdim, TN), lambda i, j: (0, j))],
        out_specs=pl.BlockSpec((tm, TN), lambda i, j: (i, j)),
        out_shape=jax.ShapeDtypeStruct((rows, n), F32),
        compiler_params=_cparams(("arbitrary", "arbitrary")),
        name=name,
    )(x, w)


def _ordinal_to_float(key):
    bits = key ^ ((key >> 31) & 0x7FFFFFFF)
    return lax.bitcast_convert_type(bits, F32)


def _select_topk(sc_ref, col, valid, n_col_bits, cut_scr):
    rows = sc_ref.shape[0]
    k_f = float(TOPK)

    def count(pred):
        return jnp.sum(jnp.where(pred, 1.0, 0.0), axis=1, keepdims=True)

    few = count(valid) <= k_f
    key0 = jnp.where(count(sc_ref[...] >= 0.0) >= k_f, 0, INT_MIN).astype(I32)

    def value_step(i, key):
        cand = key + jnp.left_shift(jnp.int32(1), 30 - i)
        cnt = count(sc_ref[...] >= _ordinal_to_float(cand))
        return jnp.where(cnt >= k_f, cand, key)

    key = lax.fori_loop(0, 31, value_step, key0)
    thr = jnp.where(few, NEG_INF, _ordinal_to_float(key))
    need = k_f - count(sc_ref[...] > thr)
    surplus = jnp.where(few, 0.0, count(sc_ref[...] >= thr) - k_f)

    cut_scr[...] = jnp.full(cut_scr.shape, 2 ** n_col_bits - 1, I32)

    @pl.when(jnp.max(surplus) > 0.0)
    def _():
        def col_step(i, cut):
            cand = cut + jnp.left_shift(jnp.int32(1), n_col_bits - 1 - i)
            cnt = count((sc_ref[...] == thr) & (col < cand))
            return jnp.where(cnt < need, cand, cut)

        cut_scr[...] = lax.fori_loop(0, n_col_bits, col_step, jnp.zeros((rows, 1), I32))

    sc = sc_ref[...]
    return valid & ((sc > thr) | ((sc == thr) & (col <= cut_scr[...])))


def _select_topk_tiled(sc_ref, valid_fn, n_col_bits, cut_scr, out_ref):
    rows, n = sc_ref.shape
    k_f = float(TOPK)
    lane = lax.broadcasted_iota(I32, (rows, LANES), 1)
    tiles = [(t * LANES, slice(t * LANES, (t + 1) * LANES)) for t in range(n // LANES)]
    wide = lambda v: jnp.broadcast_to(v, (rows, LANES))

    def count(pred_fn):
        acc = jnp.zeros((rows, LANES), F32)
        for c0, cols in tiles:
            acc = acc + jnp.where(pred_fn(sc_ref[:, cols], lane + c0), 1.0, 0.0)
        return jnp.sum(acc, axis=1, keepdims=True)

    few = count(lambda sc, col: valid_fn(col)) <= k_f
    key0 = jnp.where(count(lambda sc, col: sc >= 0.0) >= k_f, 0, INT_MIN).astype(I32)

    def value_step(i, key):
        cand = wide(_ordinal_to_float(key + jnp.left_shift(jnp.int32(1), 30 - i)))
        cnt = count(lambda sc, col: sc >= cand)
        return jnp.where(cnt >= k_f, key + jnp.left_shift(jnp.int32(1), 30 - i), key)

    key = lax.fori_loop(0, 31, value_step, key0)
    thr = jnp.where(few, NEG_INF, _ordinal_to_float(key))
    thr_w = wide(thr)
    need = k_f - count(lambda sc, col: sc > thr_w)
    surplus = jnp.where(few, 0.0, count(lambda sc, col: sc >= thr_w) - k_f)

    cut_scr[...] = jnp.full(cut_scr.shape, 2 ** n_col_bits - 1, I32)

    @pl.when(jnp.max(surplus) > 0.0)
    def _():
        def col_step(i, cut):
            cand = cut + jnp.left_shift(jnp.int32(1), n_col_bits - 1 - i)
            cand_w = wide(cand)
            cnt = count(lambda sc, col: (sc == thr_w) & (col < cand_w))
            return jnp.where(cnt < need, cand, cut)

        cut_scr[...] = lax.fori_loop(0, n_col_bits, col_step, jnp.zeros((rows, 1), I32))

    cut_w = wide(cut_scr[...])
    for c0, cols in tiles:
        sc, col = sc_ref[:, cols], lane + c0
        keep = valid_fn(col) & ((sc > thr_w) | ((sc == thr_w) & (col <= cut_w)))
        out_ref[:, cols] = jnp.where(keep, 1.0, 0.0)


def _pattn_kernel(q_ref, iq_ref, misc_ref, kb_ref, vb_ref, ikb_ref, o_ref, sc_scr, bias_scr, cut_scr,
                  *, pos_first, nk):
    tq = q_ref.shape[0]
    pos0 = pos_first + tq * pl.program_id(1)

    ik = ikb_ref[0]
    w = misc_ref[:, LANES:LANES + N_IDX_HEADS] * (N_IDX_HEADS ** -0.5)
    score = jnp.zeros((tq, nk), F32)
    for h in range(N_IDX_HEADS):
        iq_h = (iq_ref[:, h * IDX_DIM:(h + 1) * IDX_DIM] * (IDX_DIM ** -0.5)).astype(BF16)
        s_h = lax.dot_general(iq_h, ik, (((1,), (1,)), ((), ())), preferred_element_type=F32)
        score = score + jnp.maximum(s_h, 0.0) * w[:, h:h + 1]

    qpos = pos0 + lax.broadcasted_iota(I32, (tq, nk), 0)
    col = lax.broadcasted_iota(I32, (tq, nk), 1)
    causal = col <= qpos
    sc_scr[...] = jnp.where(causal, score, NEG_INF)
    sel = _select_topk(sc_scr, col, causal, 12, cut_scr)
    bias_scr[...] = jnp.where(sel, 0.0, NEG_INF)

    q_scale = (HEAD_DIM ** -0.5) * LOG2_E
    for h in range(N_HEADS):
        kv = h // (N_HEADS // N_KV_HEADS)
        q_h = (q_ref[:, h * HEAD_DIM:(h + 1) * HEAD_DIM] * q_scale).astype(BF16)
        k_g = kb_ref[0, :, kv * HEAD_DIM:(kv + 1) * HEAD_DIM]
        s = lax.dot_general(q_h, k_g, (((1,), (1,)), ((), ())), preferred_element_type=F32)
        s = s + bias_scr[...]
        m = jnp.max(s, axis=1, keepdims=True)
        p = jnp.exp2(s - m)
        l = jnp.sum(p, axis=1, keepdims=True)
        v_g = vb_ref[0, :, kv * HEAD_DIM:(kv + 1) * HEAD_DIM]
        o = jnp.dot(p.astype(BF16), v_g, preferred_element_type=F32) / l
        o_ref[0, :, h * HEAD_DIM:(h + 1) * HEAD_DIM] = o.astype(o_ref.dtype)


def _prompt_attention(h, kb, vb, ikb, *, tq, n_batch, n_blocks, row_block0, blocks_per_batch, pos_first, nk):
    kern = functools.partial(_pattn_kernel, pos_first=pos_first, nk=nk)
    qrow = lambda b, j: row_block0 + b * blocks_per_batch + j
    return pl.pallas_call(
        kern,
        grid=(n_batch, n_blocks),
        in_specs=[pl.BlockSpec((tq, ATTN_WIDTH), lambda b, j: (qrow(b, j), C_Q // ATTN_WIDTH)),
                  pl.BlockSpec((tq, 512), lambda b, j: (qrow(b, j), C_IQ // 512)),
                  pl.BlockSpec((tq, 256), lambda b, j: (qrow(b, j), C_IK // 256)),
                  pl.BlockSpec((1, nk, KV_WIDTH), lambda b, j: (b, 0, 0)),
                  pl.BlockSpec((1, nk, KV_WIDTH), lambda b, j: (b, 0, 0)),
                  pl.BlockSpec((1, nk, IDX_DIM), lambda b, j: (b, 0, 0))],
        out_specs=pl.BlockSpec((1, tq, ATTN_WIDTH), lambda b, j: (b, j, 0)),
        out_shape=jax.ShapeDtypeStruct((n_batch, n_blocks * tq, ATTN_WIDTH), BF16),
        scratch_shapes=[pltpu.VMEM((tq, nk), F32), pltpu.VMEM((tq, nk), F32),
                        pltpu.VMEM((tq, 1), I32)],
        compiler_params=_cparams(("arbitrary", "arbitrary")),
        name="prompt_attention",
    )(h, h, h, kb, vb, ikb)


KV_SHIFT = 2
N_MROWS = 256


SS_PAIR = 2
SS_STEPS = DEC_BATCH // SS_PAIR
SS_CHUNK = 2048


def _ssel_kernel(pt_ref, iq_ref, w_ref, iknew_ref, pthi_ref, ptlo_ref, cache_ref,
                 rows_ref, gbias_ref, nbias_ref, ikbuf, sem, sc_scr, sel_scr, m_scr, cut_scr):
    s = pl.program_id(0)

    def page_copy(step, slot, e, p):
        return pltpu.make_async_copy(cache_ref.at[pt_ref[SS_PAIR * step + e, p]],
                                     ikbuf.at[slot, e, :, pl.ds(pl.multiple_of(p * PAGE_SIZE, PAGE_SIZE), PAGE_SIZE)],
                                     sem.at[slot])

    def fetch(step, slot):
        def issue(p, carry):
            for e in range(SS_PAIR):
                page_copy(step, slot, e, p).start()
            return carry
        lax.fori_loop(0, N_PAGES, issue, 0)

    slot = s % 2

    @pl.when(s == 0)
    def _():
        fetch(0, 0)

    @pl.when(s + 1 < SS_STEPS)
    def _():
        fetch(s + 1, 1 - slot)

    pltpu.make_async_copy(ikbuf.at[slot], ikbuf.at[slot], sem.at[slot]).wait()

    w = w_ref[0] * (N_IDX_HEADS ** -0.5)
    iq = [iq_ref[0, e].astype(BF16) for e in range(SS_PAIR)]

    def score_of(keys_of):
        r = None
        for e in range(SS_PAIR):
            s_e = jnp.dot(iq[e], keys_of(e).astype(BF16), preferred_element_type=F32)
            r_e = jnp.maximum(s_e * (IDX_DIM ** -0.5), 0.0)
            r = r_e if r is None else r + r_e
        r = r * w
        acc = r[0:8]
        for h in range(1, N_IDX_HEADS):
            acc = acc + r[8 * h:8 * h + 8]
        return acc

    row0 = pl.multiple_of(s * 8, 8)
    for c in range(PAST_LEN // SS_CHUNK):
        cols = slice(c * SS_CHUNK, (c + 1) * SS_CHUNK)
        sc_scr[pl.ds(row0, 8), cols] = score_of(lambda e: ikbuf[slot, e, :, cols])
    sc_scr[pl.ds(row0, 8), PAST_LEN:] = score_of(lambda e: iknew_ref[0, e])

    @pl.when(s == SS_STEPS - 1)
    def _():
        _ssel_finish(pthi_ref, ptlo_ref, rows_ref, gbias_ref, nbias_ref, sc_scr, sel_scr, m_scr, cut_scr)


def _ssel_finish(pthi_ref, ptlo_ref, rows_ref, gbias_ref, nbias_ref, sc_scr, sel_scr, m_scr, cut_scr):
    n_q = sc_scr.shape[0]
    qpos = PAST_LEN + (lax.broadcasted_iota(I32, (n_q, LANES), 0) & (DEC_SEQ - 1))
    valid_fn = lambda col: col <= qpos
    lane = lax.broadcasted_iota(I32, (n_q, LANES), 1)
    for t in range(NK_SAMPLE // LANES):
        cols = slice(t * LANES, (t + 1) * LANES)
        sc_scr[:, cols] = jnp.where(valid_fn(lane + t * LANES), sc_scr[:, cols], NEG_INF)
    _select_topk_tiled(sc_scr, valid_fn, 15, cut_scr, sel_scr)

    nt = (((1,), (1,)), ((), ()))
    ones_b = jnp.ones((8, LANES), BF16)
    upper_pages = (lax.broadcasted_iota(I32, (N_MROWS, N_MROWS), 0)
                   < lax.broadcasted_iota(I32, (N_MROWS, N_MROWS), 1)).astype(BF16)
    upper_lanes = (lax.broadcasted_iota(I32, (LANES, LANES), 0)
                   < lax.broadcasted_iota(I32, (LANES, LANES), 1)).astype(BF16)
    lane_id = lax.broadcasted_iota(I32, (8, LANES), 1).astype(BF16)
    page_id = lax.broadcasted_iota(I32, (8, N_MROWS), 1).astype(BF16)
    jcol = lax.broadcasted_iota(I32, (TOPK, 1), 0).astype(F32)
    spread_picks = (lax.broadcasted_iota(I32, (TOPK, N_KV_HEADS * TOPK), 0)
                    == lax.broadcasted_iota(I32, (TOPK, N_KV_HEADS * TOPK), 1) >> KV_SHIFT).astype(BF16)
    spread_new = (lax.broadcasted_iota(I32, (LANES, LANES), 0)
                  == lax.broadcasted_iota(I32, (LANES, LANES), 1) >> KV_SHIFT).astype(BF16)
    m_scr[...] = jnp.zeros(m_scr.shape, F32)

    def compact(k, batch, new_sel):
        pt_hi = pthi_ref[batch].astype(BF16)
        pt_lo = ptlo_ref[batch].astype(BF16)
        m_b = m_scr[k].astype(BF16)
        cnt = lax.dot_general(ones_b, m_b, nt, preferred_element_type=F32)
        start = jnp.dot(cnt.astype(BF16), upper_pages, preferred_element_type=F32)
        cnt1, start1 = cnt[0:1], start[0:1]
        in_page = (start1 <= jcol) & (jcol < start1 + cnt1)
        a_b = jnp.where(in_page, 1.0, 0.0).astype(BF16)
        m_j = jnp.dot(a_b, m_b, preferred_element_type=F32)
        rank = jnp.dot(m_j.astype(BF16), upper_lanes, preferred_element_type=F32)
        start_j = jnp.sum(jnp.where(in_page, start1, 0.0), axis=1, keepdims=True)
        onehot = jnp.where((m_j > 0.5) & (rank == jcol - start_j), 1.0, 0.0).astype(BF16)
        off = lax.dot_general(lane_id, onehot, nt, preferred_element_type=F32)[0:1]
        page = lax.dot_general(page_id, a_b, nt, preferred_element_type=F32)[0:1]
        phys = (lax.dot_general(pt_hi, a_b, nt, preferred_element_type=F32)[0:1] * 64.0
                + lax.dot_general(pt_lo, a_b, nt, preferred_element_type=F32)[0:1])
        in_past = page < float(N_PAGES)
        row = jnp.where(in_past, phys * float(PAGE_SIZE) + off, 0.0).astype(I32)
        past8 = jnp.broadcast_to(jnp.where(in_past, 1.0, 0.0), (8, TOPK)).astype(BF16)
        past4 = jnp.dot(past8, spread_picks, preferred_element_type=F32)[0:1]
        new8 = jnp.broadcast_to(new_sel, (8, LANES)).astype(BF16)
        new4 = jnp.dot(new8, spread_new, preferred_element_type=F32)[0:1]
        return row, jnp.where(past4 > 0.5, 0.0, NEG_INF), jnp.where(new4 > 0.5, 0.0, NEG_INF)

    def compact_group(g, carry):
        r8 = pl.multiple_of(g * 8, 8)
        for c in range(N_PAGES + 1):
            tile = sel_scr[pl.ds(r8, 8), c * LANES:(c + 1) * LANES]
            for k in range(8):
                m_scr[k, c:c + 1, :] = tile[k:k + 1, :]
        new_tile = sel_scr[pl.ds(r8, 8), PAST_LEN:]
        rowid = lax.broadcasted_iota(I32, (8, 1), 0)
        rows_t = jnp.zeros((8, TOPK), I32)
        gb_t = jnp.zeros((8, N_KV_HEADS * TOPK), F32)
        nb_t = jnp.zeros((8, LANES), F32)
        for k in range(8):
            row, gb, nb = compact(k, g * (8 // DEC_SEQ) + k // DEC_SEQ, new_tile[k:k + 1, :])
            rows_t = jnp.where(rowid == k, row, rows_t)
            gb_t = jnp.where(rowid == k, gb, gb_t)
            nb_t = jnp.where(rowid == k, nb, nb_t)
        rows_ref[pl.ds(r8, 8), :] = rows_t
        gbias_ref[pl.ds(r8, 8), :] = gb_t
        nbias_ref[pl.ds(r8, 8), :] = nb_t
        return carry

    lax.fori_loop(0, n_q // 8, compact_group, 0)


def _sample_select(page_table, iq2, w2, ik_new_t, pt_hi, pt_lo, cache_idx_t):
    n_q = N_SAMPLE_ROWS
    whole = lambda shape: pl.BlockSpec(shape, lambda s, pt: (0,) * len(shape))
    grid_spec = pltpu.PrefetchScalarGridSpec(
        num_scalar_prefetch=1,
        grid=(SS_STEPS,),
        in_specs=[pl.BlockSpec((1, SS_PAIR, 64, IDX_DIM), lambda s, pt: (s, 0, 0, 0)),
                  pl.BlockSpec((1, 64, 1), lambda s, pt: (s, 0, 0)),
                  pl.BlockSpec((1, SS_PAIR, IDX_DIM, LANES), lambda s, pt: (s, 0, 0, 0)),
                  whole((DEC_BATCH, 8, N_MROWS)),
                  whole((DEC_BATCH, 8, N_MROWS)),
                  pl.BlockSpec(memory_space=pl.ANY)],
        out_specs=[whole((n_q, TOPK)), whole((n_q, N_KV_HEADS * TOPK)), whole((n_q, LANES))],
        scratch_shapes=[pltpu.VMEM((2, SS_PAIR, IDX_DIM, PAST_LEN), F32),
                        pltpu.SemaphoreType.DMA((2,)),
                        pltpu.VMEM((n_q, NK_SAMPLE), F32),
                        pltpu.VMEM((n_q, NK_SAMPLE), F32),
                        pltpu.VMEM((8, N_MROWS, LANES), F32),
                        pltpu.VMEM((n_q, 1), I32)])
    return pl.pallas_call(
        _ssel_kernel,
        grid_spec=grid_spec,
        out_shape=[jax.ShapeDtypeStruct((n_q, TOPK), I32),
                   jax.ShapeDtypeStruct((n_q, N_KV_HEADS * TOPK), F32),
                   jax.ShapeDtypeStruct((n_q, LANES), F32)],
        compiler_params=_cparams(("arbitrary",)),
        name="sample_select",
    )(page_table, iq2, w2, ik_new_t, pt_hi, pt_lo, cache_idx_t)


SA_COLS = N_KV_HEADS * TOPK


def _sattn_kernel(rows_ref, gbias_ref, nbias_ref, q_ref, knew_ref, vnew_ref, ck_ref, cv_ref, o_ref,
                  kbuf, vbuf, sem):
    b = pl.program_id(0)
    n_slabs = DEC_SEQ * TOPK

    def issue(t, carry):
        row = pl.multiple_of(rows_ref[b * n_slabs + t] * N_KV_HEADS, N_KV_HEADS)
        dst = pl.ds(pl.multiple_of(t * N_KV_HEADS, N_KV_HEADS), N_KV_HEADS)
        pltpu.make_async_copy(ck_ref.at[pl.ds(row, N_KV_HEADS)], kbuf.at[dst], sem.at[0]).start()
        pltpu.make_async_copy(cv_ref.at[pl.ds(row, N_KV_HEADS)], vbuf.at[dst], sem.at[1]).start()
        return carry

    lax.fori_loop(0, n_slabs, issue, 0, unroll=8)
    pltpu.make_async_copy(kbuf, kbuf, sem.at[0]).wait()
    pltpu.make_async_copy(vbuf, vbuf, sem.at[1]).wait()

    nt = (((1,), (1,)), ((), ()))
    scale = HEAD_DIM ** -0.5
    k_new = knew_ref[0].astype(BF16)
    v_new = vnew_ref[0].astype(BF16)
    heads_per_kv = N_HEADS // N_KV_HEADS

    def own_kv(width):
        head = lax.broadcasted_iota(I32, (N_HEADS, width), 0)
        col = lax.broadcasted_iota(I32, (N_HEADS, width), 1)
        return (col & (N_KV_HEADS - 1)) == (head >> (heads_per_kv.bit_length() - 1))

    for q in range(DEC_SEQ):
        q_h = q_ref[0, q].astype(BF16)
        rows = slice(q * SA_COLS, (q + 1) * SA_COLS)
        s = lax.dot_general(q_h, kbuf[rows, :].astype(BF16), nt, preferred_element_type=F32) * scale
        s = jnp.where(own_kv(SA_COLS), s + gbias_ref[0, q:q + 1, :], NEG_INF)
        sn = lax.dot_general(q_h, k_new, nt, preferred_element_type=F32) * scale
        sn = jnp.where(own_kv(LANES), sn + nbias_ref[0, q:q + 1, :], NEG_INF)
        m = jnp.maximum(jnp.max(s, axis=1, keepdims=True), jnp.max(sn, axis=1, keepdims=True))
        p = jnp.exp(s - m)
        pn = jnp.exp(sn - m)
        l = jnp.sum(p, axis=1, keepdims=True) + jnp.sum(pn, axis=1, keepdims=True)
        o = (jnp.dot(p.astype(BF16), vbuf[rows, :].astype(BF16), preferred_element_type=F32)
             + jnp.dot(pn.astype(BF16), v_new, preferred_element_type=F32)) / l
        o_ref[0, q] = o


def _sample_attention(rows_flat, gbias, nbias, q_s, k_new, v_new, cache_k2d, cache_v2d):
    grid_spec = pltpu.PrefetchScalarGridSpec(
        num_scalar_prefetch=1,
        grid=(DEC_BATCH,),
        in_specs=[pl.BlockSpec((1, DEC_SEQ, SA_COLS), lambda b, r: (b, 0, 0)),
                  pl.BlockSpec((1, DEC_SEQ, LANES), lambda b, r: (b, 0, 0)),
                  pl.BlockSpec((1, DEC_SEQ, N_HEADS, HEAD_DIM), lambda b, r: (b, 0, 0, 0)),
                  pl.BlockSpec((1, LANES, HEAD_DIM), lambda b, r: (b, 0, 0)),
                  pl.BlockSpec((1, LANES, HEAD_DIM), lambda b, r: (b, 0, 0)),
                  pl.BlockSpec(memory_space=pl.ANY),
                  pl.BlockSpec(memory_space=pl.ANY)],
        out_specs=pl.BlockSpec((1, DEC_SEQ, N_HEADS, HEAD_DIM), lambda b, r: (b, 0, 0, 0)),
        scratch_shapes=[pltpu.VMEM((DEC_SEQ * SA_COLS, HEAD_DIM), F32),
                        pltpu.VMEM((DEC_SEQ * SA_COLS, HEAD_DIM), F32),
                        pltpu.SemaphoreType.DMA((2,))])
    return pl.pallas_call(
        _sattn_kernel,
        grid_spec=grid_spec,
        out_shape=jax.ShapeDtypeStruct((DEC_BATCH, DEC_SEQ, N_HEADS, HEAD_DIM), F32),
        compiler_params=_cparams(("arbitrary",), disable_bounds_checks=True),
        name="sample_attention",
    )(rows_flat, gbias, nbias, q_s, k_new, v_new, cache_k2d, cache_v2d)


def _pool_kernel(p_ref, halo_ref, first_ref, cnt_ref, o_ref, ext_scr, *, tiles_per_batch):
    i = pl.program_id(0)
    tm = p_ref.shape[0]
    ext_scr[0:N_META, :] = jnp.where(i % tiles_per_batch == 0, first_ref[...], halo_ref[...])
    ext_scr[N_META:, :] = p_ref[...]
    for g, win in enumerate(POOL_WINDOWS):
        cols = slice(g * POOL_GROUP, (g + 1) * POOL_GROUP)
        x_self = ext_scr[N_META:N_META + tm, cols]
        acc = x_self
        for d in range(1, win):
            acc = acc + ext_scr[N_META - d:N_META - d + tm, cols]
        o_ref[:, cols] = (acc / cnt_ref[:, cols] - x_self).astype(o_ref.dtype)


def _pool(p_src, p_col_block, halo_src, first_src, first_block, cnt, *, rows, tm, tiles_per_batch):
    kern = functools.partial(_pool_kernel, tiles_per_batch=tiles_per_batch)
    halo_per_tile = tm // N_META
    cnt_rows = cnt.shape[0]
    cnt_map = (lambda i: (0, 0)) if cnt_rows == 1 else (lambda i: (i, 0))
    return pl.pallas_call(
        kern,
        grid=(rows // tm,),
        in_specs=[pl.BlockSpec((tm, POOL_WIDTH), lambda i: (i, p_col_block)),
                  pl.BlockSpec((N_META, POOL_WIDTH),
                               lambda i: (jnp.maximum(i * halo_per_tile - 1, 0), p_col_block)),
                  pl.BlockSpec((N_META, POOL_WIDTH), lambda i: first_block),
                  pl.BlockSpec((1 if cnt_rows == 1 else tm, POOL_WIDTH), cnt_map)],
        out_specs=pl.BlockSpec((tm, POOL_WIDTH), lambda i: (i, 0)),
        out_shape=jax.ShapeDtypeStruct((rows, POOL_WIDTH), BF16),
        scratch_shapes=[pltpu.VMEM((tm + N_META, POOL_WIDTH), F32)],
        compiler_params=_cparams(("arbitrary",)),
        name="pool",
    )(p_src, halo_src, first_src, cnt)


def _layer_norm(y, g_ref, b_ref):
    mu = jnp.mean(y, axis=-1, keepdims=True)
    var = jnp.mean(jnp.square(y - mu), axis=-1, keepdims=True)
    return (y - mu) * lax.rsqrt(var + LN_EPS) * g_ref[...] + b_ref[...]


def _mix_kernel(pooled_ref, a_ref, x_ref, wp_ref, ps_ref, wo_ref, g_ref, b_ref, x1_ref):
    parts = []
    for g in range(len(POOL_WINDOWS)):
        cols = slice(g * POOL_GROUP, (g + 1) * POOL_GROUP)
        parts.append(jnp.dot(pooled_ref[:, cols], wp_ref[g], preferred_element_type=F32))
    m = jnp.concatenate(parts, axis=1) * ps_ref[...]
    mix = (jnp.dot(a_ref[...], wo_ref[:ATTN_WIDTH, :], preferred_element_type=F32)
           + jnp.dot(m.astype(BF16), wo_ref[ATTN_WIDTH:, :], preferred_element_type=F32))
    x1_ref[...] = _layer_norm(DEEPNORM_ALPHA * x_ref[...] + mix, g_ref, b_ref)


def _mix(pooled, a, x, w_pool, pool_scale, w_out, g1, b1, tm):
    rows = x.shape[0]
    row = lambda i: (i, 0)
    const2 = lambda i: (0, 0)
    return pl.pallas_call(
        _mix_kernel,
        grid=(rows // tm,),
        in_specs=[pl.BlockSpec((tm, POOL_WIDTH), row),
                  pl.BlockSpec((tm, ATTN_WIDTH), row),
                  pl.BlockSpec((tm, D_MODEL), row),
                  pl.BlockSpec((len(POOL_WINDOWS), POOL_GROUP, POOL_GROUP), lambda i: (0, 0, 0)),
                  pl.BlockSpec((1, POOL_WIDTH), const2),
                  pl.BlockSpec((D_MODEL, D_MODEL), const2),
                  pl.BlockSpec((1, D_MODEL), const2),
                  pl.BlockSpec((1, D_MODEL), const2)],
        out_specs=pl.BlockSpec((tm, D_MODEL), row),
        out_shape=jax.ShapeDtypeStruct((rows, D_MODEL), F32),
        compiler_params=_cparams(("arbitrary",)),
        name="mix_ln1",
    )(pooled, a, x, w_pool, pool_scale, w_out, g1, b1)


PA_TQ = 256
PA_GROUP = 1
TF = 512
TF_SUB = 256
N_FF_CHUNKS = D_FF // TF
CONV_HALO = 8


def _silu(x):
    return x * (0.5 * jnp.tanh(0.5 * x) + 0.5)


def _ffn_accumulate(gate, val, wd_ref, x1_ref, g_ref, b_ref, o_ref, acc_ref):
    c = pl.program_id(1)
    h = (_silu(gate) * val).astype(BF16)
    part = jnp.dot(h, wd_ref[...], preferred_element_type=F32)

    @pl.when(c == 0)
    def _():
        acc_ref[...] = part

    @pl.when(c > 0)
    def _():
        acc_ref[...] += part

    @pl.when(c == N_FF_CHUNKS - 1)
    def _():
        o_ref[...] = _layer_norm(DEEPNORM_ALPHA * x1_ref[...] + acc_ref[...], g_ref, b_ref)


def _ffn_fused_kernel(x1_ref, wg_ref, wv_ref, fg_ref, fv_ref, cwg_ref, cwv_ref, cbg_ref, cbv_ref,
                      wd_ref, g_ref, b_ref, o_ref, tg_ref, tv_ref, xb_scr, acc_ref, carry_g, carry_v, ext_scr,
                      *, tiles_per_batch):
    i = pl.program_id(0)
    c = pl.program_id(1)
    tm = x1_ref.shape[0]
    first = i % tiles_per_batch == 0

    @pl.when(c == 0)
    def _():
        xb_scr[...] = x1_ref[...].astype(BF16)
        acc_ref[...] = jnp.zeros(acc_ref.shape, F32)

    @pl.when(i == 0)
    def _():
        carry_g[c] = jnp.zeros((CONV_HALO, TF), F32)
        carry_v[c] = jnp.zeros((CONV_HALO, TF), F32)

    part = None
    for s in range(TF // TF_SUB):
        cols = slice(s * TF_SUB, (s + 1) * TF_SUB)

        def conv(w_ref, f_ref, carry, tail_ref, cw_ref, cb_ref, ext):
            u = jnp.dot(xb_scr[...], w_ref[:, cols], preferred_element_type=F32)
            ext[0:CONV_HALO, :] = jnp.where(first, f_ref[:, cols], carry[c, :, cols])
            ext[CONV_HALO:, :] = u
            tail = u[tm - CONV_HALO:, :]
            carry[c, :, cols] = tail
            tail_ref[0, :, cols] = tail
            return (cb_ref[:, cols] + cw_ref[0:1, cols] * ext[CONV_HALO - 2:CONV_HALO - 2 + tm, :]
                    + cw_ref[1:2, cols] * ext[CONV_HALO - 1:CONV_HALO - 1 + tm, :]
                    + cw_ref[2:3, cols] * u)

        gate = conv(wg_ref, fg_ref, carry_g, tg_ref, cwg_ref, cbg_ref, ext_scr.at[2 * s])
        val = conv(wv_ref, fv_ref, carry_v, tv_ref, cwv_ref, cbv_ref, ext_scr.at[2 * s + 1])
        h = (_silu(gate) * val).astype(BF16)
        d = jnp.dot(h, wd_ref[cols, :], preferred_element_type=F32)
        part = d if part is None else part + d
    acc_ref[...] += part

    @pl.when(c == N_FF_CHUNKS - 1)
    def _():
        o_ref[...] = _layer_norm(DEEPNORM_ALPHA * x1_ref[...] + acc_ref[...], g_ref, b_ref)


def _ffn_prev_kernel(ug_ref, uv_ref, p1g_ref, p1v_ref, p2g_ref, p2v_ref, cwg_ref, cwv_ref, cbg_ref, cbv_ref,
                     wd_ref, x1_ref, g_ref, b_ref, o_ref, acc_ref):
    def conv(u_ref, p1_ref, p2_ref, cw_ref, cb_ref):
        return (cb_ref[...] + cw_ref[0:1, :] * p2_ref[...] + cw_ref[1:2, :] * p1_ref[...]
                + cw_ref[2:3, :] * u_ref[...])

    gate = conv(ug_ref, p1g_ref, p2g_ref, cwg_ref, cbg_ref)
    val = conv(uv_ref, p1v_ref, p2v_ref, cwv_ref, cbv_ref)
    _ffn_accumulate(gate, val, wd_ref, x1_ref, g_ref, b_ref, o_ref, acc_ref)


def _ffn_common_specs(tm):
    gate_c = lambda i, c: (0, c)
    val_c = lambda i, c: (0, N_FF_CHUNKS + c)
    return [pl.BlockSpec((3, TF), gate_c), pl.BlockSpec((3, TF), val_c),
            pl.BlockSpec((1, TF), gate_c), pl.BlockSpec((1, TF), val_c),
            pl.BlockSpec((TF, D_MODEL), lambda i, c: (c, 0)),
            pl.BlockSpec((tm, D_MODEL), lambda i, c: (i, 0)),
            pl.BlockSpec((1, D_MODEL), lambda i, c: (0, 0)),
            pl.BlockSpec((1, D_MODEL), lambda i, c: (0, 0))]


def _ffn_fused(x1, w_up, u_first, first_row_block, conv_w, conv_b, w_down, g2, b2, *, tm, tiles_per_batch):
    rows = x1.shape[0]
    kern = functools.partial(_ffn_fused_kernel, tiles_per_batch=tiles_per_batch)
    gate_c = lambda i, c: (0, c)
    val_c = lambda i, c: (0, N_FF_CHUNKS + c)
    const = lambda i, c: (0, 0)
    tail_spec = pl.BlockSpec((1, CONV_HALO, TF), lambda i, c: (i, 0, c))
    tail_shape = jax.ShapeDtypeStruct((rows // tm, CONV_HALO, D_FF), F32)
    y, tail_g, tail_v = pl.pallas_call(
        kern,
        grid=(rows // tm, N_FF_CHUNKS),
        in_specs=[pl.BlockSpec((tm, D_MODEL), lambda i, c: (i, 0)),
                  pl.BlockSpec((D_MODEL, TF), gate_c), pl.BlockSpec((D_MODEL, TF), val_c),
                  pl.BlockSpec((CONV_HALO, TF), lambda i, c: (first_row_block, c)),
                  pl.BlockSpec((CONV_HALO, TF), lambda i, c: (first_row_block, N_FF_CHUNKS + c)),
                  pl.BlockSpec((3, TF), gate_c), pl.BlockSpec((3, TF), val_c),
                  pl.BlockSpec((1, TF), gate_c), pl.BlockSpec((1, TF), val_c),
                  pl.BlockSpec((TF, D_MODEL), lambda i, c: (c, 0)),
                  pl.BlockSpec((1, D_MODEL), const), pl.BlockSpec((1, D_MODEL), const)],
        out_specs=[pl.BlockSpec((tm, D_MODEL), lambda i, c: (i, 0)), tail_spec, tail_spec],
        out_shape=[jax.ShapeDtypeStruct((rows, D_MODEL), F32), tail_shape, tail_shape],
        scratch_shapes=[pltpu.VMEM((tm, D_MODEL), BF16), pltpu.VMEM((tm, D_MODEL), F32),
                        pltpu.VMEM((N_FF_CHUNKS, CONV_HALO, TF), F32),
                        pltpu.VMEM((N_FF_CHUNKS, CONV_HALO, TF), F32),
                        pltpu.VMEM((2 * (TF // TF_SUB), tm + CONV_HALO, TF_SUB), F32)],
        compiler_params=_cparams(("arbitrary", "arbitrary")),
        name="ffn",
    )(x1, w_up, w_up, u_first, u_first, conv_w, conv_w, conv_b, conv_b, w_down, g2, b2)
    last = slice(tiles_per_batch - 1, None, tiles_per_batch)
    return y, tail_g[last], tail_v[last]


def _ffn_tail_prev(u, prev1, prev2, conv_w, conv_b, w_down, x1, g2, b2, *, tm):
    rows = u.shape[0]
    gate_t = lambda i, c: (i, c)
    val_t = lambda i, c: (i, N_FF_CHUNKS + c)
    return pl.pallas_call(
        _ffn_prev_kernel,
        grid=(rows // tm, N_FF_CHUNKS),
        in_specs=[pl.BlockSpec((tm, TF), gate_t), pl.BlockSpec((tm, TF), val_t),
                  pl.BlockSpec((tm, TF), gate_t), pl.BlockSpec((tm, TF), val_t),
                  pl.BlockSpec((tm, TF), gate_t), pl.BlockSpec((tm, TF), val_t)]
                 + _ffn_common_specs(tm),
        out_specs=pl.BlockSpec((tm, D_MODEL), lambda i, c: (i, 0)),
        out_shape=jax.ShapeDtypeStruct((rows, D_MODEL), F32),
        scratch_shapes=[pltpu.VMEM((tm, D_MODEL), F32)],
        compiler_params=_cparams(("arbitrary", "arbitrary")),
        name="ffn_tail_small",
    )(u, u, prev1, prev1, prev2, prev2, conv_w, conv_w, conv_b, conv_b, w_down, x1, g2, b2)


def kernel(x_prompt, x_sample, cache_k, cache_v, cache_idx_k, state_pool, state_conv, page_table, meta_tokens,
           w_in, w_pool, pool_scale, w_out, ln1_g, ln1_b, w_up, conv_w, conv_b, w_down, ln2_g, ln2_b):
    n_phys = cache_k.shape[1]

    wq, wk, wv, wiq, wik, wiw, wp = jnp.split(w_in[0], IN_OFFSETS, axis=1)
    zcols = lambda n: jnp.zeros((D_MODEL, n), F32)
    w_all = jnp.concatenate([wq, wp, wk, wv, wiq, wik, zcols(64), wiw, zcols(120)], axis=1).astype(BF16)
    w_pool_b = w_pool[0].astype(BF16)
    w_out_b = w_out[0].astype(BF16)
    w_up_b = w_up[0].astype(BF16)
    w_down_b = w_down[0].astype(BF16)
    row2 = lambda a: a.reshape(1, -1)

    pos_prompt = N_META + np.arange(SEQ)
    pos_small = np.concatenate([np.tile(PAST_LEN + np.arange(DEC_SEQ), DEC_BATCH), np.arange(N_META),
                                np.zeros((SMALL_ROWS - N_SAMPLE_ROWS - N_META,), np.int64)])
    tabs_p = _rope_tables(pos_prompt, ROT_DIM, HEAD_DIM) + _rope_tables(pos_prompt, IDX_ROT_DIM, IDX_DIM)
    tabs_s = _rope_tables(pos_small, ROT_DIM, HEAD_DIM) + _rope_tables(pos_small, IDX_ROT_DIM, IDX_DIM)

    xp = x_prompt.reshape(N_PROMPT_ROWS, D_MODEL)
    xs = jnp.concatenate([x_sample.reshape(N_SAMPLE_ROWS, D_MODEL), meta_tokens.astype(F32),
                          jnp.zeros((SMALL_ROWS - N_SAMPLE_ROWS - N_META, D_MODEL), F32)], axis=0)

    tm_p = 1024
    hp = _project(xp, w_all, tabs_p, SEQ // tm_p, tm_p)
    hs = _project(xs, w_all, tabs_s, 1, SMALL_ROWS)

    meta = slice(META_ROW0, META_ROW0 + N_META)

    def with_meta(c0, width):
        m = jnp.broadcast_to(hs[meta, c0:c0 + width][None], (BATCH, N_META, width))
        return jnp.concatenate([m, hp[:, c0:c0 + width].reshape(BATCH, SEQ, width)], axis=1)

    k_prompt = with_meta(C_K, KV_WIDTH)
    v_prompt = with_meta(C_V, KV_WIDTH)
    ik_prompt = with_meta(C_IK, IDX_DIM)
    pad_keys = lambda a: jnp.pad(a.astype(BF16), ((0, 0), (0, NK_PROMPT - T_PROMPT), (0, 0)))
    kb, vb, ikb = pad_keys(k_prompt), pad_keys(v_prompt), pad_keys(ik_prompt)

    blocks_per_batch = SEQ // PA_TQ
    a_groups = []
    for j0 in range(0, blocks_per_batch, PA_GROUP):
        nk = N_META + (j0 + PA_GROUP) * PA_TQ
        nk = -(-nk // LANES) * LANES
        a_groups.append(_prompt_attention(hp, kb, vb, ikb, tq=PA_TQ, n_batch=BATCH, n_blocks=PA_GROUP,
                                          row_block0=j0, blocks_per_batch=blocks_per_batch,
                                          pos_first=N_META + j0 * PA_TQ, nk=nk))
    a_p = jnp.concatenate(a_groups, axis=1).reshape(N_PROMPT_ROWS, ATTN_WIDTH)
    a_m = _prompt_attention(hs, kb, vb, ikb, tq=LANES, n_batch=1, n_blocks=1, row_block0=META_ROW0 // LANES,
                            blocks_per_batch=1, pos_first=0, nk=LANES)[0]

    hs_s = hs[:N_SAMPLE_ROWS]
    iq_s = hs_s[:, C_IQ:C_IQ + N_IDX_HEADS * IDX_DIM].reshape(SS_STEPS, SS_PAIR, DEC_SEQ, N_IDX_HEADS, IDX_DIM)
    iq_s = iq_s.transpose(0, 1, 3, 2, 4)
    iq2 = jnp.stack([jnp.pad(iq_s[:, e], ((0, 0), (0, 0), (DEC_SEQ * e, DEC_SEQ * (SS_PAIR - 1 - e)), (0, 0)))
                     for e in range(SS_PAIR)], axis=1).reshape(SS_STEPS, SS_PAIR, N_IDX_HEADS * 8, IDX_DIM)
    w_s = hs_s[:, C_IW:C_IW + N_IDX_HEADS].reshape(SS_STEPS, SS_PAIR * DEC_SEQ, N_IDX_HEADS)
    w2 = w_s.transpose(0, 2, 1).reshape(SS_STEPS, N_IDX_HEADS * 8, 1)
    ik_new_t = jnp.pad(hs_s[:, C_IK:C_IK + IDX_DIM].reshape(DEC_BATCH, DEC_SEQ, IDX_DIM).transpose(0, 2, 1),
                       ((0, 0), (0, 0), (0, LANES - DEC_SEQ))).reshape(SS_STEPS, SS_PAIR, IDX_DIM, LANES)

    pt_pad = jnp.pad(page_table, ((0, 0), (0, N_MROWS - N_PAGES)))
    pt_hi = jnp.broadcast_to((pt_pad // 64).astype(F32)[:, None, :], (DEC_BATCH, 8, N_MROWS))
    pt_lo = jnp.broadcast_to((pt_pad % 64).astype(F32)[:, None, :], (DEC_BATCH, 8, N_MROWS))
    rows_sel, gbias, nbias = _sample_select(page_table, iq2, w2, ik_new_t, pt_hi, pt_lo,
                                            jnp.swapaxes(cache_idx_k[0], 1, 2))

    def new_rows(c0):
        a = hs_s[:, c0:c0 + KV_WIDTH].reshape(DEC_BATCH, DEC_SEQ * N_KV_HEADS, HEAD_DIM)
        return jnp.pad(a, ((0, 0), (0, LANES - DEC_SEQ * N_KV_HEADS), (0, 0)))

    a_s4 = _sample_attention(rows_sel.reshape(-1), gbias.reshape(DEC_BATCH, DEC_SEQ, SA_COLS),
                             nbias.reshape(DEC_BATCH, DEC_SEQ, LANES),
                             hs_s[:, C_Q:C_Q + ATTN_WIDTH].reshape(DEC_BATCH, DEC_SEQ, N_HEADS, HEAD_DIM),
                             new_rows(C_K), new_rows(C_V),
                             cache_k[0].reshape(n_phys * PAGE_SIZE * N_KV_HEADS, HEAD_DIM),
                             cache_v[0].reshape(n_phys * PAGE_SIZE * N_KV_HEADS, HEAD_DIM))
    a_s = a_s4.reshape(N_SAMPLE_ROWS, ATTN_WIDTH).astype(BF16)
    a_small = jnp.concatenate([a_s, a_m[:N_META], jnp.zeros((SMALL_ROWS - N_SAMPLE_ROWS - N_META, ATTN_WIDTH),
                                                            BF16)], axis=0)

    win = jnp.repeat(jnp.asarray(POOL_WINDOWS, F32), POOL_GROUP)
    p_meta = hs[meta, C_P:C_P + POOL_WIDTH]
    tm_pool = 256
    pooled_p = _pool(hp, C_P // POOL_WIDTH, hp, hs, (META_ROW0 // N_META, C_P // POOL_WIDTH), row2(win),
                     rows=N_PROMPT_ROWS, tm=tm_pool, tiles_per_batch=SEQ // tm_pool)
    p_s = hs_s[:, C_P:C_P + POOL_WIDTH].reshape(DEC_BATCH, DEC_SEQ, POOL_WIDTH)
    grp = 24
    ext_s = jnp.concatenate([jnp.zeros((DEC_BATCH, 1, POOL_WIDTH), F32), state_pool[0], p_s,
                             jnp.zeros((DEC_BATCH, grp - 1 - POOL_CTX - DEC_SEQ, POOL_WIDTH), F32)], axis=1)
    ext_small = jnp.concatenate([ext_s.reshape(DEC_BATCH * grp, POOL_WIDTH),
                                 jnp.zeros((N_META, POOL_WIDTH), F32), p_meta], axis=0)
    n_ext = DEC_BATCH * grp + 2 * N_META
    cnt_meta = jnp.minimum(win[None, :], (jnp.arange(N_META, dtype=F32) + 1.0)[:, None])
    cnt_small = jnp.concatenate([jnp.broadcast_to(win[None], (n_ext - N_META, POOL_WIDTH)), cnt_meta], axis=0)
    zeros_halo = jnp.zeros((N_META, POOL_WIDTH), F32)
    pooled_ext = _pool(ext_small, 0, zeros_halo, zeros_halo, (0, 0), cnt_small,
                       rows=n_ext, tm=n_ext, tiles_per_batch=1)
    pooled_small = jnp.concatenate(
        [pooled_ext[:DEC_BATCH * grp].reshape(DEC_BATCH, grp, POOL_WIDTH)[:, 16:16 + DEC_SEQ].reshape(
            N_SAMPLE_ROWS, POOL_WIDTH),
         pooled_ext[n_ext - N_META:],
         jnp.zeros((SMALL_ROWS - N_SAMPLE_ROWS - N_META, POOL_WIDTH), BF16)], axis=0)

    mix_args = (w_pool_b, row2(pool_scale[0]), w_out_b, row2(ln1_g[0]), row2(ln1_b[0]))
    x1_p = _mix(pooled_p, a_p, xp, *mix_args, tm=256)
    x1_s = _mix(pooled_small, a_small, xs, *mix_args, tm=SMALL_ROWS)

    u_s = _matmul(x1_s.astype(BF16), w_up_b, SMALL_ROWS, "ffn_up_small")
    ffn_args = (conv_w[0], row2(conv_b[0]), w_down_b)
    ln2 = (row2(ln2_g[0]), row2(ln2_b[0]))
    tm_f = 512
    y_p, tail_g, tail_v = _ffn_fused(x1_p, w_up_b, u_s, (META_ROW0 + N_META) // CONV_HALO - 1, *ffn_args, *ln2,
                                     tm=tm_f, tiles_per_batch=SEQ // tm_f)
    u_s3 = u_s[:N_SAMPLE_ROWS].reshape(DEC_BATCH, DEC_SEQ, 2 * D_FF)
    ext_u = jnp.concatenate([state_conv[0], u_s3], axis=1)
    u_m = u_s[meta]
    ext_m = jnp.concatenate([jnp.zeros((2, 2 * D_FF), F32), u_m], axis=0)
    tail0 = jnp.zeros((SMALL_ROWS - N_SAMPLE_ROWS - N_META, 2 * D_FF), F32)
    prev1 = jnp.concatenate([ext_u[:, 1:1 + DEC_SEQ].reshape(N_SAMPLE_ROWS, -1), ext_m[1:1 + N_META], tail0], 0)
    prev2 = jnp.concatenate([ext_u[:, 0:DEC_SEQ].reshape(N_SAMPLE_ROWS, -1), ext_m[0:N_META], tail0], 0)
    y_s = _ffn_tail_prev(u_s, prev1, prev2, *ffn_args, x1_s, *ln2, tm=SMALL_ROWS)

    y_prompt = y_p.reshape(BATCH, SEQ, D_MODEL)
    y_sample = y_s[:N_SAMPLE_ROWS].reshape(DEC_BATCH, DEC_SEQ, D_MODEL)
    pool_prompt = hp.reshape(BATCH, SEQ, PROJ_WIDTH)[:, SEQ - POOL_CTX:, C_P:C_P + POOL_WIDTH]
    conv_prompt = jnp.concatenate([tail_g[:, CONV_HALO - 2:], tail_v[:, CONV_HALO - 2:]], axis=-1)
    k_sample = hs_s[:, C_K:C_K + KV_WIDTH].reshape(DEC_BATCH, DEC_SEQ, N_KV_HEADS, HEAD_DIM)
    v_sample = hs_s[:, C_V:C_V + KV_WIDTH].reshape(DEC_BATCH, DEC_SEQ, N_KV_HEADS, HEAD_DIM)
    ik_sample = hs_s[:, C_IK:C_IK + IDX_DIM].reshape(DEC_BATCH, DEC_SEQ, IDX_DIM)
    pool_sample = jnp.concatenate([state_pool[0], p_s], axis=1)[:, DEC_SEQ:]
    conv_sample = ext_u[:, DEC_SEQ:]
    return (y_prompt, y_sample,
            k_prompt.reshape(1, BATCH, T_PROMPT, N_KV_HEADS, HEAD_DIM),
            v_prompt.reshape(1, BATCH, T_PROMPT, N_KV_HEADS, HEAD_DIM),
            ik_prompt[None], pool_prompt[None], conv_prompt[None],
            k_sample[None], v_sample[None], ik_sample[None], pool_sample[None], conv_sample[None])
```

```python
import functools

import numpy as np
import jax
import jax.numpy as jnp
from jax import lax
from jax.experimental import pallas as pl
from jax.experimental.pallas import tpu as pltpu

F32 = jnp.float32
BF16 = jnp.bfloat16
I32 = jnp.int32

D_MODEL = 2048
BATCH = 8
SEQ = 2048
DEC_BATCH = 32
DEC_SEQ = 4
PAST_LEN = 16384
PAGE_SIZE = 128
N_META = 16
ATTN_WIDTH = 1024
N_HEADS = 8
HEAD_DIM = 128
N_KV_HEADS = 4
KV_WIDTH = N_KV_HEADS * HEAD_DIM
ROT_DIM = 32
N_IDX_HEADS = 8
IDX_DIM = 64
IDX_ROT_DIM = 16
TOPK = 256
POOL_WIDTH = 1024
POOL_WINDOWS = (2, 4, 8, 16)
POOL_GROUP = 256
POOL_CTX = 15
D_FF = 5632
ROPE_THETA = 500000.0
LN_EPS = 1e-5
DEEPNORM_ALPHA = 2.0 ** 0.25
IN_OFFSETS = (1024, 1536, 2048, 2560, 2624, 2632)

LANES = 128
N_PROMPT_ROWS = BATCH * SEQ
N_SAMPLE_ROWS = DEC_BATCH * DEC_SEQ
SMALL_ROWS = 256
META_ROW0 = N_SAMPLE_ROWS
T_PROMPT = N_META + SEQ
NK_PROMPT = 2176
NK_SAMPLE = PAST_LEN + LANES
N_PAGES = PAST_LEN // PAGE_SIZE

C_Q, C_P, C_K, C_V, C_IQ, C_IK, C_IW = 0, 1024, 2048, 2560, 3072, 3584, 3712
PROJ_WIDTH = 3840
TN = 256
ROPE_A_TILES = (0, 1, 2, 3, 8, 9)
ROPE_B_TILES = (12, 13)
ROPE_BH_TILE = 14

VMEM_LIMIT = 56 * 1024 * 1024
FFN_VMEM_LIMIT = 62 * 1024 * 1024
INT_MIN = -2 ** 31
NEG_INF = float("-inf")
LOG2_E = 1.4426950408889634


def _cparams(sem, vmem_limit=VMEM_LIMIT, **kw):
    return pltpu.CompilerParams(dimension_semantics=sem, vmem_limit_bytes=vmem_limit, **kw)


def _rope_tables(pos, rot_dim, head_dim):
    half = rot_dim // 2
    inv_freq = ROPE_THETA ** (-np.arange(half, dtype=np.float64) / half)
    ang = np.asarray(pos, np.float64)[:, None] * inv_freq[None, :]
    cos, sin = np.cos(ang), np.sin(ang)
    rows = len(pos)
    zh = np.zeros((rows, half))
    rest0 = np.zeros((rows, head_dim - rot_dim))
    c = np.concatenate([cos, cos, rest0 + 1.0], axis=1)
    s1 = np.concatenate([-sin, zh, rest0], axis=1)
    s2 = np.concatenate([zh, sin, rest0], axis=1)
    reps = LANES // head_dim
    return tuple(jnp.asarray(np.tile(t, (1, reps)), F32) for t in (c, s1, s2))


def _rope(x, c_ref, s1_ref, s2_ref, half):
    return (x * c_ref[...] + pltpu.roll(x, LANES - half, 1) * s1_ref[...]
            + pltpu.roll(x, half, 1) * s2_ref[...])


def _proj_kernel(x_ref, w_ref, ca, sa1, sa2, cb, sb1, sb2, o_ref, xb_ref):
    j = pl.program_id(1)

    @pl.when(j == 0)
    def _():
        xb_ref[...] = x_ref[...].astype(BF16)

    acc = jnp.dot(xb_ref[...], w_ref[...], preferred_element_type=F32)
    is_a = functools.reduce(jnp.logical_or, [j == t for t in ROPE_A_TILES])
    is_b = functools.reduce(jnp.logical_or, [j == t for t in ROPE_B_TILES])
    is_bh = j == ROPE_BH_TILE

    @pl.when(is_a)
    def _():
        o_ref[:, :LANES] = _rope(acc[:, :LANES], ca, sa1, sa2, ROT_DIM // 2)
        o_ref[:, LANES:] = _rope(acc[:, LANES:], ca, sa1, sa2, ROT_DIM // 2)

    @pl.when(is_b)
    def _():
        o_ref[:, :LANES] = _rope(acc[:, :LANES], cb, sb1, sb2, IDX_ROT_DIM // 2)
        o_ref[:, LANES:] = _rope(acc[:, LANES:], cb, sb1, sb2, IDX_ROT_DIM // 2)

    @pl.when(is_bh)
    def _():
        o_ref[:, :LANES] = _rope(acc[:, :LANES], cb, sb1, sb2, IDX_ROT_DIM // 2)
        o_ref[:, LANES:] = acc[:, LANES:]

    @pl.when(jnp.logical_not(is_a | is_b | is_bh))
    def _():
        o_ref[...] = acc


def _project(x, w_all, tabs, tab_blocks, tm):
    rows = x.shape[0]
    tab_spec = pl.BlockSpec((tm, LANES), lambda i, j: (i % tab_blocks, 0))
    return pl.pallas_call(
        _proj_kernel,
        grid=(rows // tm, PROJ_WIDTH // TN),
        in_specs=[pl.BlockSpec((tm, D_MODEL), lambda i, j: (i, 0)),
                  pl.BlockSpec((D_MODEL, TN), lambda i, j: (0, j))] + [tab_spec] * 6,
        out_specs=pl.BlockSpec((tm, TN), lambda i, j: (i, j)),
        out_shape=jax.ShapeDtypeStruct((rows, PROJ_WIDTH), F32),
        scratch_shapes=[pltpu.VMEM((tm, D_MODEL), BF16)],
        compiler_params=_cparams(("arbitrary", "arbitrary")),
        name="in_proj",
    )(x, w_all, *tabs)


def _mm_kernel(x_ref, w_ref, o_ref):
    o_ref[...] = jnp.dot(x_ref[...], w_ref[...], preferred_element_type=F32)


def _matmul(x, w, tm, name):
    rows, kdim = x.shape
    n = w.shape[1]
    return pl.pallas_call(
        _mm_kernel,
        grid=(rows // tm, n // TN),
        in_specs=[pl.BlockSpec((tm, kdim), lambda i, j: (i, 0)),
                  pl.BlockSpec((kdim, TN), lambda i, j: (0, j))],
        out_specs=pl.BlockSpec((tm, TN), lambda i, j: (i, j)),
        out_shape=jax.ShapeDtypeStruct((rows, n), F32),
        compiler_params=_cparams(("arbitrary", "arbitrary")),
        name=name,
    )(x, w)


def _ordinal_to_float(key):
    bits = key ^ ((key >> 31) & 0x7FFFFFFF)
    return lax.bitcast_convert_type(bits, F32)


def _select_topk(sc_ref, col, valid, n_col_bits, cut_scr):
    rows = sc_ref.shape[0]
    k_f = float(TOPK)

    def count(pred):
        return jnp.sum(jnp.where(pred, 1.0, 0.0), axis=1, keepdims=True)

    few = count(valid) <= k_f
    key0 = jnp.where(count(sc_ref[...] >= 0.0) >= k_f, 0, INT_MIN).astype(I32)

    def value_step(i, key):
        cand = key + jnp.left_shift(jnp.int32(1), 30 - i)
        cnt = count(sc_ref[...] >= _ordinal_to_float(cand))
        return jnp.where(cnt >= k_f, cand, key)

    key = lax.fori_loop(0, 31, value_step, key0)
    thr = jnp.where(few, NEG_INF, _ordinal_to_float(key))
    need = k_f - count(sc_ref[...] > thr)
    surplus = jnp.where(few, 0.0, count(sc_ref[...] >= thr) - k_f)

    cut_scr[...] = jnp.full(cut_scr.shape, 2 ** n_col_bits - 1, I32)

    @pl.when(jnp.max(surplus) > 0.0)
    def _():
        def col_step(i, cut):
            cand = cut + jnp.left_shift(jnp.int32(1), n_col_bits - 1 - i)
            cnt = count((sc_ref[...] == thr) & (col < cand))
            return jnp.where(cnt < need, cand, cut)

        cut_scr[...] = lax.fori_loop(0, n_col_bits, col_step, jnp.zeros((rows, 1), I32))

    sc = sc_ref[...]
    return valid & ((sc > thr) | ((sc == thr) & (col <= cut_scr[...])))


def _select_topk_tiled(sc_ref, valid_fn, n_col_bits, cut_scr, out_ref):
    rows, n = sc_ref.shape
    k_f = float(TOPK)
    lane = lax.broadcasted_iota(I32, (rows, LANES), 1)
    tiles = [(t * LANES, slice(t * LANES, (t + 1) * LANES)) for t in range(n // LANES)]
    wide = lambda v: jnp.broadcast_to(v, (rows, LANES))

    def count(pred_fn):
        acc = jnp.zeros((rows, LANES), F32)
        for c0, cols in tiles:
            acc = acc + jnp.where(pred_fn(sc_ref[:, cols], lane + c0), 1.0, 0.0)
        return jnp.sum(acc, axis=1, keepdims=True)

    few = count(lambda sc, col: valid_fn(col)) <= k_f
    key0 = jnp.where(count(lambda sc, col: sc >= 0.0) >= k_f, 0, INT_MIN).astype(I32)

    def value_step(i, key):
        cand = wide(_ordinal_to_float(key + jnp.left_shift(jnp.int32(1), 30 - i)))
        cnt = count(lambda sc, col: sc >= cand)
        return jnp.where(cnt >= k_f, key + jnp.left_shift(jnp.int32(1), 30 - i), key)

    key = lax.fori_loop(0, 31, value_step, key0)
    thr = jnp.where(few, NEG_INF, _ordinal_to_float(key))
    thr_w = wide(thr)
    need = k_f - count(lambda sc, col: sc > thr_w)
    surplus = jnp.where(few, 0.0, count(lambda sc, col: sc >= thr_w) - k_f)

    cut_scr[...] = jnp.full(cut_scr.shape, 2 ** n_col_bits - 1, I32)

    @pl.when(jnp.max(surplus) > 0.0)
    def _():
        def col_step(i, cut):
            cand = cut + jnp.left_shift(jnp.int32(1), n_col_bits - 1 - i)
            cand_w = wide(cand)
            cnt = count(lambda sc, col: (sc == thr_w) & (col < cand_w))
            return jnp.where(cnt < need, cand, cut)

        cut_scr[...] = lax.fori_loop(0, n_col_bits, col_step, jnp.zeros((rows, 1), I32))

    cut_w = wide(cut_scr[...])
    for c0, cols in tiles:
        sc, col = sc_ref[:, cols], lane + c0
        keep = valid_fn(col) & ((sc > thr_w) | ((sc == thr_w) & (col <= cut_w)))
        out_ref[:, cols] = jnp.where(keep, 1.0, 0.0)


def _pattn_kernel(q_ref, iq_ref, misc_ref, kb_ref, vb_ref, ikb_ref, o_ref, sc_scr, bias_scr, cut_scr,
                  *, pos_first, nk):
    tq = q_ref.shape[0]
    pos0 = pos_first + tq * pl.program_id(1)

    ik = ikb_ref[0]
    w = misc_ref[:, LANES:LANES + N_IDX_HEADS] * (N_IDX_HEADS ** -0.5)
    score = jnp.zeros((tq, nk), F32)
    for h in range(N_IDX_HEADS):
        iq_h = (iq_ref[:, h * IDX_DIM:(h + 1) * IDX_DIM] * (IDX_DIM ** -0.5)).astype(BF16)
        s_h = lax.dot_general(iq_h, ik, (((1,), (1,)), ((), ())), preferred_element_type=F32)
        score = score + jnp.maximum(s_h, 0.0) * w[:, h:h + 1]

    qpos = pos0 + lax.broadcasted_iota(I32, (tq, nk), 0)
    col = lax.broadcasted_iota(I32, (tq, nk), 1)
    causal = col <= qpos
    sc_scr[...] = jnp.where(causal, score, NEG_INF)
    sel = _select_topk(sc_scr, col, causal, 12, cut_scr)
    bias_scr[...] = jnp.where(sel, 0.0, NEG_INF)

    q_scale = (HEAD_DIM ** -0.5) * LOG2_E
    for h in range(N_HEADS):
        kv = h // (N_HEADS // N_KV_HEADS)
        q_h = (q_ref[:, h * HEAD_DIM:(h + 1) * HEAD_DIM] * q_scale).astype(BF16)
        k_g = kb_ref[0, :, kv * HEAD_DIM:(kv + 1) * HEAD_DIM]
        s = lax.dot_general(q_h, k_g, (((1,), (1,)), ((), ())), preferred_element_type=F32)
        s = s + bias_scr[...]
        m = jnp.max(s, axis=1, keepdims=True)
        p = jnp.exp2(s - m)
        l = jnp.sum(p, axis=1, keepdims=True)
        v_g = vb_ref[0, :, kv * HEAD_DIM:(kv + 1) * HEAD_DIM]
        o = jnp.dot(p.astype(BF16), v_g, preferred_element_type=F32) / l
        o_ref[0, :, h * HEAD_DIM:(h + 1) * HEAD_DIM] = o.astype(o_ref.dtype)


def _prompt_attention(h, kb, vb, ikb, *, tq, n_batch, n_blocks, row_block0, blocks_per_batch, pos_first, nk):
    kern = functools.partial(_pattn_kernel, pos_first=pos_first, nk=nk)
    qrow = lambda b, j: row_block0 + b * blocks_per_batch + j
    return pl.pallas_call(
        kern,
        grid=(n_batch, n_blocks),
        in_specs=[pl.BlockSpec((tq, ATTN_WIDTH), lambda b, j: (qrow(b, j), C_Q // ATTN_WIDTH)),
                  pl.BlockSpec((tq, 512), lambda b, j: (qrow(b, j), C_IQ // 512)),
                  pl.BlockSpec((tq, 256), lambda b, j: (qrow(b, j), C_IK // 256)),
                  pl.BlockSpec((1, nk, KV_WIDTH), lambda b, j: (b, 0, 0)),
                  pl.BlockSpec((1, nk, KV_WIDTH), lambda b, j: (b, 0, 0)),
                  pl.BlockSpec((1, nk, IDX_DIM), lambda b, j: (b, 0, 0))],
        out_specs=pl.BlockSpec((1, tq, ATTN_WIDTH), lambda b, j: (b, j, 0)),
        out_shape=jax.ShapeDtypeStruct((n_batch, n_blocks * tq, ATTN_WIDTH), BF16),
        scratch_shapes=[pltpu.VMEM((tq, nk), F32), pltpu.VMEM((tq, nk), F32),
                        pltpu.VMEM((tq, 1), I32)],
        compiler_params=_cparams(("arbitrary", "arbitrary")),
        name="prompt_attention",
    )(h, h, h, kb, vb, ikb)


KV_SHIFT = 2
N_MROWS = 256


SS_PAIR = 2
SS_STEPS = DEC_BATCH // SS_PAIR
SS_CHUNK = 2048


def _ssel_kernel(pt_ref, iq_ref, w_ref, iknew_ref, pthi_ref, ptlo_ref, cache_ref,
                 rows_ref, gbias_ref, nbias_ref, ikbuf, sem, sc_scr, sel_scr, m_scr, cut_scr):
    s = pl.program_id(0)

    def page_copy(step, slot, e, p):
        return pltpu.make_async_copy(cache_ref.at[pt_ref[SS_PAIR * step + e, p]],
                                     ikbuf.at[slot, e, :, pl.ds(pl.multiple_of(p * PAGE_SIZE, PAGE_SIZE), PAGE_SIZE)],
                                     sem.at[slot])

    def fetch(step, slot):
        def issue(p, carry):
            for e in range(SS_PAIR):
                page_copy(step, slot, e, p).start()
            return carry
        lax.fori_loop(0, N_PAGES, issue, 0)

    slot = s % 2

    @pl.when(s == 0)
    def _():
        fetch(0, 0)

    @pl.when(s + 1 < SS_STEPS)
    def _():
        fetch(s + 1, 1 - slot)

    pltpu.make_async_copy(ikbuf.at[slot], ikbuf.at[slot], sem.at[slot]).wait()

    w = w_ref[0] * (N_IDX_HEADS ** -0.5)
    iq = [iq_ref[0, e].astype(BF16) for e in range(SS_PAIR)]

    def score_of(keys_of):
        r = None
        for e in range(SS_PAIR):
            s_e = jnp.dot(iq[e], keys_of(e).astype(BF16), preferred_element_type=F32)
            r_e = jnp.maximum(s_e * (IDX_DIM ** -0.5), 0.0)
            r = r_e if r is None else r + r_e
        r = r * w
        acc = r[0:8]
        for h in range(1, N_IDX_HEADS):
            acc = acc + r[8 * h:8 * h + 8]
        return acc

    row0 = pl.multiple_of(s * 8, 8)
    for c in range(PAST_LEN // SS_CHUNK):
        cols = slice(c * SS_CHUNK, (c + 1) * SS_CHUNK)
        sc_scr[pl.ds(row0, 8), cols] = score_of(lambda e: ikbuf[slot, e, :, cols])
    sc_scr[pl.ds(row0, 8), PAST_LEN:] = score_of(lambda e: iknew_ref[0, e])

    @pl.when(s == SS_STEPS - 1)
    def _():
        _ssel_finish(pthi_ref, ptlo_ref, rows_ref, gbias_ref, nbias_ref, sc_scr, sel_scr, m_scr, cut_scr)


def _ssel_finish(pthi_ref, ptlo_ref, rows_ref, gbias_ref, nbias_ref, sc_scr, sel_scr, m_scr, cut_scr):
    n_q = sc_scr.shape[0]
    qpos = PAST_LEN + (lax.broadcasted_iota(I32, (n_q, LANES), 0) & (DEC_SEQ - 1))
    valid_fn = lambda col: col <= qpos
    lane = lax.broadcasted_iota(I32, (n_q, LANES), 1)
    for t in range(NK_SAMPLE // LANES):
        cols = slice(t * LANES, (t + 1) * LANES)
        sc_scr[:, cols] = jnp.where(valid_fn(lane + t * LANES), sc_scr[:, cols], NEG_INF)
    _select_topk_tiled(sc_scr, valid_fn, 15, cut_scr, sel_scr)

    nt = (((1,), (1,)), ((), ()))
    ones_b = jnp.ones((8, LANES), BF16)
    upper_pages = (lax.broadcasted_iota(I32, (N_MROWS, N_MROWS), 0)
                   < lax.broadcasted_iota(I32, (N_MROWS, N_MROWS), 1)).astype(BF16)
    upper_lanes = (lax.broadcasted_iota(I32, (LANES, LANES), 0)
                   < lax.broadcasted_iota(I32, (LANES, LANES), 1)).astype(BF16)
    lane_id = lax.broadcasted_iota(I32, (8, LANES), 1).astype(BF16)
    page_id = lax.broadcasted_iota(I32, (8, N_MROWS), 1).astype(BF16)
    jcol = lax.broadcasted_iota(I32, (TOPK, 1), 0).astype(F32)
    spread_picks = (lax.broadcasted_iota(I32, (TOPK, N_KV_HEADS * TOPK), 0)
                    == lax.broadcasted_iota(I32, (TOPK, N_KV_HEADS * TOPK), 1) >> KV_SHIFT).astype(BF16)
    spread_new = (lax.broadcasted_iota(I32, (LANES, LANES), 0)
                  == lax.broadcasted_iota(I32, (LANES, LANES), 1) >> KV_SHIFT).astype(BF16)
    m_scr[...] = jnp.zeros(m_scr.shape, F32)

    def compact(k, batch, new_sel):
        pt_hi = pthi_ref[batch].astype(BF16)
        pt_lo = ptlo_ref[batch].astype(BF16)
        m_b = m_scr[k].astype(BF16)
        cnt = lax.dot_general(ones_b, m_b, nt, preferred_element_type=F32)
        start = jnp.dot(cnt.astype(BF16), upper_pages, preferred_element_type=F32)
        cnt1, start1 = cnt[0:1], start[0:1]
        in_page = (start1 <= jcol) & (jcol < start1 + cnt1)
        a_b = jnp.where(in_page, 1.0, 0.0).astype(BF16)
        m_j = jnp.dot(a_b, m_b, preferred_element_type=F32)
        rank = jnp.dot(m_j.astype(BF16), upper_lanes, preferred_element_type=F32)
        start_j = jnp.sum(jnp.where(in_page, start1, 0.0), axis=1, keepdims=True)
        onehot = jnp.where((m_j > 0.5) & (rank == jcol - start_j), 1.0, 0.0).astype(BF16)
        off = lax.dot_general(lane_id, onehot, nt, preferred_element_type=F32)[0:1]
        page = lax.dot_general(page_id, a_b, nt, preferred_element_type=F32)[0:1]
        phys = (lax.dot_general(pt_hi, a_b, nt, preferred_element_type=F32)[0:1] * 64.0
                + lax.dot_general(pt_lo, a_b, nt, preferred_element_type=F32)[0:1])
        in_past = page < float(N_PAGES)
        row = jnp.where(in_past, phys * float(PAGE_SIZE) + off, 0.0).astype(I32)
        past8 = jnp.broadcast_to(jnp.where(in_past, 1.0, 0.0), (8, TOPK)).astype(BF16)
        past4 = jnp.dot(past8, spread_picks, preferred_element_type=F32)[0:1]
        new8 = jnp.broadcast_to(new_sel, (8, LANES)).astype(BF16)
        new4 = jnp.dot(new8, spread_new, preferred_element_type=F32)[0:1]
        return row, jnp.where(past4 > 0.5, 0.0, NEG_INF), jnp.where(new4 > 0.5, 0.0, NEG_INF)

    def compact_group(g, carry):
        r8 = pl.multiple_of(g * 8, 8)
        for c in range(N_PAGES + 1):
            tile = sel_scr[pl.ds(r8, 8), c * LANES:(c + 1) * LANES]
            for k in range(8):
                m_scr[k, c:c + 1, :] = tile[k:k + 1, :]
        new_tile = sel_scr[pl.ds(r8, 8), PAST_LEN:]
        rowid = lax.broadcasted_iota(I32, (8, 1), 0)
        rows_t = jnp.zeros((8, TOPK), I32)
        gb_t = jnp.zeros((8, N_KV_HEADS * TOPK), F32)
        nb_t = jnp.zeros((8, LANES), F32)
        for k in range(8):
            row, gb, nb = compact(k, g * (8 // DEC_SEQ) + k // DEC_SEQ, new_tile[k:k + 1, :])
            rows_t = jnp.where(rowid == k, row, rows_t)
            gb_t = jnp.where(rowid == k, gb, gb_t)
            nb_t = jnp.where(rowid == k, nb, nb_t)
        rows_ref[pl.ds(r8, 8), :] = rows_t
        gbias_ref[pl.ds(r8, 8), :] = gb_t
        nbias_ref[pl.ds(r8, 8), :] = nb_t
        return carry

    lax.fori_loop(0, n_q // 8, compact_group, 0)


def _sample_select(page_table, iq2, w2, ik_new_t, pt_hi, pt_lo, cache_idx_t):
    n_q = N_SAMPLE_ROWS
    whole = lambda shape: pl.BlockSpec(shape, lambda s, pt: (0,) * len(shape))
    grid_spec = pltpu.PrefetchScalarGridSpec(
        num_scalar_prefetch=1,
        grid=(SS_STEPS,),
        in_specs=[pl.BlockSpec((1, SS_PAIR, 64, IDX_DIM), lambda s, pt: (s, 0, 0, 0)),
                  pl.BlockSpec((1, 64, 1), lambda s, pt: (s, 0, 0)),
                  pl.BlockSpec((1, SS_PAIR, IDX_DIM, LANES), lambda s, pt: (s, 0, 0, 0)),
                  whole((DEC_BATCH, 8, N_MROWS)),
                  whole((DEC_BATCH, 8, N_MROWS)),
                  pl.BlockSpec(memory_space=pl.ANY)],
        out_specs=[whole((n_q, TOPK)), whole((n_q, N_KV_HEADS * TOPK)), whole((n_q, LANES))],
        scratch_shapes=[pltpu.VMEM((2, SS_PAIR, IDX_DIM, PAST_LEN), F32),
                        pltpu.SemaphoreType.DMA((2,)),
                        pltpu.VMEM((n_q, NK_SAMPLE), F32),
                        pltpu.VMEM((n_q, NK_SAMPLE), F32),
                        pltpu.VMEM((8, N_MROWS, LANES), F32),
                        pltpu.VMEM((n_q, 1), I32)])
    return pl.pallas_call(
        _ssel_kernel,
        grid_spec=grid_spec,
        out_shape=[jax.ShapeDtypeStruct((n_q, TOPK), I32),
                   jax.ShapeDtypeStruct((n_q, N_KV_HEADS * TOPK), F32),
                   jax.ShapeDtypeStruct((n_q, LANES), F32)],
        compiler_params=_cparams(("arbitrary",)),
        name="sample_select",
    )(page_table, iq2, w2, ik_new_t, pt_hi, pt_lo, cache_idx_t)


SA_COLS = N_KV_HEADS * TOPK


def _sattn_kernel(rows_ref, gbias_ref, nbias_ref, q_ref, knew_ref, vnew_ref, ck_ref, cv_ref, o_ref,
                  kbuf, vbuf, sem):
    b = pl.program_id(0)
    n_slabs = DEC_SEQ * TOPK

    def issue(t, carry):
        row = pl.multiple_of(rows_ref[b * n_slabs + t] * N_KV_HEADS, N_KV_HEADS)
        dst = pl.ds(pl.multiple_of(t * N_KV_HEADS, N_KV_HEADS), N_KV_HEADS)
        pltpu.make_async_copy(ck_ref.at[pl.ds(row, N_KV_HEADS)], kbuf.at[dst], sem.at[0]).start()
        pltpu.make_async_copy(cv_ref.at[pl.ds(row, N_KV_HEADS)], vbuf.at[dst], sem.at[1]).start()
        return carry

    lax.fori_loop(0, n_slabs, issue, 0, unroll=8)
    pltpu.make_async_copy(kbuf, kbuf, sem.at[0]).wait()
    pltpu.make_async_copy(vbuf, vbuf, sem.at[1]).wait()

    nt = (((1,), (1,)), ((), ()))
    scale = HEAD_DIM ** -0.5
    k_new = knew_ref[0].astype(BF16)
    v_new = vnew_ref[0].astype(BF16)
    heads_per_kv = N_HEADS // N_KV_HEADS

    def own_kv(width):
        head = lax.broadcasted_iota(I32, (N_HEADS, width), 0)
        col = lax.broadcasted_iota(I32, (N_HEADS, width), 1)
        return (col & (N_KV_HEADS - 1)) == (head >> (heads_per_kv.bit_length() - 1))

    for q in range(DEC_SEQ):
        q_h = q_ref[0, q].astype(BF16)
        rows = slice(q * SA_COLS, (q + 1) * SA_COLS)
        s = lax.dot_general(q_h, kbuf[rows, :].astype(BF16), nt, preferred_element_type=F32) * scale
        s = jnp.where(own_kv(SA_COLS), s + gbias_ref[0, q:q + 1, :], NEG_INF)
        sn = lax.dot_general(q_h, k_new, nt, preferred_element_type=F32) * scale
        sn = jnp.where(own_kv(LANES), sn + nbias_ref[0, q:q + 1, :], NEG_INF)
        m = jnp.maximum(jnp.max(s, axis=1, keepdims=True), jnp.max(sn, axis=1, keepdims=True))
        p = jnp.exp(s - m)
        pn = jnp.exp(sn - m)
        l = jnp.sum(p, axis=1, keepdims=True) + jnp.sum(pn, axis=1, keepdims=True)
        o = (jnp.dot(p.astype(BF16), vbuf[rows, :].astype(BF16), preferred_element_type=F32)
             + jnp.dot(pn.astype(BF16), v_new, preferred_element_type=F32)) / l
        o_ref[0, q] = o


def _sample_attention(rows_flat, gbias, nbias, q_s, k_new, v_new, cache_k2d, cache_v2d):
    grid_spec = pltpu.PrefetchScalarGridSpec(
        num_scalar_prefetch=1,
        grid=(DEC_BATCH,),
        in_specs=[pl.BlockSpec((1, DEC_SEQ, SA_COLS), lambda b, r: (b, 0, 0)),
                  pl.BlockSpec((1, DEC_SEQ, LANES), lambda b, r: (b, 0, 0)),
                  pl.BlockSpec((1, DEC_SEQ, N_HEADS, HEAD_DIM), lambda b, r: (b, 0, 0, 0)),
                  pl.BlockSpec((1, LANES, HEAD_DIM), lambda b, r: (b, 0, 0)),
                  pl.BlockSpec((1, LANES, HEAD_DIM), lambda b, r: (b, 0, 0)),
                  pl.BlockSpec(memory_space=pl.ANY),
                  pl.BlockSpec(memory_space=pl.ANY)],
        out_specs=pl.BlockSpec((1, DEC_SEQ, N_HEADS, HEAD_DIM), lambda b, r: (b, 0, 0, 0)),
        scratch_shapes=[pltpu.VMEM((DEC_SEQ * SA_COLS, HEAD_DIM), F32),
                        pltpu.VMEM((DEC_SEQ * SA_COLS, HEAD_DIM), F32),
                        pltpu.SemaphoreType.DMA((2,))])
    return pl.pallas_call(
        _sattn_kernel,
        grid_spec=grid_spec,
        out_shape=jax.ShapeDtypeStruct((DEC_BATCH, DEC_SEQ, N_HEADS, HEAD_DIM), F32),
        compiler_params=_cparams(("arbitrary",), disable_bounds_checks=True),
        name="sample_attention",
    )(rows_flat, gbias, nbias, q_s, k_new, v_new, cache_k2d, cache_v2d)


def _pool_kernel(p_ref, halo_ref, first_ref, cnt_ref, o_ref, ext_scr, *, tiles_per_batch):
    i = pl.program_id(0)
    tm = p_ref.shape[0]
    ext_scr[0:N_META, :] = jnp.where(i % tiles_per_batch == 0, first_ref[...], halo_ref[...])
    ext_scr[N_META:, :] = p_ref[...]
    for g, win in enumerate(POOL_WINDOWS):
        cols = slice(g * POOL_GROUP, (g + 1) * POOL_GROUP)
        x_self = ext_scr[N_META:N_META + tm, cols]
        acc = x_self
        for d in range(1, win):
            acc = acc + ext_scr[N_META - d:N_META - d + tm, cols]
        o_ref[:, cols] = (acc / cnt_ref[:, cols] - x_self).astype(o_ref.dtype)


def _pool(p_src, p_col_block, halo_src, first_src, first_block, cnt, *, rows, tm, tiles_per_batch):
    kern = functools.partial(_pool_kernel, tiles_per_batch=tiles_per_batch)
    halo_per_tile = tm // N_META
    cnt_rows = cnt.shape[0]
    cnt_map = (lambda i: (0, 0)) if cnt_rows == 1 else (lambda i: (i, 0))
    return pl.pallas_call(
        kern,
        grid=(rows // tm,),
        in_specs=[pl.BlockSpec((tm, POOL_WIDTH), lambda i: (i, p_col_block)),
                  pl.BlockSpec((N_META, POOL_WIDTH),
                               lambda i: (jnp.maximum(i * halo_per_tile - 1, 0), p_col_block)),
                  pl.BlockSpec((N_META, POOL_WIDTH), lambda i: first_block),
                  pl.BlockSpec((1 if cnt_rows == 1 else tm, POOL_WIDTH), cnt_map)],
        out_specs=pl.BlockSpec((tm, POOL_WIDTH), lambda i: (i, 0)),
        out_shape=jax.ShapeDtypeStruct((rows, POOL_WIDTH), BF16),
        scratch_shapes=[pltpu.VMEM((tm + N_META, POOL_WIDTH), F32)],
        compiler_params=_cparams(("arbitrary",)),
        name="pool",
    )(p_src, halo_src, first_src, cnt)


def _layer_norm(y, g_ref, b_ref):
    mu = jnp.mean(y, axis=-1, keepdims=True)
    var = jnp.mean(jnp.square(y - mu), axis=-1, keepdims=True)
    return (y - mu) * lax.rsqrt(var + LN_EPS) * g_ref[...] + b_ref[...]


def _mix_kernel(pooled_ref, a_ref, x_ref, wp_ref, ps_ref, wo_ref, g_ref, b_ref, x1_ref):
    parts = []
    for g in range(len(POOL_WINDOWS)):
        cols = slice(g * POOL_GROUP, (g + 1) * POOL_GROUP)
        parts.append(jnp.dot(pooled_ref[:, cols], wp_ref[g], preferred_element_type=F32))
    m = jnp.concatenate(parts, axis=1) * ps_ref[...]
    mix = (jnp.dot(a_ref[...], wo_ref[:ATTN_WIDTH, :], preferred_element_type=F32)
           + jnp.dot(m.astype(BF16), wo_ref[ATTN_WIDTH:, :], preferred_element_type=F32))
    x1_ref[...] = _layer_norm(DEEPNORM_ALPHA * x_ref[...] + mix, g_ref, b_ref)


def _mix(pooled, a, x, w_pool, pool_scale, w_out, g1, b1, tm):
    rows = x.shape[0]
    row = lambda i: (i, 0)
    const2 = lambda i: (0, 0)
    return pl.pallas_call(
        _mix_kernel,
        grid=(rows // tm,),
        in_specs=[pl.BlockSpec((tm, POOL_WIDTH), row),
                  pl.BlockSpec((tm, ATTN_WIDTH), row),
                  pl.BlockSpec((tm, D_MODEL), row),
                  pl.BlockSpec((len(POOL_WINDOWS), POOL_GROUP, POOL_GROUP), lambda i: (0, 0, 0)),
                  pl.BlockSpec((1, POOL_WIDTH), const2),
                  pl.BlockSpec((D_MODEL, D_MODEL), const2),
                  pl.BlockSpec((1, D_MODEL), const2),
                  pl.BlockSpec((1, D_MODEL), const2)],
        out_specs=pl.BlockSpec((tm, D_MODEL), row),
        out_shape=jax.ShapeDtypeStruct((rows, D_MODEL), F32),
        compiler_params=_cparams(("arbitrary",)),
        name="mix_ln1",
    )(pooled, a, x, w_pool, pool_scale, w_out, g1, b1)


PA_TQ = 256
PA_GROUP = 1
TF = 512
TF_SUB = 256
N_FF_CHUNKS = D_FF // TF
CONV_HALO = 8


def _silu(x):
    return x * (0.5 * jnp.tanh(0.5 * x) + 0.5)


def _ffn_accumulate(gate, val, wd_ref, x1_ref, g_ref, b_ref, o_ref, acc_ref):
    c = pl.program_id(1)
    h = (_silu(gate) * val).astype(BF16)
    part = jnp.dot(h, wd_ref[...], preferred_element_type=F32)

    @pl.when(c == 0)
    def _():
        acc_ref[...] = part

    @pl.when(c > 0)
    def _():
        acc_ref[...] += part

    @pl.when(c == N_FF_CHUNKS - 1)
    def _():
        o_ref[...] = _layer_norm(DEEPNORM_ALPHA * x1_ref[...] + acc_ref[...], g_ref, b_ref)


def _ffn_fused_kernel(x1_ref, wg_ref, wv_ref, fg_ref, fv_ref, cwg_ref, cwv_ref, cbg_ref, cbv_ref,
                      wd_ref, g_ref, b_ref, o_ref, tg_ref, tv_ref, xb_scr, carry_g, carry_v, ext_scr,
                      *, tiles_per_batch):
    i = pl.program_id(0)
    c = pl.program_id(1)
    tm = x1_ref.shape[0]
    first = i % tiles_per_batch == 0

    @pl.when(c == 0)
    def _():
        xb_scr[...] = x1_ref[...].astype(BF16)
        o_ref[...] = jnp.zeros(o_ref.shape, F32)

    @pl.when(i == 0)
    def _():
        carry_g[c] = jnp.zeros((CONV_HALO, TF), F32)
        carry_v[c] = jnp.zeros((CONV_HALO, TF), F32)

    part = None
    for s in range(TF // TF_SUB):
        cols = slice(s * TF_SUB, (s + 1) * TF_SUB)

        def conv(w_ref, f_ref, carry, tail_ref, cw_ref, cb_ref, ext):
            u = jnp.dot(xb_scr[...], w_ref[:, cols], preferred_element_type=F32)
            ext[0:CONV_HALO, :] = jnp.where(first, f_ref[:, cols], carry[c, :, cols])
            ext[CONV_HALO:, :] = u
            tail = u[tm - CONV_HALO:, :]
            carry[c, :, cols] = tail
            tail_ref[0, :, cols] = tail
            return (cb_ref[:, cols] + cw_ref[0:1, cols] * ext[CONV_HALO - 2:CONV_HALO - 2 + tm, :]
                    + cw_ref[1:2, cols] * ext[CONV_HALO - 1:CONV_HALO - 1 + tm, :]
                    + cw_ref[2:3, cols] * u)

        gate = conv(wg_ref, fg_ref, carry_g, tg_ref, cwg_ref, cbg_ref, ext_scr.at[2 * s])
        val = conv(wv_ref, fv_ref, carry_v, tv_ref, cwv_ref, cbv_ref, ext_scr.at[2 * s + 1])
        h = (_silu(gate) * val).astype(BF16)
        d = jnp.dot(h, wd_ref[cols, :], preferred_element_type=F32)
        part = d if part is None else part + d
    o_ref[...] += part

    @pl.when(c == N_FF_CHUNKS - 1)
    def _():
        o_ref[...] = _layer_norm(DEEPNORM_ALPHA * x1_ref[...] + o_ref[...], g_ref, b_ref)


def _ffn_prev_kernel(ug_ref, uv_ref, p1g_ref, p1v_ref, p2g_ref, p2v_ref, cwg_ref, cwv_ref, cbg_ref, cbv_ref,
                     wd_ref, x1_ref, g_ref, b_ref, o_ref, acc_ref):
    def conv(u_ref, p1_ref, p2_ref, cw_ref, cb_ref):
        return (cb_ref[...] + cw_ref[0:1, :] * p2_ref[...] + cw_ref[1:2, :] * p1_ref[...]
                + cw_ref[2:3, :] * u_ref[...])

    gate = conv(ug_ref, p1g_ref, p2g_ref, cwg_ref, cbg_ref)
    val = conv(uv_ref, p1v_ref, p2v_ref, cwv_ref, cbv_ref)
    _ffn_accumulate(gate, val, wd_ref, x1_ref, g_ref, b_ref, o_ref, acc_ref)


def _ffn_common_specs(tm):
    gate_c = lambda i, c: (0, c)
    val_c = lambda i, c: (0, N_FF_CHUNKS + c)
    return [pl.BlockSpec((3, TF), gate_c), pl.BlockSpec((3, TF), val_c),
            pl.BlockSpec((1, TF), gate_c), pl.BlockSpec((1, TF), val_c),
            pl.BlockSpec((TF, D_MODEL), lambda i, c: (c, 0)),
            pl.BlockSpec((tm, D_MODEL), lambda i, c: (i, 0)),
            pl.BlockSpec((1, D_MODEL), lambda i, c: (0, 0)),
            pl.BlockSpec((1, D_MODEL), lambda i, c: (0, 0))]


def _ffn_fused(x1, w_up, u_first, first_row_block, conv_w, conv_b, w_down, g2, b2, *, tm, tiles_per_batch):
    rows = x1.shape[0]
    kern = functools.partial(_ffn_fused_kernel, tiles_per_batch=tiles_per_batch)
    gate_c = lambda i, c: (0, c)
    val_c = lambda i, c: (0, N_FF_CHUNKS + c)
    const = lambda i, c: (0, 0)
    tail_spec = pl.BlockSpec((1, CONV_HALO, TF), lambda i, c: (i, 0, c))
    tail_shape = jax.ShapeDtypeStruct((rows // tm, CONV_HALO, D_FF), F32)
    y, tail_g, tail_v = pl.pallas_call(
        kern,
        grid=(rows // tm, N_FF_CHUNKS),
        in_specs=[pl.BlockSpec((tm, D_MODEL), lambda i, c: (i, 0), pipeline_mode=pl.Buffered(1)),
                  pl.BlockSpec((D_MODEL, TF), gate_c), pl.BlockSpec((D_MODEL, TF), val_c),
                  pl.BlockSpec((CONV_HALO, TF), lambda i, c: (first_row_block, c)),
                  pl.BlockSpec((CONV_HALO, TF), lambda i, c: (first_row_block, N_FF_CHUNKS + c)),
                  pl.BlockSpec((3, TF), gate_c), pl.BlockSpec((3, TF), val_c),
                  pl.BlockSpec((1, TF), gate_c), pl.BlockSpec((1, TF), val_c),
                  pl.BlockSpec((TF, D_MODEL), lambda i, c: (c, 0)),
                  pl.BlockSpec((1, D_MODEL), const), pl.BlockSpec((1, D_MODEL), const)],
        out_specs=[pl.BlockSpec((tm, D_MODEL), lambda i, c: (i, 0)), tail_spec, tail_spec],
        out_shape=[jax.ShapeDtypeStruct((rows, D_MODEL), F32), tail_shape, tail_shape],
        scratch_shapes=[pltpu.VMEM((tm, D_MODEL), BF16),
                        pltpu.VMEM((N_FF_CHUNKS, CONV_HALO, TF), F32),
                        pltpu.VMEM((N_FF_CHUNKS, CONV_HALO, TF), F32),
                        pltpu.VMEM((2 * (TF // TF_SUB), tm + CONV_HALO, TF_SUB), F32)],
        compiler_params=_cparams(("arbitrary", "arbitrary"), vmem_limit=FFN_VMEM_LIMIT),
        name="ffn",
    )(x1, w_up, w_up, u_first, u_first, conv_w, conv_w, conv_b, conv_b, w_down, g2, b2)
    last = slice(tiles_per_batch - 1, None, tiles_per_batch)
    return y, tail_g[last], tail_v[last]


def _ffn_tail_prev(u, prev1, prev2, conv_w, conv_b, w_down, x1, g2, b2, *, tm):
    rows = u.shape[0]
    gate_t = lambda i, c: (i, c)
    val_t = lambda i, c: (i, N_FF_CHUNKS + c)
    return pl.pallas_call(
        _ffn_prev_kernel,
        grid=(rows // tm, N_FF_CHUNKS),
        in_specs=[pl.BlockSpec((tm, TF), gate_t), pl.BlockSpec((tm, TF), val_t),
                  pl.BlockSpec((tm, TF), gate_t), pl.BlockSpec((tm, TF), val_t),
                  pl.BlockSpec((tm, TF), gate_t), pl.BlockSpec((tm, TF), val_t)]
                 + _ffn_common_specs(tm),
        out_specs=pl.BlockSpec((tm, D_MODEL), lambda i, c: (i, 0)),
        out_shape=jax.ShapeDtypeStruct((rows, D_MODEL), F32),
        scratch_shapes=[pltpu.VMEM((tm, D_MODEL), F32)],
        compiler_params=_cparams(("arbitrary", "arbitrary")),
        name="ffn_tail_small",
    )(u, u, prev1, prev1, prev2, prev2, conv_w, conv_w, conv_b, conv_b, w_down, x1, g2, b2)


def kernel(x_prompt, x_sample, cache_k, cache_v, cache_idx_k, state_pool, state_conv, page_table, meta_tokens,
           w_in, w_pool, pool_scale, w_out, ln1_g, ln1_b, w_up, conv_w, conv_b, w_down, ln2_g, ln2_b):
    n_phys = cache_k.shape[1]

    wq, wk, wv, wiq, wik, wiw, wp = jnp.split(w_in[0], IN_OFFSETS, axis=1)
    zcols = lambda n: jnp.zeros((D_MODEL, n), F32)
    w_all = jnp.concatenate([wq, wp, wk, wv, wiq, wik, zcols(64), wiw, zcols(120)], axis=1).astype(BF16)
    w_pool_b = w_pool[0].astype(BF16)
    w_out_b = w_out[0].astype(BF16)
    w_up_b = w_up[0].astype(BF16)
    w_down_b = w_down[0].astype(BF16)
    row2 = lambda a: a.reshape(1, -1)

    pos_prompt = N_META + np.arange(SEQ)
    pos_small = np.concatenate([np.tile(PAST_LEN + np.arange(DEC_SEQ), DEC_BATCH), np.arange(N_META),
                                np.zeros((SMALL_ROWS - N_SAMPLE_ROWS - N_META,), np.int64)])
    tabs_p = _rope_tables(pos_prompt, ROT_DIM, HEAD_DIM) + _rope_tables(pos_prompt, IDX_ROT_DIM, IDX_DIM)
    tabs_s = _rope_tables(pos_small, ROT_DIM, HEAD_DIM) + _rope_tables(pos_small, IDX_ROT_DIM, IDX_DIM)

    xp = x_prompt.reshape(N_PROMPT_ROWS, D_MODEL)
    xs = jnp.concatenate([x_sample.reshape(N_SAMPLE_ROWS, D_MODEL), meta_tokens.astype(F32),
                          jnp.zeros((SMALL_ROWS - N_SAMPLE_ROWS - N_META, D_MODEL), F32)], axis=0)

    tm_p = 1024
    hp = _project(xp, w_all, tabs_p, SEQ // tm_p, tm_p)
    hs = _project(xs, w_all, tabs_s, 1, SMALL_ROWS)

    meta = slice(META_ROW0, META_ROW0 + N_META)

    def with_meta(c0, width):
        m = jnp.broadcast_to(hs[meta, c0:c0 + width][None], (BATCH, N_META, width))
        return jnp.concatenate([m, hp[:, c0:c0 + width].reshape(BATCH, SEQ, width)], axis=1)

    k_prompt = with_meta(C_K, KV_WIDTH)
    v_prompt = with_meta(C_V, KV_WIDTH)
    ik_prompt = with_meta(C_IK, IDX_DIM)
    pad_keys = lambda a: jnp.pad(a.astype(BF16), ((0, 0), (0, NK_PROMPT - T_PROMPT), (0, 0)))
    kb, vb, ikb = pad_keys(k_prompt), pad_keys(v_prompt), pad_keys(ik_prompt)

    blocks_per_batch = SEQ // PA_TQ
    a_groups = []
    for j0 in range(0, blocks_per_batch, PA_GROUP):
        nk = N_META + (j0 + PA_GROUP) * PA_TQ
        nk = -(-nk // LANES) * LANES
        a_groups.append(_prompt_attention(hp, kb, vb, ikb, tq=PA_TQ, n_batch=BATCH, n_blocks=PA_GROUP,
                                          row_block0=j0, blocks_per_batch=blocks_per_batch,
                                          pos_first=N_META + j0 * PA_TQ, nk=nk))
    a_p = jnp.concatenate(a_groups, axis=1).reshape(N_PROMPT_ROWS, ATTN_WIDTH)
    a_m = _prompt_attention(hs, kb, vb, ikb, tq=LANES, n_batch=1, n_blocks=1, row_block0=META_ROW0 // LANES,
                            blocks_per_batch=1, pos_first=0, nk=LANES)[0]

    hs_s = hs[:N_SAMPLE_ROWS]
    iq_s = hs_s[:, C_IQ:C_IQ + N_IDX_HEADS * IDX_DIM].reshape(SS_STEPS, SS_PAIR, DEC_SEQ, N_IDX_HEADS, IDX_DIM)
    iq_s = iq_s.transpose(0, 1, 3, 2, 4)
    iq2 = jnp.stack([jnp.pad(iq_s[:, e], ((0, 0), (0, 0), (DEC_SEQ * e, DEC_SEQ * (SS_PAIR - 1 - e)), (0, 0)))
                     for e in range(SS_PAIR)], axis=1).reshape(SS_STEPS, SS_PAIR, N_IDX_HEADS * 8, IDX_DIM)
    w_s = hs_s[:, C_IW:C_IW + N_IDX_HEADS].reshape(SS_STEPS, SS_PAIR * DEC_SEQ, N_IDX_HEADS)
    w2 = w_s.transpose(0, 2, 1).reshape(SS_STEPS, N_IDX_HEADS * 8, 1)
    ik_new_t = jnp.pad(hs_s[:, C_IK:C_IK + IDX_DIM].reshape(DEC_BATCH, DEC_SEQ, IDX_DIM).transpose(0, 2, 1),
                       ((0, 0), (0, 0), (0, LANES - DEC_SEQ))).reshape(SS_STEPS, SS_PAIR, IDX_DIM, LANES)

    pt_pad = jnp.pad(page_table, ((0, 0), (0, N_MROWS - N_PAGES)))
    pt_hi = jnp.broadcast_to((pt_pad // 64).astype(F32)[:, None, :], (DEC_BATCH, 8, N_MROWS))
    pt_lo = jnp.broadcast_to((pt_pad % 64).astype(F32)[:, None, :], (DEC_BATCH, 8, N_MROWS))
    rows_sel, gbias, nbias = _sample_select(page_table, iq2, w2, ik_new_t, pt_hi, pt_lo,
                                            jnp.swapaxes(cache_idx_k[0], 1, 2))

    def new_rows(c0):
        a = hs_s[:, c0:c0 + KV_WIDTH].reshape(DEC_BATCH, DEC_SEQ * N_KV_HEADS, HEAD_DIM)
        return jnp.pad(a, ((0, 0), (0, LANES - DEC_SEQ * N_KV_HEADS), (0, 0)))

    a_s4 = _sample_attention(rows_sel.reshape(-1), gbias.reshape(DEC_BATCH, DEC_SEQ, SA_COLS),
                             nbias.reshape(DEC_BATCH, DEC_SEQ, LANES),
                             hs_s[:, C_Q:C_Q + ATTN_WIDTH].reshape(DEC_BATCH, DEC_SEQ, N_HEADS, HEAD_DIM),
                             new_rows(C_K), new_rows(C_V),
                             cache_k[0].reshape(n_phys * PAGE_SIZE * N_KV_HEADS, HEAD_DIM),
                             cache_v[0].reshape(n_phys * PAGE_SIZE * N_KV_HEADS, HEAD_DIM))
    a_s = a_s4.reshape(N_SAMPLE_ROWS, ATTN_WIDTH).astype(BF16)
    a_small = jnp.concatenate([a_s, a_m[:N_META], jnp.zeros((SMALL_ROWS - N_SAMPLE_ROWS - N_META, ATTN_WIDTH),
                                                            BF16)], axis=0)

    win = jnp.repeat(jnp.asarray(POOL_WINDOWS, F32), POOL_GROUP)
    p_meta = hs[meta, C_P:C_P + POOL_WIDTH]
    tm_pool = 256
    pooled_p = _pool(hp, C_P // POOL_WIDTH, hp, hs, (META_ROW0 // N_META, C_P // POOL_WIDTH), row2(win),
                     rows=N_PROMPT_ROWS, tm=tm_pool, tiles_per_batch=SEQ // tm_pool)
    p_s = hs_s[:, C_P:C_P + POOL_WIDTH].reshape(DEC_BATCH, DEC_SEQ, POOL_WIDTH)
    grp = 24
    ext_s = jnp.concatenate([jnp.zeros((DEC_BATCH, 1, POOL_WIDTH), F32), state_pool[0], p_s,
                             jnp.zeros((DEC_BATCH, grp - 1 - POOL_CTX - DEC_SEQ, POOL_WIDTH), F32)], axis=1)
    ext_small = jnp.concatenate([ext_s.reshape(DEC_BATCH * grp, POOL_WIDTH),
                                 jnp.zeros((N_META, POOL_WIDTH), F32), p_meta], axis=0)
    n_ext = DEC_BATCH * grp + 2 * N_META
    cnt_meta = jnp.minimum(win[None, :], (jnp.arange(N_META, dtype=F32) + 1.0)[:, None])
    cnt_small = jnp.concatenate([jnp.broadcast_to(win[None], (n_ext - N_META, POOL_WIDTH)), cnt_meta], axis=0)
    zeros_halo = jnp.zeros((N_META, POOL_WIDTH), F32)
    pooled_ext = _pool(ext_small, 0, zeros_halo, zeros_halo, (0, 0), cnt_small,
                       rows=n_ext, tm=n_ext, tiles_per_batch=1)
    pooled_small = jnp.concatenate(
        [pooled_ext[:DEC_BATCH * grp].reshape(DEC_BATCH, grp, POOL_WIDTH)[:, 16:16 + DEC_SEQ].reshape(
            N_SAMPLE_ROWS, POOL_WIDTH),
         pooled_ext[n_ext - N_META:],
         jnp.zeros((SMALL_ROWS - N_SAMPLE_ROWS - N_META, POOL_WIDTH), BF16)], axis=0)

    mix_args = (w_pool_b, row2(pool_scale[0]), w_out_b, row2(ln1_g[0]), row2(ln1_b[0]))
    x1_p = _mix(pooled_p, a_p, xp, *mix_args, tm=256)
    x1_s = _mix(pooled_small, a_small, xs, *mix_args, tm=SMALL_ROWS)

    u_s = _matmul(x1_s.astype(BF16), w_up_b, SMALL_ROWS, "ffn_up_small")
    ffn_args = (conv_w[0], row2(conv_b[0]), w_down_b)
    ln2 = (row2(ln2_g[0]), row2(ln2_b[0]))
    tm_f = 1024
    y_p, tail_g, tail_v = _ffn_fused(x1_p, w_up_b, u_s, (META_ROW0 + N_META) // CONV_HALO - 1, *ffn_args, *ln2,
                                     tm=tm_f, tiles_per_batch=SEQ // tm_f)
    u_s3 = u_s[:N_SAMPLE_ROWS].reshape(DEC_BATCH, DEC_SEQ, 2 * D_FF)
    ext_u = jnp.concatenate([state_conv[0], u_s3], axis=1)
    u_m = u_s[meta]
    ext_m = jnp.concatenate([jnp.zeros((2, 2 * D_FF), F32), u_m], axis=0)
    tail0 = jnp.zeros((SMALL_ROWS - N_SAMPLE_ROWS - N_META, 2 * D_FF), F32)
    prev1 = jnp.concatenate([ext_u[:, 1:1 + DEC_SEQ].reshape(N_SAMPLE_ROWS, -1), ext_m[1:1 + N_META], tail0], 0)
    prev2 = jnp.concatenate([ext_u[:, 0:DEC_SEQ].reshape(N_SAMPLE_ROWS, -1), ext_m[0:N_META], tail0], 0)
    y_s = _ffn_tail_prev(u_s, prev1, prev2, *ffn_args, x1_s, *ln2, tm=SMALL_ROWS)

    y_prompt = y_p.reshape(BATCH, SEQ, D_MODEL)
    y_sample = y_s[:N_SAMPLE_ROWS].reshape(DEC_BATCH, DEC_SEQ, D_MODEL)
    pool_prompt = hp.reshape(BATCH, SEQ, PROJ_WIDTH)[:, SEQ - POOL_CTX:, C_P:C_P + POOL_WIDTH]
    conv_prompt = jnp.concatenate([tail_g[:, CONV_HALO - 2:], tail_v[:, CONV_HALO - 2:]], axis=-1)
    k_sample = hs_s[:, C_K:C_K + KV_WIDTH].reshape(DEC_BATCH, DEC_SEQ, N_KV_HEADS, HEAD_DIM)
    v_sample = hs_s[:, C_V:C_V + KV_WIDTH].reshape(DEC_BATCH, DEC_SEQ, N_KV_HEADS, HEAD_DIM)
    ik_sample = hs_s[:, C_IK:C_IK + IDX_DIM].reshape(DEC_BATCH, DEC_SEQ, IDX_DIM)
    pool_sample = jnp.concatenate([state_pool[0], p_s], axis=1)[:, DEC_SEQ:]
    conv_sample = ext_u[:, DEC_SEQ:]
    return (y_prompt, y_sample,
            k_prompt.reshape(1, BATCH, T_PROMPT, N_KV_HEADS, HEAD_DIM),
            v_prompt.reshape(1, BATCH, T_PROMPT, N_KV_HEADS, HEAD_DIM),
            ik_prompt[None], pool_prompt[None], conv_prompt[None],
            k_sample[None], v_sample[None], ik_sample[None], pool_sample[None], conv_sample[None])
```

```python
import functools

import numpy as np
import jax
import jax.numpy as jnp
from jax import lax
from jax.experimental import pallas as pl
from jax.experimental.pallas import tpu as pltpu

F32 = jnp.float32
BF16 = jnp.bfloat16
I32 = jnp.int32

D_MODEL = 2048
BATCH = 8
SEQ = 2048
DEC_BATCH = 32
DEC_SEQ = 4
PAST_LEN = 16384
PAGE_SIZE = 128
N_META = 16
ATTN_WIDTH = 1024
N_HEADS = 8
HEAD_DIM = 128
N_KV_HEADS = 4
KV_WIDTH = N_KV_HEADS * HEAD_DIM
ROT_DIM = 32
N_IDX_HEADS = 8
IDX_DIM = 64
IDX_ROT_DIM = 16
TOPK = 256
POOL_WIDTH = 1024
POOL_WINDOWS = (2, 4, 8, 16)
POOL_GROUP = 256
POOL_CTX = 15
D_FF = 5632
ROPE_THETA = 500000.0
LN_EPS = 1e-5
DEEPNORM_ALPHA = 2.0 ** 0.25
IN_OFFSETS = (1024, 1536, 2048, 2560, 2624, 2632)

LANES = 128
N_PROMPT_ROWS = BATCH * SEQ
N_SAMPLE_ROWS = DEC_BATCH * DEC_SEQ
SMALL_ROWS = 256
META_ROW0 = N_SAMPLE_ROWS
T_PROMPT = N_META + SEQ
NK_PROMPT = 2176
NK_SAMPLE = PAST_LEN + LANES
N_PAGES = PAST_LEN // PAGE_SIZE

C_Q, C_P, C_K, C_V, C_IQ, C_IK, C_IW = 0, 1024, 2048, 2560, 3072, 3584, 3712
PROJ_WIDTH = 3840
TN = 256
ROPE_A_TILES = (0, 1, 2, 3, 8, 9)
ROPE_B_TILES = (12, 13)
ROPE_BH_TILE = 14

VMEM_LIMIT = 56 * 1024 * 1024
FFN_VMEM_LIMIT = 62 * 1024 * 1024
INT_MIN = -2 ** 31
NEG_INF = float("-inf")
LOG2_E = 1.4426950408889634


def _cparams(sem, vmem_limit=VMEM_LIMIT, **kw):
    return pltpu.CompilerParams(dimension_semantics=sem, vmem_limit_bytes=vmem_limit, **kw)


def _rope_tables(pos, rot_dim, head_dim):
    half = rot_dim // 2
    inv_freq = ROPE_THETA ** (-np.arange(half, dtype=np.float64) / half)
    ang = np.asarray(pos, np.float64)[:, None] * inv_freq[None, :]
    cos, sin = np.cos(ang), np.sin(ang)
    rows = len(pos)
    zh = np.zeros((rows, half))
    rest0 = np.zeros((rows, head_dim - rot_dim))
    c = np.concatenate([cos, cos, rest0 + 1.0], axis=1)
    s1 = np.concatenate([-sin, zh, rest0], axis=1)
    s2 = np.concatenate([zh, sin, rest0], axis=1)
    reps = LANES // head_dim
    return tuple(jnp.asarray(np.tile(t, (1, reps)), F32) for t in (c, s1, s2))


def _rope(x, c_ref, s1_ref, s2_ref, half):
    return (x * c_ref[...] + pltpu.roll(x, LANES - half, 1) * s1_ref[...]
            + pltpu.roll(x, half, 1) * s2_ref[...])


def _proj_kernel(x_ref, w_ref, ca, sa1, sa2, cb, sb1, sb2, o_ref, xb_ref):
    j = pl.program_id(1)

    @pl.when(j == 0)
    def _():
        xb_ref[...] = x_ref[...].astype(BF16)

    acc = jnp.dot(xb_ref[...], w_ref[...], preferred_element_type=F32)
    is_a = functools.reduce(jnp.logical_or, [j == t for t in ROPE_A_TILES])
    is_b = functools.reduce(jnp.logical_or, [j == t for t in ROPE_B_TILES])
    is_bh = j == ROPE_BH_TILE

    @pl.when(is_a)
    def _():
        o_ref[:, :LANES] = _rope(acc[:, :LANES], ca, sa1, sa2, ROT_DIM // 2)
        o_ref[:, LANES:] = _rope(acc[:, LANES:], ca, sa1, sa2, ROT_DIM // 2)

    @pl.when(is_b)
    def _():
        o_ref[:, :LANES] = _rope(acc[:, :LANES], cb, sb1, sb2, IDX_ROT_DIM // 2)
        o_ref[:, LANES:] = _rope(acc[:, LANES:], cb, sb1, sb2, IDX_ROT_DIM // 2)

    @pl.when(is_bh)
    def _():
        o_ref[:, :LANES] = _rope(acc[:, :LANES], cb, sb1, sb2, IDX_ROT_DIM // 2)
        o_ref[:, LANES:] = acc[:, LANES:]

    @pl.when(jnp.logical_not(is_a | is_b | is_bh))
    def _():
        o_ref[...] = acc


def _project(x, w_all, tabs, tab_blocks, tm):
    rows = x.shape[0]
    tab_spec = pl.BlockSpec((tm, LANES), lambda i, j: (i % tab_blocks, 0))
    return pl.pallas_call(
        _proj_kernel,
        grid=(rows // tm, PROJ_WIDTH // TN),
        in_specs=[pl.BlockSpec((tm, D_MODEL), lambda i, j: (i, 0)),
                  pl.BlockSpec((D_MODEL, TN), lambda i, j: (0, j))] + [tab_spec] * 6,
        out_specs=pl.BlockSpec((tm, TN), lambda i, j: (i, j)),
        out_shape=jax.ShapeDtypeStruct((rows, PROJ_WIDTH), F32),
        scratch_shapes=[pltpu.VMEM((tm, D_MODEL), BF16)],
        compiler_params=_cparams(("arbitrary", "arbitrary")),
        name="in_proj",
    )(x, w_all, *tabs)


def _mm_kernel(x_ref, w_ref, o_ref):
    o_ref[...] = jnp.dot(x_ref[...], w_ref[...], preferred_element_type=F32)


def _matmul(x, w, tm, name):
    rows, kdim = x.shape
    n = w.shape[1]
    return pl.pallas_call(
        _mm_kernel,
        grid=(rows // tm, n // TN),
        in_specs=[pl.BlockSpec((tm, kdim), lambda i, j: (i, 0)),
                  pl.BlockSpec((kdim, TN), lambda i, j: (0, j))],
        out_specs=pl.BlockSpec((tm, TN), lambda i, j: (i, j)),
        out_shape=jax.ShapeDtypeStruct((rows, n), F32),
        compiler_params=_cparams(("arbitrary", "arbitrary")),
        name=name,
    )(x, w)


def _ordinal_to_float(key):
    bits = key ^ ((key >> 31) & 0x7FFFFFFF)
    return lax.bitcast_convert_type(bits, F32)


def _select_topk(sc_ref, col, valid, n_col_bits, cut_scr):
    rows = sc_ref.shape[0]
    k_f = float(TOPK)

    def count(pred):
        return jnp.sum(jnp.where(pred, 1.0, 0.0), axis=1, keepdims=True)

    few = count(valid) <= k_f
    key0 = jnp.where(count(sc_ref[...] >= 0.0) >= k_f, 0, INT_MIN).astype(I32)

    def value_step(i, key):
        cand = key + jnp.left_shift(jnp.int32(1), 30 - i)
        cnt = count(sc_ref[...] >= _ordinal_to_float(cand))
        return jnp.where(cnt >= k_f, cand, key)

    key = lax.fori_loop(0, 31, value_step, key0)
    thr = jnp.where(few, NEG_INF, _ordinal_to_float(key))
    need = k_f - count(sc_ref[...] > thr)
    surplus = jnp.where(few, 0.0, count(sc_ref[...] >= thr) - k_f)

    cut_scr[...] = jnp.full(cut_scr.shape, 2 ** n_col_bits - 1, I32)

    @pl.when(jnp.max(surplus) > 0.0)
    def _():
        def col_step(i, cut):
            cand = cut + jnp.left_shift(jnp.int32(1), n_col_bits - 1 - i)
            cnt = count((sc_ref[...] == thr) & (col < cand))
            return jnp.where(cnt < need, cand, cut)

        cut_scr[...] = lax.fori_loop(0, n_col_bits, col_step, jnp.zeros((rows, 1), I32))

    sc = sc_ref[...]
    return valid & ((sc > thr) | ((sc == thr) & (col <= cut_scr[...])))


def _select_topk_tiled(sc_ref, valid_fn, n_col_bits, cut_scr, out_ref):
    rows, n = sc_ref.shape
    k_f = float(TOPK)
    lane = lax.broadcasted_iota(I32, (rows, LANES), 1)
    tiles = [(t * LANES, slice(t * LANES, (t + 1) * LANES)) for t in range(n // LANES)]
    wide = lambda v: jnp.broadcast_to(v, (rows, LANES))

    def count(pred_fn):
        acc = jnp.zeros((rows, LANES), F32)
        for c0, cols in tiles:
            acc = acc + jnp.where(pred_fn(sc_ref[:, cols], lane + c0), 1.0, 0.0)
        return jnp.sum(acc, axis=1, keepdims=True)

    few = count(lambda sc, col: valid_fn(col)) <= k_f
    key0 = jnp.where(count(lambda sc, col: sc >= 0.0) >= k_f, 0, INT_MIN).astype(I32)

    def value_step(i, key):
        cand = wide(_ordinal_to_float(key + jnp.left_shift(jnp.int32(1), 30 - i)))
        cnt = count(lambda sc, col: sc >= cand)
        return jnp.where(cnt >= k_f, key + jnp.left_shift(jnp.int32(1), 30 - i), key)

    key = lax.fori_loop(0, 31, value_step, key0)
    thr = jnp.where(few, NEG_INF, _ordinal_to_float(key))
    thr_w = wide(thr)
    need = k_f - count(lambda sc, col: sc > thr_w)
    surplus = jnp.where(few, 0.0, count(lambda sc, col: sc >= thr_w) - k_f)

    cut_scr[...] = jnp.full(cut_scr.shape, 2 ** n_col_bits - 1, I32)

    @pl.when(jnp.max(surplus) > 0.0)
    def _():
        def col_step(i, cut):
            cand = cut + jnp.left_shift(jnp.int32(1), n_col_bits - 1 - i)
            cand_w = wide(cand)
            cnt = count(lambda sc, col: (sc == thr_w) & (col < cand_w))
            return jnp.where(cnt < need, cand, cut)

        cut_scr[...] = lax.fori_loop(0, n_col_bits, col_step, jnp.zeros((rows, 1), I32))

    cut_w = wide(cut_scr[...])
    for c0, cols in tiles:
        sc, col = sc_ref[:, cols], lane + c0
        keep = valid_fn(col) & ((sc > thr_w) | ((sc == thr_w) & (col <= cut_w)))
        out_ref[:, cols] = jnp.where(keep, 1.0, 0.0)


def _pattn_kernel(q_ref, iq_ref, misc_ref, kb_ref, vb_ref, ikb_ref, o_ref, sc_scr, bias_scr, cut_scr,
                  *, pos_first, nk):
    tq = q_ref.shape[0]
    pos0 = pos_first + tq * pl.program_id(1)

    ik = ikb_ref[0]
    w = misc_ref[:, LANES:LANES + N_IDX_HEADS] * (N_IDX_HEADS ** -0.5)
    score = jnp.zeros((tq, nk), F32)
    for h in range(N_IDX_HEADS):
        iq_h = (iq_ref[:, h * IDX_DIM:(h + 1) * IDX_DIM] * (IDX_DIM ** -0.5)).astype(BF16)
        s_h = lax.dot_general(iq_h, ik, (((1,), (1,)), ((), ())), preferred_element_type=F32)
        score = score + jnp.maximum(s_h, 0.0) * w[:, h:h + 1]

    qpos = pos0 + lax.broadcasted_iota(I32, (tq, nk), 0)
    col = lax.broadcasted_iota(I32, (tq, nk), 1)
    causal = col <= qpos
    sc_scr[...] = jnp.where(causal, score, NEG_INF)
    sel = _select_topk(sc_scr, col, causal, 12, cut_scr)
    bias_scr[...] = jnp.where(sel, 0.0, NEG_INF)

    q_scale = (HEAD_DIM ** -0.5) * LOG2_E
    for h in range(N_HEADS):
        kv = h // (N_HEADS // N_KV_HEADS)
        q_h = (q_ref[:, h * HEAD_DIM:(h + 1) * HEAD_DIM] * q_scale).astype(BF16)
        k_g = kb_ref[0, :, kv * HEAD_DIM:(kv + 1) * HEAD_DIM]
        s = lax.dot_general(q_h, k_g, (((1,), (1,)), ((), ())), preferred_element_type=F32)
        s = s + bias_scr[...]
        m = jnp.max(s, axis=1, keepdims=True)
        p = jnp.exp2(s - m)
        l = jnp.sum(p, axis=1, keepdims=True)
        v_g = vb_ref[0, :, kv * HEAD_DIM:(kv + 1) * HEAD_DIM]
        o = jnp.dot(p.astype(BF16), v_g, preferred_element_type=F32) / l
        o_ref[0, :, h * HEAD_DIM:(h + 1) * HEAD_DIM] = o.astype(o_ref.dtype)


def _prompt_attention(h, kb, vb, ikb, *, tq, n_batch, n_blocks, row_block0, blocks_per_batch, pos_first, nk):
    kern = functools.partial(_pattn_kernel, pos_first=pos_first, nk=nk)
    qrow = lambda b, j: row_block0 + b * blocks_per_batch + j
    return pl.pallas_call(
        kern,
        grid=(n_batch, n_blocks),
        in_specs=[pl.BlockSpec((tq, ATTN_WIDTH), lambda b, j: (qrow(b, j), C_Q // ATTN_WIDTH)),
                  pl.BlockSpec((tq, 512), lambda b, j: (qrow(b, j), C_IQ // 512)),
                  pl.BlockSpec((tq, 256), lambda b, j: (qrow(b, j), C_IK // 256)),
                  pl.BlockSpec((1, nk, KV_WIDTH), lambda b, j: (b, 0, 0)),
                  pl.BlockSpec((1, nk, KV_WIDTH), lambda b, j: (b, 0, 0)),
                  pl.BlockSpec((1, nk, IDX_DIM), lambda b, j: (b, 0, 0))],
        out_specs=pl.BlockSpec((1, tq, ATTN_WIDTH), lambda b, j: (b, j, 0)),
        out_shape=jax.ShapeDtypeStruct((n_batch, n_blocks * tq, ATTN_WIDTH), BF16),
        scratch_shapes=[pltpu.VMEM((tq, nk), F32), pltpu.VMEM((tq, nk), F32),
                        pltpu.VMEM((tq, 1), I32)],
        compiler_params=_cparams(("arbitrary", "arbitrary")),
        name="prompt_attention",
    )(h, h, h, kb, vb, ikb)


KV_SHIFT = 2
N_MROWS = 256


SS_PAIR = 2
SS_STEPS = DEC_BATCH // SS_PAIR
SS_CHUNK = 2048


def _ssel_kernel(pt_ref, iq_ref, w_ref, iknew_ref, pthi_ref, ptlo_ref, cache_ref,
                 rows_ref, gbias_ref, nbias_ref, ikbuf, sem, sc_scr, sel_scr, m_scr, cut_scr):
    s = pl.program_id(0)

    def page_copy(step, slot, e, p):
        return pltpu.make_async_copy(cache_ref.at[pt_ref[SS_PAIR * step + e, p]],
                                     ikbuf.at[slot, e, :, pl.ds(pl.multiple_of(p * PAGE_SIZE, PAGE_SIZE), PAGE_SIZE)],
                                     sem.at[slot])

    def fetch(step, slot):
        def issue(p, carry):
            for e in range(SS_PAIR):
                page_copy(step, slot, e, p).start(priority=e)
            return carry
        lax.fori_loop(0, N_PAGES, issue, 0)

    slot = s % 2

    @pl.when(s == 0)
    def _():
        fetch(0, 0)

    @pl.when(s + 1 < SS_STEPS)
    def _():
        fetch(s + 1, 1 - slot)

    pltpu.make_async_copy(ikbuf.at[slot], ikbuf.at[slot], sem.at[slot]).wait()

    w = w_ref[0] * (N_IDX_HEADS ** -0.5)
    iq = [iq_ref[0, e].astype(BF16) for e in range(SS_PAIR)]

    def score_of(keys_of):
        r = None
        for e in range(SS_PAIR):
            s_e = jnp.dot(iq[e], keys_of(e).astype(BF16), preferred_element_type=F32)
            r_e = jnp.maximum(s_e * (IDX_DIM ** -0.5), 0.0)
            r = r_e if r is None else r + r_e
        r = r * w
        acc = r[0:8]
        for h in range(1, N_IDX_HEADS):
            acc = acc + r[8 * h:8 * h + 8]
        return acc

    row0 = pl.multiple_of(s * 8, 8)
    for c in range(PAST_LEN // SS_CHUNK):
        cols = slice(c * SS_CHUNK, (c + 1) * SS_CHUNK)
        sc_scr[pl.ds(row0, 8), cols] = score_of(lambda e: ikbuf[slot, e, :, cols])
    sc_scr[pl.ds(row0, 8), PAST_LEN:] = score_of(lambda e: iknew_ref[0, e])

    @pl.when(s == SS_STEPS - 1)
    def _():
        _ssel_finish(pthi_ref, ptlo_ref, rows_ref, gbias_ref, nbias_ref, sc_scr, sel_scr, m_scr, cut_scr)


def _ssel_finish(pthi_ref, ptlo_ref, rows_ref, gbias_ref, nbias_ref, sc_scr, sel_scr, m_scr, cut_scr):
    n_q = sc_scr.shape[0]
    qpos = PAST_LEN + (lax.broadcasted_iota(I32, (n_q, LANES), 0) & (DEC_SEQ - 1))
    valid_fn = lambda col: col <= qpos
    lane = lax.broadcasted_iota(I32, (n_q, LANES), 1)
    for t in range(NK_SAMPLE // LANES):
        cols = slice(t * LANES, (t + 1) * LANES)
        sc_scr[:, cols] = jnp.where(valid_fn(lane + t * LANES), sc_scr[:, cols], NEG_INF)
    _select_topk_tiled(sc_scr, valid_fn, 15, cut_scr, sel_scr)

    nt = (((1,), (1,)), ((), ()))
    ones_b = jnp.ones((8, LANES), BF16)
    upper_pages = (lax.broadcasted_iota(I32, (N_MROWS, N_MROWS), 0)
                   < lax.broadcasted_iota(I32, (N_MROWS, N_MROWS), 1)).astype(BF16)
    upper_lanes = (lax.broadcasted_iota(I32, (LANES, LANES), 0)
                   < lax.broadcasted_iota(I32, (LANES, LANES), 1)).astype(BF16)
    lane_id = lax.broadcasted_iota(I32, (8, LANES), 1).astype(BF16)
    page_id = lax.broadcasted_iota(I32, (8, N_MROWS), 1).astype(BF16)
    jcol = lax.broadcasted_iota(I32, (TOPK, 1), 0).astype(F32)
    spread_picks = (lax.broadcasted_iota(I32, (TOPK, N_KV_HEADS * TOPK), 0)
                    == lax.broadcasted_iota(I32, (TOPK, N_KV_HEADS * TOPK), 1) >> KV_SHIFT).astype(BF16)
    spread_new = (lax.broadcasted_iota(I32, (LANES, LANES), 0)
                  == lax.broadcasted_iota(I32, (LANES, LANES), 1) >> KV_SHIFT).astype(BF16)
    m_scr[...] = jnp.zeros(m_scr.shape, F32)

    def compact(k, batch, new_sel):
        pt_hi = pthi_ref[batch].astype(BF16)
        pt_lo = ptlo_ref[batch].astype(BF16)
        m_b = m_scr[k].astype(BF16)
        cnt = lax.dot_general(ones_b, m_b, nt, preferred_element_type=F32)
        start = jnp.dot(cnt.astype(BF16), upper_pages, preferred_element_type=F32)
        cnt1, start1 = cnt[0:1], start[0:1]
        in_page = (start1 <= jcol) & (jcol < start1 + cnt1)
        a_b = jnp.where(in_page, 1.0, 0.0).astype(BF16)
        m_j = jnp.dot(a_b, m_b, preferred_element_type=F32)
        rank = jnp.dot(m_j.astype(BF16), upper_lanes, preferred_element_type=F32)
        start_j = jnp.sum(jnp.where(in_page, start1, 0.0), axis=1, keepdims=True)
        onehot = jnp.where((m_j > 0.5) & (rank == jcol - start_j), 1.0, 0.0).astype(BF16)
        off = lax.dot_general(lane_id, onehot, nt, preferred_element_type=F32)[0:1]
        page = lax.dot_general(page_id, a_b, nt, preferred_element_type=F32)[0:1]
        phys = (lax.dot_general(pt_hi, a_b, nt, preferred_element_type=F32)[0:1] * 64.0
                + lax.dot_general(pt_lo, a_b, nt, preferred_element_type=F32)[0:1])
        in_past = page < float(N_PAGES)
        row = jnp.where(in_past, phys * float(PAGE_SIZE) + off, 0.0).astype(I32)
        past8 = jnp.broadcast_to(jnp.where(in_past, 1.0, 0.0), (8, TOPK)).astype(BF16)
        past4 = jnp.dot(past8, spread_picks, preferred_element_type=F32)[0:1]
        new8 = jnp.broadcast_to(new_sel, (8, LANES)).astype(BF16)
        new4 = jnp.dot(new8, spread_new, preferred_element_type=F32)[0:1]
        return row, jnp.where(past4 > 0.5, 0.0, NEG_INF), jnp.where(new4 > 0.5, 0.0, NEG_INF)

    def compact_group(g, carry):
        r8 = pl.multiple_of(g * 8, 8)
        for c in range(N_PAGES + 1):
            tile = sel_scr[pl.ds(r8, 8), c * LANES:(c + 1) * LANES]
            for k in range(8):
                m_scr[k, c:c + 1, :] = tile[k:k + 1, :]
        new_tile = sel_scr[pl.ds(r8, 8), PAST_LEN:]
        rowid = lax.broadcasted_iota(I32, (8, 1), 0)
        rows_t = jnp.zeros((8, TOPK), I32)
        gb_t = jnp.zeros((8, N_KV_HEADS * TOPK), F32)
        nb_t = jnp.zeros((8, LANES), F32)
        for k in range(8):
            row, gb, nb = compact(k, g * (8 // DEC_SEQ) + k // DEC_SEQ, new_tile[k:k + 1, :])
            rows_t = jnp.where(rowid == k, row, rows_t)
            gb_t = jnp.where(rowid == k, gb, gb_t)
            nb_t = jnp.where(rowid == k, nb, nb_t)
        rows_ref[pl.ds(r8, 8), :] = rows_t
        gbias_ref[pl.ds(r8, 8), :] = gb_t
        nbias_ref[pl.ds(r8, 8), :] = nb_t
        return carry

    lax.fori_loop(0, n_q // 8, compact_group, 0)


def _sample_select(page_table, iq2, w2, ik_new_t, pt_hi, pt_lo, cache_idx_t):
    n_q = N_SAMPLE_ROWS
    whole = lambda shape: pl.BlockSpec(shape, lambda s, pt: (0,) * len(shape))
    grid_spec = pltpu.PrefetchScalarGridSpec(
        num_scalar_prefetch=1,
        grid=(SS_STEPS,),
        in_specs=[pl.BlockSpec((1, SS_PAIR, 64, IDX_DIM), lambda s, pt: (s, 0, 0, 0)),
                  pl.BlockSpec((1, 64, 1), lambda s, pt: (s, 0, 0)),
                  pl.BlockSpec((1, SS_PAIR, IDX_DIM, LANES), lambda s, pt: (s, 0, 0, 0)),
                  whole((DEC_BATCH, 8, N_MROWS)),
                  whole((DEC_BATCH, 8, N_MROWS)),
                  pl.BlockSpec(memory_space=pl.ANY)],
        out_specs=[whole((n_q, TOPK)), whole((n_q, N_KV_HEADS * TOPK)), whole((n_q, LANES))],
        scratch_shapes=[pltpu.VMEM((2, SS_PAIR, IDX_DIM, PAST_LEN), F32),
                        pltpu.SemaphoreType.DMA((2,)),
                        pltpu.VMEM((n_q, NK_SAMPLE), F32),
                        pltpu.VMEM((n_q, NK_SAMPLE), F32),
                        pltpu.VMEM((8, N_MROWS, LANES), F32),
                        pltpu.VMEM((n_q, 1), I32)])
    return pl.pallas_call(
        _ssel_kernel,
        grid_spec=grid_spec,
        out_shape=[jax.ShapeDtypeStruct((n_q, TOPK), I32),
                   jax.ShapeDtypeStruct((n_q, N_KV_HEADS * TOPK), F32),
                   jax.ShapeDtypeStruct((n_q, LANES), F32)],
        compiler_params=_cparams(("arbitrary",)),
        name="sample_select",
    )(page_table, iq2, w2, ik_new_t, pt_hi, pt_lo, cache_idx_t)


SA_COLS = N_KV_HEADS * TOPK


def _sattn_kernel(rows_ref, gbias_ref, nbias_ref, q_ref, knew_ref, vnew_ref, ck_ref, cv_ref, o_ref,
                  kbuf, vbuf, sem):
    b = pl.program_id(0)
    n_slabs = DEC_SEQ * TOPK

    def issue(t, carry):
        row = pl.multiple_of(rows_ref[b * n_slabs + t] * N_KV_HEADS, N_KV_HEADS)
        dst = pl.ds(pl.multiple_of(t * N_KV_HEADS, N_KV_HEADS), N_KV_HEADS)
        pltpu.make_async_copy(ck_ref.at[pl.ds(row, N_KV_HEADS)], kbuf.at[dst], sem.at[0]).start(priority=0)
        pltpu.make_async_copy(cv_ref.at[pl.ds(row, N_KV_HEADS)], vbuf.at[dst], sem.at[1]).start(priority=1)
        return carry

    lax.fori_loop(0, n_slabs, issue, 0, unroll=8)
    pltpu.make_async_copy(kbuf, kbuf, sem.at[0]).wait()
    pltpu.make_async_copy(vbuf, vbuf, sem.at[1]).wait()

    nt = (((1,), (1,)), ((), ()))
    scale = HEAD_DIM ** -0.5
    k_new = knew_ref[0].astype(BF16)
    v_new = vnew_ref[0].astype(BF16)
    heads_per_kv = N_HEADS // N_KV_HEADS

    def own_kv(width):
        head = lax.broadcasted_iota(I32, (N_HEADS, width), 0)
        col = lax.broadcasted_iota(I32, (N_HEADS, width), 1)
        return (col & (N_KV_HEADS - 1)) == (head >> (heads_per_kv.bit_length() - 1))

    for q in range(DEC_SEQ):
        q_h = q_ref[0, q].astype(BF16)
        rows = slice(q * SA_COLS, (q + 1) * SA_COLS)
        s = lax.dot_general(q_h, kbuf[rows, :].astype(BF16), nt, preferred_element_type=F32) * scale
        s = jnp.where(own_kv(SA_COLS), s + gbias_ref[0, q:q + 1, :], NEG_INF)
        sn = lax.dot_general(q_h, k_new, nt, preferred_element_type=F32) * scale
        sn = jnp.where(own_kv(LANES), sn + nbias_ref[0, q:q + 1, :], NEG_INF)
        m = jnp.maximum(jnp.max(s, axis=1, keepdims=True), jnp.max(sn, axis=1, keepdims=True))
        p = jnp.exp(s - m)
        pn = jnp.exp(sn - m)
        l = jnp.sum(p, axis=1, keepdims=True) + jnp.sum(pn, axis=1, keepdims=True)
        o = (jnp.dot(p.astype(BF16), vbuf[rows, :].astype(BF16), preferred_element_type=F32)
             + jnp.dot(pn.astype(BF16), v_new, preferred_element_type=F32)) / l
        o_ref[0, q] = o


def _sample_attention(rows_flat, gbias, nbias, q_s, k_new, v_new, cache_k2d, cache_v2d):
    grid_spec = pltpu.PrefetchScalarGridSpec(
        num_scalar_prefetch=1,
        grid=(DEC_BATCH,),
        in_specs=[pl.BlockSpec((1, DEC_SEQ, SA_COLS), lambda b, r: (b, 0, 0)),
                  pl.BlockSpec((1, DEC_SEQ, LANES), lambda b, r: (b, 0, 0)),
                  pl.BlockSpec((1, DEC_SEQ, N_HEADS, HEAD_DIM), lambda b, r: (b, 0, 0, 0)),
                  pl.BlockSpec((1, LANES, HEAD_DIM), lambda b, r: (b, 0, 0)),
                  pl.BlockSpec((1, LANES, HEAD_DIM), lambda b, r: (b, 0, 0)),
                  pl.BlockSpec(memory_space=pl.ANY),
                  pl.BlockSpec(memory_space=pl.ANY)],
        out_specs=pl.BlockSpec((1, DEC_SEQ, N_HEADS, HEAD_DIM), lambda b, r: (b, 0, 0, 0)),
        scratch_shapes=[pltpu.VMEM((DEC_SEQ * SA_COLS, HEAD_DIM), F32),
                        pltpu.VMEM((DEC_SEQ * SA_COLS, HEAD_DIM), F32),
                        pltpu.SemaphoreType.DMA((2,))])
    return pl.pallas_call(
        _sattn_kernel,
        grid_spec=grid_spec,
        out_shape=jax.ShapeDtypeStruct((DEC_BATCH, DEC_SEQ, N_HEADS, HEAD_DIM), F32),
        compiler_params=_cparams(("arbitrary",), disable_bounds_checks=True),
        name="sample_attention",
    )(rows_flat, gbias, nbias, q_s, k_new, v_new, cache_k2d, cache_v2d)


def _pool_kernel(p_ref, halo_ref, first_ref, cnt_ref, o_ref, ext_scr, *, tiles_per_batch):
    i = pl.program_id(0)
    tm = p_ref.shape[0]
    ext_scr[0:N_META, :] = jnp.where(i % tiles_per_batch == 0, first_ref[...], halo_ref[...])
    ext_scr[N_META:, :] = p_ref[...]
    for g, win in enumerate(POOL_WINDOWS):
        cols = slice(g * POOL_GROUP, (g + 1) * POOL_GROUP)
        x_self = ext_scr[N_META:N_META + tm, cols]
        acc = x_self
        for d in range(1, win):
            acc = acc + ext_scr[N_META - d:N_META - d + tm, cols]
        o_ref[:, cols] = (acc / cnt_ref[:, cols] - x_self).astype(o_ref.dtype)


def _pool(p_src, p_col_block, halo_src, first_src, first_block, cnt, *, rows, tm, tiles_per_batch):
    kern = functools.partial(_pool_kernel, tiles_per_batch=tiles_per_batch)
    halo_per_tile = tm // N_META
    cnt_rows = cnt.shape[0]
    cnt_map = (lambda i: (0, 0)) if cnt_rows == 1 else (lambda i: (i, 0))
    return pl.pallas_call(
        kern,
        grid=(rows // tm,),
        in_specs=[pl.BlockSpec((tm, POOL_WIDTH), lambda i: (i, p_col_block)),
                  pl.BlockSpec((N_META, POOL_WIDTH),
                               lambda i: (jnp.maximum(i * halo_per_tile - 1, 0), p_col_block)),
                  pl.BlockSpec((N_META, POOL_WIDTH), lambda i: first_block),
                  pl.BlockSpec((1 if cnt_rows == 1 else tm, POOL_WIDTH), cnt_map)],
        out_specs=pl.BlockSpec((tm, POOL_WIDTH), lambda i: (i, 0)),
        out_shape=jax.ShapeDtypeStruct((rows, POOL_WIDTH), BF16),
        scratch_shapes=[pltpu.VMEM((tm + N_META, POOL_WIDTH), F32)],
        compiler_params=_cparams(("arbitrary",)),
        name="pool",
    )(p_src, halo_src, first_src, cnt)


def _layer_norm(y, g_ref, b_ref):
    mu = jnp.mean(y, axis=-1, keepdims=True)
    var = jnp.mean(jnp.square(y - mu), axis=-1, keepdims=True)
    return (y - mu) * lax.rsqrt(var + LN_EPS) * g_ref[...] + b_ref[...]


def _mix_kernel(pooled_ref, a_ref, x_ref, wp_ref, ps_ref, wo_ref, g_ref, b_ref, x1_ref):
    parts = []
    for g in range(len(POOL_WINDOWS)):
        cols = slice(g * POOL_GROUP, (g + 1) * POOL_GROUP)
        parts.append(jnp.dot(pooled_ref[:, cols], wp_ref[g], preferred_element_type=F32))
    m = jnp.concatenate(parts, axis=1) * ps_ref[...]
    mix = (jnp.dot(a_ref[...], wo_ref[:ATTN_WIDTH, :], preferred_element_type=F32)
           + jnp.dot(m.astype(BF16), wo_ref[ATTN_WIDTH:, :], preferred_element_type=F32))
    x1_ref[...] = _layer_norm(DEEPNORM_ALPHA * x_ref[...] + mix, g_ref, b_ref)


def _mix(pooled, a, x, w_pool, pool_scale, w_out, g1, b1, tm):
    rows = x.shape[0]
    row = lambda i: (i, 0)
    const2 = lambda i: (0, 0)
    return pl.pallas_call(
        _mix_kernel,
        grid=(rows // tm,),
        in_specs=[pl.BlockSpec((tm, POOL_WIDTH), row),
                  pl.BlockSpec((tm, ATTN_WIDTH), row),
                  pl.BlockSpec((tm, D_MODEL), row),
                  pl.BlockSpec((len(POOL_WINDOWS), POOL_GROUP, POOL_GROUP), lambda i: (0, 0, 0)),
                  pl.BlockSpec((1, POOL_WIDTH), const2),
                  pl.BlockSpec((D_MODEL, D_MODEL), const2),
                  pl.BlockSpec((1, D_MODEL), const2),
                  pl.BlockSpec((1, D_MODEL), const2)],
        out_specs=pl.BlockSpec((tm, D_MODEL), row),
        out_shape=jax.ShapeDtypeStruct((rows, D_MODEL), F32),
        compiler_params=_cparams(("arbitrary",)),
        name="mix_ln1",
    )(pooled, a, x, w_pool, pool_scale, w_out, g1, b1)


PA_TQ = 256
PA_GROUP = 1
TF = 512
TF_SUB = 256
N_FF_CHUNKS = D_FF // TF
CONV_HALO = 8


def _silu(x):
    return x * (0.5 * jnp.tanh(0.5 * x) + 0.5)


def _ffn_accumulate(gate, val, wd_ref, x1_ref, g_ref, b_ref, o_ref, acc_ref):
    c = pl.program_id(1)
    h = (_silu(gate) * val).astype(BF16)
    part = jnp.dot(h, wd_ref[...], preferred_element_type=F32)

    @pl.when(c == 0)
    def _():
        acc_ref[...] = part

    @pl.when(c > 0)
    def _():
        acc_ref[...] += part

    @pl.when(c == N_FF_CHUNKS - 1)
    def _():
        o_ref[...] = _layer_norm(DEEPNORM_ALPHA * x1_ref[...] + acc_ref[...], g_ref, b_ref)


def _ffn_fused_kernel(x1_ref, wg_ref, wv_ref, fg_ref, fv_ref, cwg_ref, cwv_ref, cbg_ref, cbv_ref,
                      wd_ref, g_ref, b_ref, o_ref, tg_ref, tv_ref, xb_scr, carry_g, carry_v, ext_scr,
                      *, tiles_per_batch):
    i = pl.program_id(0)
    c = pl.program_id(1)
    tm = x1_ref.shape[0]
    first = i % tiles_per_batch == 0

    @pl.when(c == 0)
    def _():
        xb_scr[...] = x1_ref[...].astype(BF16)
        o_ref[...] = jnp.zeros(o_ref.shape, F32)

    @pl.when(i == 0)
    def _():
        carry_g[c] = jnp.zeros((CONV_HALO, TF), F32)
        carry_v[c] = jnp.zeros((CONV_HALO, TF), F32)

    part = None
    for s in range(TF // TF_SUB):
        cols = slice(s * TF_SUB, (s + 1) * TF_SUB)

        def conv(w_ref, f_ref, carry, tail_ref, cw_ref, cb_ref, ext):
            u = jnp.dot(xb_scr[...], w_ref[:, cols], preferred_element_type=F32)
            ext[0:CONV_HALO, :] = jnp.where(first, f_ref[:, cols], carry[c, :, cols])
            ext[CONV_HALO:, :] = u
            tail = u[tm - CONV_HALO:, :]
            carry[c, :, cols] = tail
            tail_ref[0, :, cols] = tail
            return (cb_ref[:, cols] + cw_ref[0:1, cols] * ext[CONV_HALO - 2:CONV_HALO - 2 + tm, :]
                    + cw_ref[1:2, cols] * ext[CONV_HALO - 1:CONV_HALO - 1 + tm, :]
                    + cw_ref[2:3, cols] * u)

        gate = conv(wg_ref, fg_ref, carry_g, tg_ref, cwg_ref, cbg_ref, ext_scr.at[2 * s])
        val = conv(wv_ref, fv_ref, carry_v, tv_ref, cwv_ref, cbv_ref, ext_scr.at[2 * s + 1])
        h = (_silu(gate) * val).astype(BF16)
        d = jnp.dot(h, wd_ref[cols, :], preferred_element_type=F32)
        part = d if part is None else part + d
    o_ref[...] += part

    @pl.when(c == N_FF_CHUNKS - 1)
    def _():
        o_ref[...] = _layer_norm(DEEPNORM_ALPHA * x1_ref[...] + o_ref[...], g_ref, b_ref)


def _ffn_prev_kernel(ug_ref, uv_ref, p1g_ref, p1v_ref, p2g_ref, p2v_ref, cwg_ref, cwv_ref, cbg_ref, cbv_ref,
                     wd_ref, x1_ref, g_ref, b_ref, o_ref, acc_ref):
    def conv(u_ref, p1_ref, p2_ref, cw_ref, cb_ref):
        return (cb_ref[...] + cw_ref[0:1, :] * p2_ref[...] + cw_ref[1:2, :] * p1_ref[...]
                + cw_ref[2:3, :] * u_ref[...])

    gate = conv(ug_ref, p1g_ref, p2g_ref, cwg_ref, cbg_ref)
    val = conv(uv_ref, p1v_ref, p2v_ref, cwv_ref, cbv_ref)
    _ffn_accumulate(gate, val, wd_ref, x1_ref, g_ref, b_ref, o_ref, acc_ref)


def _ffn_common_specs(tm):
    gate_c = lambda i, c: (0, c)
    val_c = lambda i, c: (0, N_FF_CHUNKS + c)
    return [pl.BlockSpec((3, TF), gate_c), pl.BlockSpec((3, TF), val_c),
            pl.BlockSpec((1, TF), gate_c), pl.BlockSpec((1, TF), val_c),
            pl.BlockSpec((TF, D_MODEL), lambda i, c: (c, 0)),
            pl.BlockSpec((tm, D_MODEL), lambda i, c: (i, 0)),
            pl.BlockSpec((1, D_MODEL), lambda i, c: (0, 0)),
            pl.BlockSpec((1, D_MODEL), lambda i, c: (0, 0))]


def _ffn_fused(x1, w_up, u_first, first_row_block, conv_w, conv_b, w_down, g2, b2, *, tm, tiles_per_batch):
    rows = x1.shape[0]
    kern = functools.partial(_ffn_fused_kernel, tiles_per_batch=tiles_per_batch)
    gate_c = lambda i, c: (0, c)
    val_c = lambda i, c: (0, N_FF_CHUNKS + c)
    const = lambda i, c: (0, 0)
    tail_spec = pl.BlockSpec((1, CONV_HALO, TF), lambda i, c: (i, 0, c))
    tail_shape = jax.ShapeDtypeStruct((rows // tm, CONV_HALO, D_FF), F32)
    y, tail_g, tail_v = pl.pallas_call(
        kern,
        grid=(rows // tm, N_FF_CHUNKS),
        in_specs=[pl.BlockSpec((tm, D_MODEL), lambda i, c: (i, 0), pipeline_mode=pl.Buffered(1)),
                  pl.BlockSpec((D_MODEL, TF), gate_c), pl.BlockSpec((D_MODEL, TF), val_c),
                  pl.BlockSpec((CONV_HALO, TF), lambda i, c: (first_row_block, c)),
                  pl.BlockSpec((CONV_HALO, TF), lambda i, c: (first_row_block, N_FF_CHUNKS + c)),
                  pl.BlockSpec((3, TF), gate_c), pl.BlockSpec((3, TF), val_c),
                  pl.BlockSpec((1, TF), gate_c), pl.BlockSpec((1, TF), val_c),
                  pl.BlockSpec((TF, D_MODEL), lambda i, c: (c, 0)),
                  pl.BlockSpec((1, D_MODEL), const), pl.BlockSpec((1, D_MODEL), const)],
        out_specs=[pl.BlockSpec((tm, D_MODEL), lambda i, c: (i, 0)), tail_spec, tail_spec],
        out_shape=[jax.ShapeDtypeStruct((rows, D_MODEL), F32), tail_shape, tail_shape],
        scratch_shapes=[pltpu.VMEM((tm, D_MODEL), BF16),
                        pltpu.VMEM((N_FF_CHUNKS, CONV_HALO, TF), F32),
                        pltpu.VMEM((N_FF_CHUNKS, CONV_HALO, TF), F32),
                        pltpu.VMEM((2 * (TF // TF_SUB), tm + CONV_HALO, TF_SUB), F32)],
        compiler_params=_cparams(("arbitrary", "arbitrary"), vmem_limit=FFN_VMEM_LIMIT),
        name="ffn",
    )(x1, w_up, w_up, u_first, u_first, conv_w, conv_w, conv_b, conv_b, w_down, g2, b2)
    last = slice(tiles_per_batch - 1, None, tiles_per_batch)
    return y, tail_g[last], tail_v[last]


def _ffn_tail_prev(u, prev1, prev2, conv_w, conv_b, w_down, x1, g2, b2, *, tm):
    rows = u.shape[0]
    gate_t = lambda i, c: (i, c)
    val_t = lambda i, c: (i, N_FF_CHUNKS + c)
    return pl.pallas_call(
        _ffn_prev_kernel,
        grid=(rows // tm, N_FF_CHUNKS),
        in_specs=[pl.BlockSpec((tm, TF), gate_t), pl.BlockSpec((tm, TF), val_t),
                  pl.BlockSpec((tm, TF), gate_t), pl.BlockSpec((tm, TF), val_t),
                  pl.BlockSpec((tm, TF), gate_t), pl.BlockSpec((tm, TF), val_t)]
                 + _ffn_common_specs(tm),
        out_specs=pl.BlockSpec((tm, D_MODEL), lambda i, c: (i, 0)),
        out_shape=jax.ShapeDtypeStruct((rows, D_MODEL), F32),
        scratch_shapes=[pltpu.VMEM((tm, D_MODEL), F32)],
        compiler_params=_cparams(("arbitrary", "arbitrary")),
        name="ffn_tail_small",
    )(u, u, prev1, prev1, prev2, prev2, conv_w, conv_w, conv_b, conv_b, w_down, x1, g2, b2)


def kernel(x_prompt, x_sample, cache_k, cache_v, cache_idx_k, state_pool, state_conv, page_table, meta_tokens,
           w_in, w_pool, pool_scale, w_out, ln1_g, ln1_b, w_up, conv_w, conv_b, w_down, ln2_g, ln2_b):
    n_phys = cache_k.shape[1]

    wq, wk, wv, wiq, wik, wiw, wp = jnp.split(w_in[0], IN_OFFSETS, axis=1)
    zcols = lambda n: jnp.zeros((D_MODEL, n), F32)
    w_all = jnp.concatenate([wq, wp, wk, wv, wiq, wik, zcols(64), wiw, zcols(120)], axis=1).astype(BF16)
    w_pool_b = w_pool[0].astype(BF16)
    w_out_b = w_out[0].astype(BF16)
    w_up_b = w_up[0].astype(BF16)
    w_down_b = w_down[0].astype(BF16)
    row2 = lambda a: a.reshape(1, -1)

    pos_prompt = N_META + np.arange(SEQ)
    pos_small = np.concatenate([np.tile(PAST_LEN + np.arange(DEC_SEQ), DEC_BATCH), np.arange(N_META),
                                np.zeros((SMALL_ROWS - N_SAMPLE_ROWS - N_META,), np.int64)])
    tabs_p = _rope_tables(pos_prompt, ROT_DIM, HEAD_DIM) + _rope_tables(pos_prompt, IDX_ROT_DIM, IDX_DIM)
    tabs_s = _rope_tables(pos_small, ROT_DIM, HEAD_DIM) + _rope_tables(pos_small, IDX_ROT_DIM, IDX_DIM)

    xp = x_prompt.reshape(N_PROMPT_ROWS, D_MODEL)
    xs = jnp.concatenate([x_sample.reshape(N_SAMPLE_ROWS, D_MODEL), meta_tokens.astype(F32),
                          jnp.zeros((SMALL_ROWS - N_SAMPLE_ROWS - N_META, D_MODEL), F32)], axis=0)

    tm_p = 1024
    hp = _project(xp, w_all, tabs_p, SEQ // tm_p, tm_p)
    hs = _project(xs, w_all, tabs_s, 1, SMALL_ROWS)

    meta = slice(META_ROW0, META_ROW0 + N_META)

    def with_meta(c0, width):
        m = jnp.broadcast_to(hs[meta, c0:c0 + width][None], (BATCH, N_META, width))
        return jnp.concatenate([m, hp[:, c0:c0 + width].reshape(BATCH, SEQ, width)], axis=1)

    k_prompt = with_meta(C_K, KV_WIDTH)
    v_prompt = with_meta(C_V, KV_WIDTH)
    ik_prompt = with_meta(C_IK, IDX_DIM)
    pad_keys = lambda a: jnp.pad(a.astype(BF16), ((0, 0), (0, NK_PROMPT - T_PROMPT), (0, 0)))
    kb, vb, ikb = pad_keys(k_prompt), pad_keys(v_prompt), pad_keys(ik_prompt)

    blocks_per_batch = SEQ // PA_TQ
    a_groups = []
    for j0 in range(0, blocks_per_batch, PA_GROUP):
        nk = N_META + (j0 + PA_GROUP) * PA_TQ
        nk = -(-nk // LANES) * LANES
        a_groups.append(_prompt_attention(hp, kb, vb, ikb, tq=PA_TQ, n_batch=BATCH, n_blocks=PA_GROUP,
                                          row_block0=j0, blocks_per_batch=blocks_per_batch,
                                          pos_first=N_META + j0 * PA_TQ, nk=nk))
    a_p = jnp.concatenate(a_groups, axis=1).reshape(N_PROMPT_ROWS, ATTN_WIDTH)
    a_m = _prompt_attention(hs, kb, vb, ikb, tq=LANES, n_batch=1, n_blocks=1, row_block0=META_ROW0 // LANES,
                            blocks_per_batch=1, pos_first=0, nk=LANES)[0]

    hs_s = hs[:N_SAMPLE_ROWS]
    iq_s = hs_s[:, C_IQ:C_IQ + N_IDX_HEADS * IDX_DIM].reshape(SS_STEPS, SS_PAIR, DEC_SEQ, N_IDX_HEADS, IDX_DIM)
    iq_s = iq_s.transpose(0, 1, 3, 2, 4)
    iq2 = jnp.stack([jnp.pad(iq_s[:, e], ((0, 0), (0, 0), (DEC_SEQ * e, DEC_SEQ * (SS_PAIR - 1 - e)), (0, 0)))
                     for e in range(SS_PAIR)], axis=1).reshape(SS_STEPS, SS_PAIR, N_IDX_HEADS * 8, IDX_DIM)
    w_s = hs_s[:, C_IW:C_IW + N_IDX_HEADS].reshape(SS_STEPS, SS_PAIR * DEC_SEQ, N_IDX_HEADS)
    w2 = w_s.transpose(0, 2, 1).reshape(SS_STEPS, N_IDX_HEADS * 8, 1)
    ik_new_t = jnp.pad(hs_s[:, C_IK:C_IK + IDX_DIM].reshape(DEC_BATCH, DEC_SEQ, IDX_DIM).transpose(0, 2, 1),
                       ((0, 0), (0, 0), (0, LANES - DEC_SEQ))).reshape(SS_STEPS, SS_PAIR, IDX_DIM, LANES)

    pt_pad = jnp.pad(page_table, ((0, 0), (0, N_MROWS - N_PAGES)))
    pt_hi = jnp.broadcast_to((pt_pad // 64).astype(F32)[:, None, :], (DEC_BATCH, 8, N_MROWS))
    pt_lo = jnp.broadcast_to((pt_pad % 64).astype(F32)[:, None, :], (DEC_BATCH, 8, N_MROWS))
    rows_sel, gbias, nbias = _sample_select(page_table, iq2, w2, ik_new_t, pt_hi, pt_lo,
                                            jnp.swapaxes(cache_idx_k[0], 1, 2))

    def new_rows(c0):
        a = hs_s[:, c0:c0 + KV_WIDTH].reshape(DEC_BATCH, DEC_SEQ * N_KV_HEADS, HEAD_DIM)
        return jnp.pad(a, ((0, 0), (0, LANES - DEC_SEQ * N_KV_HEADS), (0, 0)))

    a_s4 = _sample_attention(rows_sel.reshape(-1), gbias.reshape(DEC_BATCH, DEC_SEQ, SA_COLS),
                             nbias.reshape(DEC_BATCH, DEC_SEQ, LANES),
                             hs_s[:, C_Q:C_Q + ATTN_WIDTH].reshape(DEC_BATCH, DEC_SEQ, N_HEADS, HEAD_DIM),
                             new_rows(C_K), new_rows(C_V),
                             cache_k[0].reshape(n_phys * PAGE_SIZE * N_KV_HEADS, HEAD_DIM),
                             cache_v[0].reshape(n_phys * PAGE_SIZE * N_KV_HEADS, HEAD_DIM))
    a_s = a_s4.reshape(N_SAMPLE_ROWS, ATTN_WIDTH).astype(BF16)
    a_small = jnp.concatenate([a_s, a_m[:N_META], jnp.zeros((SMALL_ROWS - N_SAMPLE_ROWS - N_META, ATTN_WIDTH),
                                                            BF16)], axis=0)

    win = jnp.repeat(jnp.asarray(POOL_WINDOWS, F32), POOL_GROUP)
    p_meta = hs[meta, C_P:C_P + POOL_WIDTH]
    tm_pool = 256
    pooled_p = _pool(hp, C_P // POOL_WIDTH, hp, hs, (META_ROW0 // N_META, C_P // POOL_WIDTH), row2(win),
                     rows=N_PROMPT_ROWS, tm=tm_pool, tiles_per_batch=SEQ // tm_pool)
    p_s = hs_s[:, C_P:C_P + POOL_WIDTH].reshape(DEC_BATCH, DEC_SEQ, POOL_WIDTH)
    grp = 24
    ext_s = jnp.concatenate([jnp.zeros((DEC_BATCH, 1, POOL_WIDTH), F32), state_pool[0], p_s,
                             jnp.zeros((DEC_BATCH, grp - 1 - POOL_CTX - DEC_SEQ, POOL_WIDTH), F32)], axis=1)
    ext_small = jnp.concatenate([ext_s.reshape(DEC_BATCH * grp, POOL_WIDTH),
                                 jnp.zeros((N_META, POOL_WIDTH), F32), p_meta], axis=0)
    n_ext = DEC_BATCH * grp + 2 * N_META
    cnt_meta = jnp.minimum(win[None, :], (jnp.arange(N_META, dtype=F32) + 1.0)[:, None])
    cnt_small = jnp.concatenate([jnp.broadcast_to(win[None], (n_ext - N_META, POOL_WIDTH)), cnt_meta], axis=0)
    zeros_halo = jnp.zeros((N_META, POOL_WIDTH), F32)
    pooled_ext = _pool(ext_small, 0, zeros_halo, zeros_halo, (0, 0), cnt_small,
                       rows=n_ext, tm=n_ext, tiles_per_batch=1)
    pooled_small = jnp.concatenate(
        [pooled_ext[:DEC_BATCH * grp].reshape(DEC_BATCH, grp, POOL_WIDTH)[:, 16:16 + DEC_SEQ].reshape(
            N_SAMPLE_ROWS, POOL_WIDTH),
         pooled_ext[n_ext - N_META:],
         jnp.zeros((SMALL_ROWS - N_SAMPLE_ROWS - N_META, POOL_WIDTH), BF16)], axis=0)

    mix_args = (w_pool_b, row2(pool_scale[0]), w_out_b, row2(ln1_g[0]), row2(ln1_b[0]))
    x1_p = _mix(pooled_p, a_p, xp, *mix_args, tm=256)
    x1_s = _mix(pooled_small, a_small, xs, *mix_args, tm=SMALL_ROWS)

    u_s = _matmul(x1_s.astype(BF16), w_up_b, SMALL_ROWS, "ffn_up_small")
    ffn_args = (conv_w[0], row2(conv_b[0]), w_down_b)
    ln2 = (row2(ln2_g[0]), row2(ln2_b[0]))
    tm_f = 1024
    y_p, tail_g, tail_v = _ffn_fused(x1_p, w_up_b, u_s, (META_ROW0 + N_META) // CONV_HALO - 1, *ffn_args, *ln2,
                                     tm=tm_f, tiles_per_batch=SEQ // tm_f)
    u_s3 = u_s[:N_SAMPLE_ROWS].reshape(DEC_BATCH, DEC_SEQ, 2 * D_FF)
    ext_u = jnp.concatenate([state_conv[0], u_s3], axis=1)
    u_m = u_s[meta]
    ext_m = jnp.concatenate([jnp.zeros((2, 2 * D_FF), F32), u_m], axis=0)
    tail0 = jnp.zeros((SMALL_ROWS - N_SAMPLE_ROWS - N_META, 2 * D_FF), F32)
    prev1 = jnp.concatenate([ext_u[:, 1:1 + DEC_SEQ].reshape(N_SAMPLE_ROWS, -1), ext_m[1:1 + N_META], tail0], 0)
    prev2 = jnp.concatenate([ext_u[:, 0:DEC_SEQ].reshape(N_SAMPLE_ROWS, -1), ext_m[0:N_META], tail0], 0)
    y_s = _ffn_tail_prev(u_s, prev1, prev2, *ffn_args, x1_s, *ln2, tm=SMALL_ROWS)

    y_prompt = y_p.reshape(BATCH, SEQ, D_MODEL)
    y_sample = y_s[:N_SAMPLE_ROWS].reshape(DEC_BATCH, DEC_SEQ, D_MODEL)
    pool_prompt = hp.reshape(BATCH, SEQ, PROJ_WIDTH)[:, SEQ - POOL_CTX:, C_P:C_P + POOL_WIDTH]
    conv_prompt = jnp.concatenate([tail_g[:, CONV_HALO - 2:], tail_v[:, CONV_HALO - 2:]], axis=-1)
    k_sample = hs_s[:, C_K:C_K + KV_WIDTH].reshape(DEC_BATCH, DEC_SEQ, N_KV_HEADS, HEAD_DIM)
    v_sample = hs_s[:, C_V:C_V + KV_WIDTH].reshape(DEC_BATCH, DEC_SEQ, N_KV_HEADS, HEAD_DIM)
    ik_sample = hs_s[:, C_IK:C_IK + IDX_DIM].reshape(DEC_BATCH, DEC_SEQ, IDX_DIM)
    pool_sample = jnp.concatenate([state_pool[0], p_s], axis=1)[:, DEC_SEQ:]
    conv_sample = ext_u[:, DEC_SEQ:]
    return (y_prompt, y_sample,
            k_prompt.reshape(1, BATCH, T_PROMPT, N_KV_HEADS, HEAD_DIM),
            v_prompt.reshape(1, BATCH, T_PROMPT, N_KV_HEADS, HEAD_DIM),
            ik_prompt[None], pool_prompt[None], conv_prompt[None],
            k_sample[None], v_sample[None], ik_sample[None], pool_sample[None], conv_sample[None])
```

```python
import functools

import numpy as np
import jax
import jax.numpy as jnp
from jax import lax
from jax.experimental import pallas as pl
from jax.experimental.pallas import tpu as pltpu

F32 = jnp.float32
BF16 = jnp.bfloat16
I32 = jnp.int32

D_MODEL = 2048
BATCH = 8
SEQ = 2048
DEC_BATCH = 32
DEC_SEQ = 4
PAST_LEN = 16384
PAGE_SIZE = 128
N_META = 16
ATTN_WIDTH = 1024
N_HEADS = 8
HEAD_DIM = 128
N_KV_HEADS = 4
KV_WIDTH = N_KV_HEADS * HEAD_DIM
ROT_DIM = 32
N_IDX_HEADS = 8
IDX_DIM = 64
IDX_ROT_DIM = 16
TOPK = 256
POOL_WIDTH = 1024
POOL_WINDOWS = (2, 4, 8, 16)
POOL_GROUP = 256
POOL_CTX = 15
D_FF = 5632
ROPE_THETA = 500000.0
LN_EPS = 1e-5
DEEPNORM_ALPHA = 2.0 ** 0.25
IN_OFFSETS = (1024, 1536, 2048, 2560, 2624, 2632)

LANES = 128
N_PROMPT_ROWS = BATCH * SEQ
N_SAMPLE_ROWS = DEC_BATCH * DEC_SEQ
SMALL_ROWS = 256
META_ROW0 = N_SAMPLE_ROWS
T_PROMPT = N_META + SEQ
NK_PROMPT = 2176
NK_SAMPLE = PAST_LEN + LANES
N_PAGES = PAST_LEN // PAGE_SIZE

C_Q, C_P, C_K, C_V, C_IQ, C_IK, C_IW = 0, 1024, 2048, 2560, 3072, 3584, 3712
PROJ_WIDTH = 3840
TN = 256
ROPE_A_TILES = (0, 1, 2, 3, 8, 9)
ROPE_B_TILES = (12, 13)
ROPE_BH_TILE = 14

VMEM_LIMIT = 56 * 1024 * 1024
FFN_VMEM_LIMIT = 62 * 1024 * 1024
INT_MIN = -2 ** 31
NEG_INF = float("-inf")
LOG2_E = 1.4426950408889634


def _cparams(sem, vmem_limit=VMEM_LIMIT, **kw):
    return pltpu.CompilerParams(dimension_semantics=sem, vmem_limit_bytes=vmem_limit, **kw)


def _rope_tables(pos, rot_dim, head_dim):
    half = rot_dim // 2
    inv_freq = ROPE_THETA ** (-np.arange(half, dtype=np.float64) / half)
    ang = np.asarray(pos, np.float64)[:, None] * inv_freq[None, :]
    cos, sin = np.cos(ang), np.sin(ang)
    rows = len(pos)
    zh = np.zeros((rows, half))
    rest0 = np.zeros((rows, head_dim - rot_dim))
    c = np.concatenate([cos, cos, rest0 + 1.0], axis=1)
    s1 = np.concatenate([-sin, zh, rest0], axis=1)
    s2 = np.concatenate([zh, sin, rest0], axis=1)
    reps = LANES // head_dim
    return tuple(jnp.asarray(np.tile(t, (1, reps)), F32) for t in (c, s1, s2))


def _rope(x, c_ref, s1_ref, s2_ref, half):
    return (x * c_ref[...] + pltpu.roll(x, LANES - half, 1) * s1_ref[...]
            + pltpu.roll(x, half, 1) * s2_ref[...])


def _proj_kernel(x_ref, w_ref, ca, sa1, sa2, cb, sb1, sb2, o_ref, xb_ref):
    j = pl.program_id(1)

    @pl.when(j == 0)
    def _():
        xb_ref[...] = x_ref[...].astype(BF16)

    acc = jnp.dot(xb_ref[...], w_ref[...], preferred_element_type=F32)
    is_a = functools.reduce(jnp.logical_or, [j == t for t in ROPE_A_TILES])
    is_b = functools.reduce(jnp.logical_or, [j == t for t in ROPE_B_TILES])
    is_bh = j == ROPE_BH_TILE

    @pl.when(is_a)
    def _():
        o_ref[:, :LANES] = _rope(acc[:, :LANES], ca, sa1, sa2, ROT_DIM // 2)
        o_ref[:, LANES:] = _rope(acc[:, LANES:], ca, sa1, sa2, ROT_DIM // 2)

    @pl.when(is_b)
    def _():
        o_ref[:, :LANES] = _rope(acc[:, :LANES], cb, sb1, sb2, IDX_ROT_DIM // 2)
        o_ref[:, LANES:] = _rope(acc[:, LANES:], cb, sb1, sb2, IDX_ROT_DIM // 2)

    @pl.when(is_bh)
    def _():
        o_ref[:, :LANES] = _rope(acc[:, :LANES], cb, sb1, sb2, IDX_ROT_DIM // 2)
        o_ref[:, LANES:] = acc[:, LANES:]

    @pl.when(jnp.logical_not(is_a | is_b | is_bh))
    def _():
        o_ref[...] = acc


def _project(x, w_all, tabs, tab_blocks, tm):
    rows = x.shape[0]
    tab_spec = pl.BlockSpec((tm, LANES), lambda i, j: (i % tab_blocks, 0))
    return pl.pallas_call(
        _proj_kernel,
        grid=(rows // tm, PROJ_WIDTH // TN),
        in_specs=[pl.BlockSpec((tm, D_MODEL), lambda i, j: (i, 0)),
                  pl.BlockSpec((D_MODEL, TN), lambda i, j: (0, j))] + [tab_spec] * 6,
        out_specs=pl.BlockSpec((tm, TN), lambda i, j: (i, j)),
        out_shape=jax.ShapeDtypeStruct((rows, PROJ_WIDTH), F32),
        scratch_shapes=[pltpu.VMEM((tm, D_MODEL), BF16)],
        compiler_params=_cparams(("arbitrary", "arbitrary")),
        name="in_proj",
    )(x, w_all, *tabs)


def _mm_kernel(x_ref, w_ref, o_ref):
    o_ref[...] = jnp.dot(x_ref[...], w_ref[...], preferred_element_type=F32)


def _matmul(x, w, tm, name):
    rows, kdim = x.shape
    n = w.shape[1]
    return pl.pallas_call(
        _mm_kernel,
        grid=(rows // tm, n // TN),
        in_specs=[pl.BlockSpec((tm, kdim), lambda i, j: (i, 0)),
                  pl.BlockSpec((kdim, TN), lambda i, j: (0, j))],
        out_specs=pl.BlockSpec((tm, TN), lambda i, j: (i, j)),
        out_shape=jax.ShapeDtypeStruct((rows, n), F32),
        compiler_params=_cparams(("arbitrary", "arbitrary")),
        name=name,
    )(x, w)


def _ordinal_to_float(key):
    bits = key ^ ((key >> 31) & 0x7FFFFFFF)
    return lax.bitcast_convert_type(bits, F32)


def _select_topk(sc_ref, col, valid, n_col_bits, cut_scr):
    rows = sc_ref.shape[0]
    k_f = float(TOPK)

    def count(pred):
        return jnp.sum(jnp.where(pred, 1.0, 0.0), axis=1, keepdims=True)

    few = count(valid) <= k_f
    key0 = jnp.where(count(sc_ref[...] >= 0.0) >= k_f, 0, INT_MIN).astype(I32)

    def value_step(i, key):
        cand = key + jnp.left_shift(jnp.int32(1), 30 - i)
        cnt = count(sc_ref[...] >= _ordinal_to_float(cand))
        return jnp.where(cnt >= k_f, cand, key)

    key = lax.fori_loop(0, 31, value_step, key0)
    thr = jnp.where(few, NEG_INF, _ordinal_to_float(key))
    need = k_f - count(sc_ref[...] > thr)
    surplus = jnp.where(few, 0.0, count(sc_ref[...] >= thr) - k_f)

    cut_scr[...] = jnp.full(cut_scr.shape, 2 ** n_col_bits - 1, I32)

    @pl.when(jnp.max(surplus) > 0.0)
    def _():
        def col_step(i, cut):
            cand = cut + jnp.left_shift(jnp.int32(1), n_col_bits - 1 - i)
            cnt = count((sc_ref[...] == thr) & (col < cand))
            return jnp.where(cnt < need, cand, cut)

        cut_scr[...] = lax.fori_loop(0, n_col_bits, col_step, jnp.zeros((rows, 1), I32))

    sc = sc_ref[...]
    return valid & ((sc > thr) | ((sc == thr) & (col <= cut_scr[...])))


def _select_topk_tiled(sc_ref, valid_fn, n_col_bits, cut_scr, out_ref):
    rows, n = sc_ref.shape
    k_f = float(TOPK)
    lane = lax.broadcasted_iota(I32, (rows, LANES), 1)
    tiles = [(t * LANES, slice(t * LANES, (t + 1) * LANES)) for t in range(n // LANES)]
    wide = lambda v: jnp.broadcast_to(v, (rows, LANES))

    def count(pred_fn):
        acc = jnp.zeros((rows, LANES), F32)
        for c0, cols in tiles:
            acc = acc + jnp.where(pred_fn(sc_ref[:, cols], lane + c0), 1.0, 0.0)
        return jnp.sum(acc, axis=1, keepdims=True)

    few = count(lambda sc, col: valid_fn(col)) <= k_f
    key0 = jnp.where(count(lambda sc, col: sc >= 0.0) >= k_f, 0, INT_MIN).astype(I32)

    def value_step(i, key):
        cand = wide(_ordinal_to_float(key + jnp.left_shift(jnp.int32(1), 30 - i)))
        cnt = count(lambda sc, col: sc >= cand)
        return jnp.where(cnt >= k_f, key + jnp.left_shift(jnp.int32(1), 30 - i), key)

    key = lax.fori_loop(0, 31, value_step, key0)
    thr = jnp.where(few, NEG_INF, _ordinal_to_float(key))
    thr_w = wide(thr)
    need = k_f - count(lambda sc, col: sc > thr_w)
    surplus = jnp.where(few, 0.0, count(lambda sc, col: sc >= thr_w) - k_f)

    cut_scr[...] = jnp.full(cut_scr.shape, 2 ** n_col_bits - 1, I32)

    @pl.when(jnp.max(surplus) > 0.0)
    def _():
        def col_step(i, cut):
            cand = cut + jnp.left_shift(jnp.int32(1), n_col_bits - 1 - i)
            cand_w = wide(cand)
            cnt = count(lambda sc, col: (sc == thr_w) & (col < cand_w))
            return jnp.where(cnt < need, cand, cut)

        cut_scr[...] = lax.fori_loop(0, n_col_bits, col_step, jnp.zeros((rows, 1), I32))

    cut_w = wide(cut_scr[...])
    for c0, cols in tiles:
        sc, col = sc_ref[:, cols], lane + c0
        keep = valid_fn(col) & ((sc > thr_w) | ((sc == thr_w) & (col <= cut_w)))
        out_ref[:, cols] = jnp.where(keep, 1.0, 0.0)


def _pattn_kernel(q_ref, iq_ref, misc_ref, kb_ref, vb_ref, ikb_ref, o_ref, sc_scr, bias_scr, cut_scr,
                  *, pos_first, nk):
    tq = q_ref.shape[0]
    pos0 = pos_first + tq * pl.program_id(1)

    ik = ikb_ref[0]
    w = misc_ref[:, LANES:LANES + N_IDX_HEADS] * (N_IDX_HEADS ** -0.5)
    score = jnp.zeros((tq, nk), F32)
    for h in range(N_IDX_HEADS):
        iq_h = (iq_ref[:, h * IDX_DIM:(h + 1) * IDX_DIM] * (IDX_DIM ** -0.5)).astype(BF16)
        s_h = lax.dot_general(iq_h, ik, (((1,), (1,)), ((), ())), preferred_element_type=F32)
        score = score + jnp.maximum(s_h, 0.0) * w[:, h:h + 1]

    qpos = pos0 + lax.broadcasted_iota(I32, (tq, nk), 0)
    col = lax.broadcasted_iota(I32, (tq, nk), 1)
    causal = col <= qpos
    sc_scr[...] = jnp.where(causal, score, NEG_INF)
    sel = _select_topk(sc_scr, col, causal, 12, cut_scr)
    bias_scr[...] = jnp.where(sel, 0.0, NEG_INF)

    q_scale = (HEAD_DIM ** -0.5) * LOG2_E
    for h in range(N_HEADS):
        kv = h // (N_HEADS // N_KV_HEADS)
        q_h = (q_ref[:, h * HEAD_DIM:(h + 1) * HEAD_DIM] * q_scale).astype(BF16)
        k_g = kb_ref[0, :, kv * HEAD_DIM:(kv + 1) * HEAD_DIM]
        s = lax.dot_general(q_h, k_g, (((1,), (1,)), ((), ())), preferred_element_type=F32)
        s = s + bias_scr[...]
        m = jnp.max(s, axis=1, keepdims=True)
        p = jnp.exp2(s - m)
        l = jnp.sum(p, axis=1, keepdims=True)
        v_g = vb_ref[0, :, kv * HEAD_DIM:(kv + 1) * HEAD_DIM]
        o = jnp.dot(p.astype(BF16), v_g, preferred_element_type=F32) / l
        o_ref[0, :, h * HEAD_DIM:(h + 1) * HEAD_DIM] = o.astype(o_ref.dtype)


def _prompt_attention(h, kb, vb, ikb, *, tq, n_batch, n_blocks, row_block0, blocks_per_batch, pos_first, nk):
    kern = functools.partial(_pattn_kernel, pos_first=pos_first, nk=nk)
    qrow = lambda b, j: row_block0 + b * blocks_per_batch + j
    return pl.pallas_call(
        kern,
        grid=(n_batch, n_blocks),
        in_specs=[pl.BlockSpec((tq, ATTN_WIDTH), lambda b, j: (qrow(b, j), C_Q // ATTN_WIDTH)),
                  pl.BlockSpec((tq, 512), lambda b, j: (qrow(b, j), C_IQ // 512)),
                  pl.BlockSpec((tq, 256), lambda b, j: (qrow(b, j), C_IK // 256)),
                  pl.BlockSpec((1, nk, KV_WIDTH), lambda b, j: (b, 0, 0)),
                  pl.BlockSpec((1, nk, KV_WIDTH), lambda b, j: (b, 0, 0)),
                  pl.BlockSpec((1, nk, IDX_DIM), lambda b, j: (b, 0, 0))],
        out_specs=pl.BlockSpec((1, tq, ATTN_WIDTH), lambda b, j: (b, j, 0)),
        out_shape=jax.ShapeDtypeStruct((n_batch, n_blocks * tq, ATTN_WIDTH), BF16),
        scratch_shapes=[pltpu.VMEM((tq, nk), F32), pltpu.VMEM((tq, nk), F32),
                        pltpu.VMEM((tq, 1), I32)],
        compiler_params=_cparams(("arbitrary", "arbitrary")),
        name="prompt_attention",
    )(h, h, h, kb, vb, ikb)


KV_SHIFT = 2
N_MROWS = 256


SS_PAIR = 2
SS_STEPS = DEC_BATCH // SS_PAIR
SS_CHUNK = 2048


def _ssel_kernel(pt_ref, iq_ref, w_ref, iknew_ref, pthi_ref, ptlo_ref, cache_ref,
                 rows_ref, gbias_ref, nbias_ref, ikbuf, sem, sc_scr, sel_scr, m_scr, cut_scr):
    s = pl.program_id(0)

    def page_copy(step, slot, e, p):
        return pltpu.make_async_copy(cache_ref.at[pt_ref[SS_PAIR * step + e, p]],
                                     ikbuf.at[slot, e, :, pl.ds(pl.multiple_of(p * PAGE_SIZE, PAGE_SIZE), PAGE_SIZE)],
                                     sem.at[slot])

    def fetch(step, slot):
        def issue(p, carry):
            for e in range(SS_PAIR):
                page_copy(step, slot, e, p).start(priority=e)
            return carry
        lax.fori_loop(0, N_PAGES, issue, 0)

    slot = s % 2

    @pl.when(s == 0)
    def _():
        fetch(0, 0)

    @pl.when(s + 1 < SS_STEPS)
    def _():
        fetch(s + 1, 1 - slot)

    pltpu.make_async_copy(ikbuf.at[slot], ikbuf.at[slot], sem.at[slot]).wait()

    w = w_ref[0] * (N_IDX_HEADS ** -0.5)
    iq = [iq_ref[0, e].astype(BF16) for e in range(SS_PAIR)]

    def score_of(keys_of):
        r = None
        for e in range(SS_PAIR):
            s_e = jnp.dot(iq[e], keys_of(e).astype(BF16), preferred_element_type=F32)
            r_e = jnp.maximum(s_e * (IDX_DIM ** -0.5), 0.0)
            r = r_e if r is None else r + r_e
        r = r * w
        acc = r[0:8]
        for h in range(1, N_IDX_HEADS):
            acc = acc + r[8 * h:8 * h + 8]
        return acc

    row0 = pl.multiple_of(s * 8, 8)
    for c in range(PAST_LEN // SS_CHUNK):
        cols = slice(c * SS_CHUNK, (c + 1) * SS_CHUNK)
        sc_scr[pl.ds(row0, 8), cols] = score_of(lambda e: ikbuf[slot, e, :, cols])
    sc_scr[pl.ds(row0, 8), PAST_LEN:] = score_of(lambda e: iknew_ref[0, e])

    @pl.when(s == SS_STEPS - 1)
    def _():
        _ssel_finish(pthi_ref, ptlo_ref, rows_ref, gbias_ref, nbias_ref, sc_scr, sel_scr, m_scr, cut_scr)


def _ssel_finish(pthi_ref, ptlo_ref, rows_ref, gbias_ref, nbias_ref, sc_scr, sel_scr, m_scr, cut_scr):
    n_q = sc_scr.shape[0]
    qpos = PAST_LEN + (lax.broadcasted_iota(I32, (n_q, LANES), 0) & (DEC_SEQ - 1))
    valid_fn = lambda col: col <= qpos
    lane = lax.broadcasted_iota(I32, (n_q, LANES), 1)
    for t in range(NK_SAMPLE // LANES):
        cols = slice(t * LANES, (t + 1) * LANES)
        sc_scr[:, cols] = jnp.where(valid_fn(lane + t * LANES), sc_scr[:, cols], NEG_INF)
    _select_topk_tiled(sc_scr, valid_fn, 15, cut_scr, sel_scr)

    nt = (((1,), (1,)), ((), ()))
    ones_b = jnp.ones((8, LANES), BF16)
    upper_pages = (lax.broadcasted_iota(I32, (N_MROWS, N_MROWS), 0)
                   < lax.broadcasted_iota(I32, (N_MROWS, N_MROWS), 1)).astype(BF16)
    upper_lanes = (lax.broadcasted_iota(I32, (LANES, LANES), 0)
                   < lax.broadcasted_iota(I32, (LANES, LANES), 1)).astype(BF16)
    lane_id = lax.broadcasted_iota(I32, (8, LANES), 1).astype(BF16)
    page_id = lax.broadcasted_iota(I32, (8, N_MROWS), 1).astype(BF16)
    jcol = lax.broadcasted_iota(I32, (TOPK, 1), 0).astype(F32)
    spread_picks = (lax.broadcasted_iota(I32, (TOPK, N_KV_HEADS * TOPK), 0)
                    == lax.broadcasted_iota(I32, (TOPK, N_KV_HEADS * TOPK), 1) >> KV_SHIFT).astype(BF16)
    spread_new = (lax.broadcasted_iota(I32, (LANES, LANES), 0)
                  == lax.broadcasted_iota(I32, (LANES, LANES), 1) >> KV_SHIFT).astype(BF16)
    m_scr[...] = jnp.zeros(m_scr.shape, F32)

    def compact8(batch_of, new_tile):
        each = lambda f: [f(k) for k in range(8)]
        f32dot = lambda a, b: jnp.dot(a, b, preferred_element_type=F32)
        ntdot = lambda a, b: lax.dot_general(a, b, nt, preferred_element_type=F32)
        m_b = each(lambda k: m_scr[k].astype(BF16))
        cnt = each(lambda k: ntdot(ones_b, m_b[k]))
        start = each(lambda k: f32dot(cnt[k].astype(BF16), upper_pages))
        in_page = each(lambda k: (start[k][0:1] <= jcol) & (jcol < start[k][0:1] + cnt[k][0:1]))
        a_b = each(lambda k: jnp.where(in_page[k], 1.0, 0.0).astype(BF16))
        m_j = each(lambda k: f32dot(a_b[k], m_b[k]))
        rank = each(lambda k: f32dot(m_j[k].astype(BF16), upper_lanes))
        start_j = each(lambda k: jnp.sum(jnp.where(in_page[k], start[k][0:1], 0.0), axis=1, keepdims=True))
        onehot = each(lambda k: jnp.where((m_j[k] > 0.5) & (rank[k] == jcol - start_j[k]), 1.0, 0.0).astype(BF16))
        off = each(lambda k: ntdot(lane_id, onehot[k])[0:1])
        page = each(lambda k: ntdot(page_id, a_b[k])[0:1])
        phys = each(lambda k: ntdot(pthi_ref[batch_of(k)].astype(BF16), a_b[k])[0:1] * 64.0
                    + ntdot(ptlo_ref[batch_of(k)].astype(BF16), a_b[k])[0:1])
        in_past = each(lambda k: page[k] < float(N_PAGES))
        row = each(lambda k: jnp.where(in_past[k], phys[k] * float(PAGE_SIZE) + off[k], 0.0).astype(I32))
        past4 = each(lambda k: f32dot(jnp.broadcast_to(jnp.where(in_past[k], 1.0, 0.0), (8, TOPK)).astype(BF16),
                                      spread_picks)[0:1])
        new4 = each(lambda k: f32dot(jnp.broadcast_to(new_tile[k:k + 1, :], (8, LANES)).astype(BF16),
                                     spread_new)[0:1])
        return (row, each(lambda k: jnp.where(past4[k] > 0.5, 0.0, NEG_INF)),
                each(lambda k: jnp.where(new4[k] > 0.5, 0.0, NEG_INF)))

    def compact_group(g, carry):
        r8 = pl.multiple_of(g * 8, 8)
        for c in range(N_PAGES + 1):
            tile = sel_scr[pl.ds(r8, 8), c * LANES:(c + 1) * LANES]
            for k in range(8):
                m_scr[k, c:c + 1, :] = tile[k:k + 1, :]
        new_tile = sel_scr[pl.ds(r8, 8), PAST_LEN:]
        rowid = lax.broadcasted_iota(I32, (8, 1), 0)
        rows_t = jnp.zeros((8, TOPK), I32)
        gb_t = jnp.zeros((8, N_KV_HEADS * TOPK), F32)
        nb_t = jnp.zeros((8, LANES), F32)
        rows8, gb8, nb8 = compact8(lambda k: g * (8 // DEC_SEQ) + k // DEC_SEQ, new_tile)
        for k, (row, gb, nb) in enumerate(zip(rows8, gb8, nb8)):
            rows_t = jnp.where(rowid == k, row, rows_t)
            gb_t = jnp.where(rowid == k, gb, gb_t)
            nb_t = jnp.where(rowid == k, nb, nb_t)
        rows_ref[pl.ds(r8, 8), :] = rows_t
        gbias_ref[pl.ds(r8, 8), :] = gb_t
        nbias_ref[pl.ds(r8, 8), :] = nb_t
        return carry

    lax.fori_loop(0, n_q // 8, compact_group, 0)


def _sample_select(page_table, iq2, w2, ik_new_t, pt_hi, pt_lo, cache_idx_t):
    n_q = N_SAMPLE_ROWS
    whole = lambda shape: pl.BlockSpec(shape, lambda s, pt: (0,) * len(shape))
    grid_spec = pltpu.PrefetchScalarGridSpec(
        num_scalar_prefetch=1,
        grid=(SS_STEPS,),
        in_specs=[pl.BlockSpec((1, SS_PAIR, 64, IDX_DIM), lambda s, pt: (s, 0, 0, 0)),
                  pl.BlockSpec((1, 64, 1), lambda s, pt: (s, 0, 0)),
                  pl.BlockSpec((1, SS_PAIR, IDX_DIM, LANES), lambda s, pt: (s, 0, 0, 0)),
                  whole((DEC_BATCH, 8, N_MROWS)),
                  whole((DEC_BATCH, 8, N_MROWS)),
                  pl.BlockSpec(memory_space=pl.ANY)],
        out_specs=[whole((n_q, TOPK)), whole((n_q, N_KV_HEADS * TOPK)), whole((n_q, LANES))],
        scratch_shapes=[pltpu.VMEM((2, SS_PAIR, IDX_DIM, PAST_LEN), F32),
                        pltpu.SemaphoreType.DMA((2,)),
                        pltpu.VMEM((n_q, NK_SAMPLE), F32),
                        pltpu.VMEM((n_q, NK_SAMPLE), F32),
                        pltpu.VMEM((8, N_MROWS, LANES), F32),
                        pltpu.VMEM((n_q, 1), I32)])
    return pl.pallas_call(
        _ssel_kernel,
        grid_spec=grid_spec,
        out_shape=[jax.ShapeDtypeStruct((n_q, TOPK), I32),
                   jax.ShapeDtypeStruct((n_q, N_KV_HEADS * TOPK), F32),
                   jax.ShapeDtypeStruct((n_q, LANES), F32)],
        compiler_params=_cparams(("arbitrary",)),
        name="sample_select",
    )(page_table, iq2, w2, ik_new_t, pt_hi, pt_lo, cache_idx_t)


SA_COLS = N_KV_HEADS * TOPK


def _sattn_kernel(rows_ref, gbias_ref, nbias_ref, q_ref, knew_ref, vnew_ref, ck_ref, cv_ref, o_ref,
                  kbuf, vbuf, sem):
    b = pl.program_id(0)
    n_slabs = DEC_SEQ * TOPK

    def issue(t, carry):
        row = pl.multiple_of(rows_ref[b * n_slabs + t] * N_KV_HEADS, N_KV_HEADS)
        dst = pl.ds(pl.multiple_of(t * N_KV_HEADS, N_KV_HEADS), N_KV_HEADS)
        pltpu.make_async_copy(ck_ref.at[pl.ds(row, N_KV_HEADS)], kbuf.at[dst], sem.at[0]).start(priority=0)
        pltpu.make_async_copy(cv_ref.at[pl.ds(row, N_KV_HEADS)], vbuf.at[dst], sem.at[1]).start(priority=1)
        return carry

    lax.fori_loop(0, n_slabs, issue, 0, unroll=8)
    pltpu.make_async_copy(kbuf, kbuf, sem.at[0]).wait()
    pltpu.make_async_copy(vbuf, vbuf, sem.at[1]).wait()

    nt = (((1,), (1,)), ((), ()))
    scale = HEAD_DIM ** -0.5
    k_new = knew_ref[0].astype(BF16)
    v_new = vnew_ref[0].astype(BF16)
    heads_per_kv = N_HEADS // N_KV_HEADS

    def own_kv(width):
        head = lax.broadcasted_iota(I32, (N_HEADS, width), 0)
        col = lax.broadcasted_iota(I32, (N_HEADS, width), 1)
        return (col & (N_KV_HEADS - 1)) == (head >> (heads_per_kv.bit_length() - 1))

    for q in range(DEC_SEQ):
        q_h = q_ref[0, q].astype(BF16)
        rows = slice(q * SA_COLS, (q + 1) * SA_COLS)
        s = lax.dot_general(q_h, kbuf[rows, :].astype(BF16), nt, preferred_element_type=F32) * scale
        s = jnp.where(own_kv(SA_COLS), s + gbias_ref[0, q:q + 1, :], NEG_INF)
        sn = lax.dot_general(q_h, k_new, nt, preferred_element_type=F32) * scale
        sn = jnp.where(own_kv(LANES), sn + nbias_ref[0, q:q + 1, :], NEG_INF)
        m = jnp.maximum(jnp.max(s, axis=1, keepdims=True), jnp.max(sn, axis=1, keepdims=True))
        p = jnp.exp(s - m)
        pn = jnp.exp(sn - m)
        l = jnp.sum(p, axis=1, keepdims=True) + jnp.sum(pn, axis=1, keepdims=True)
        o = (jnp.dot(p.astype(BF16), vbuf[rows, :].astype(BF16), preferred_element_type=F32)
             + jnp.dot(pn.astype(BF16), v_new, preferred_element_type=F32)) / l
        o_ref[0, q] = o


def _sample_attention(rows_flat, gbias, nbias, q_s, k_new, v_new, cache_k2d, cache_v2d):
    grid_spec = pltpu.PrefetchScalarGridSpec(
        num_scalar_prefetch=1,
        grid=(DEC_BATCH,),
        in_specs=[pl.BlockSpec((1, DEC_SEQ, SA_COLS), lambda b, r: (b, 0, 0)),
                  pl.BlockSpec((1, DEC_SEQ, LANES), lambda b, r: (b, 0, 0)),
                  pl.BlockSpec((1, DEC_SEQ, N_HEADS, HEAD_DIM), lambda b, r: (b, 0, 0, 0)),
                  pl.BlockSpec((1, LANES, HEAD_DIM), lambda b, r: (b, 0, 0)),
                  pl.BlockSpec((1, LANES, HEAD_DIM), lambda b, r: (b, 0, 0)),
                  pl.BlockSpec(memory_space=pl.ANY),
                  pl.BlockSpec(memory_space=pl.ANY)],
        out_specs=pl.BlockSpec((1, DEC_SEQ, N_HEADS, HEAD_DIM), lambda b, r: (b, 0, 0, 0)),
        scratch_shapes=[pltpu.VMEM((DEC_SEQ * SA_COLS, HEAD_DIM), F32),
                        pltpu.VMEM((DEC_SEQ * SA_COLS, HEAD_DIM), F32),
                        pltpu.SemaphoreType.DMA((2,))])
    return pl.pallas_call(
        _sattn_kernel,
        grid_spec=grid_spec,
        out_shape=jax.ShapeDtypeStruct((DEC_BATCH, DEC_SEQ, N_HEADS, HEAD_DIM), F32),
        compiler_params=_cparams(("arbitrary",), disable_bounds_checks=True),
        name="sample_attention",
    )(rows_flat, gbias, nbias, q_s, k_new, v_new, cache_k2d, cache_v2d)


def _pool_kernel(p_ref, halo_ref, first_ref, cnt_ref, o_ref, ext_scr, *, tiles_per_batch):
    i = pl.program_id(0)
    tm = p_ref.shape[0]
    ext_scr[0:N_META, :] = jnp.where(i % tiles_per_batch == 0, first_ref[...], halo_ref[...])
    ext_scr[N_META:, :] = p_ref[...]
    for g, win in enumerate(POOL_WINDOWS):
        cols = slice(g * POOL_GROUP, (g + 1) * POOL_GROUP)
        x_self = ext_scr[N_META:N_META + tm, cols]
        acc = x_self
        for d in range(1, win):
            acc = acc + ext_scr[N_META - d:N_META - d + tm, cols]
        o_ref[:, cols] = (acc / cnt_ref[:, cols] - x_self).astype(o_ref.dtype)


def _pool(p_src, p_col_block, halo_src, first_src, first_block, cnt, *, rows, tm, tiles_per_batch):
    kern = functools.partial(_pool_kernel, tiles_per_batch=tiles_per_batch)
    halo_per_tile = tm // N_META
    cnt_rows = cnt.shape[0]
    cnt_map = (lambda i: (0, 0)) if cnt_rows == 1 else (lambda i: (i, 0))
    return pl.pallas_call(
        kern,
        grid=(rows // tm,),
        in_specs=[pl.BlockSpec((tm, POOL_WIDTH), lambda i: (i, p_col_block)),
                  pl.BlockSpec((N_META, POOL_WIDTH),
                               lambda i: (jnp.maximum(i * halo_per_tile - 1, 0), p_col_block)),
                  pl.BlockSpec((N_META, POOL_WIDTH), lambda i: first_block),
                  pl.BlockSpec((1 if cnt_rows == 1 else tm, POOL_WIDTH), cnt_map)],
        out_specs=pl.BlockSpec((tm, POOL_WIDTH), lambda i: (i, 0)),
        out_shape=jax.ShapeDtypeStruct((rows, POOL_WIDTH), BF16),
        scratch_shapes=[pltpu.VMEM((tm + N_META, POOL_WIDTH), F32)],
        compiler_params=_cparams(("arbitrary",)),
        name="pool",
    )(p_src, halo_src, first_src, cnt)


def _layer_norm(y, g_ref, b_ref):
    mu = jnp.mean(y, axis=-1, keepdims=True)
    var = jnp.mean(jnp.square(y - mu), axis=-1, keepdims=True)
    return (y - mu) * lax.rsqrt(var + LN_EPS) * g_ref[...] + b_ref[...]


def _mix_kernel(pooled_ref, a_ref, x_ref, wp_ref, ps_ref, wo_ref, g_ref, b_ref, x1_ref):
    parts = []
    for g in range(len(POOL_WINDOWS)):
        cols = slice(g * POOL_GROUP, (g + 1) * POOL_GROUP)
        parts.append(jnp.dot(pooled_ref[:, cols], wp_ref[g], preferred_element_type=F32))
    m = jnp.concatenate(parts, axis=1) * ps_ref[...]
    mix = (jnp.dot(a_ref[...], wo_ref[:ATTN_WIDTH, :], preferred_element_type=F32)
           + jnp.dot(m.astype(BF16), wo_ref[ATTN_WIDTH:, :], preferred_element_type=F32))
    x1_ref[...] = _layer_norm(DEEPNORM_ALPHA * x_ref[...] + mix, g_ref, b_ref)


def _mix(pooled, a, x, w_pool, pool_scale, w_out, g1, b1, tm):
    rows = x.shape[0]
    row = lambda i: (i, 0)
    const2 = lambda i: (0, 0)
    return pl.pallas_call(
        _mix_kernel,
        grid=(rows // tm,),
        in_specs=[pl.BlockSpec((tm, POOL_WIDTH), row),
                  pl.BlockSpec((tm, ATTN_WIDTH), row),
                  pl.BlockSpec((tm, D_MODEL), row),
                  pl.BlockSpec((len(POOL_WINDOWS), POOL_GROUP, POOL_GROUP), lambda i: (0, 0, 0)),
                  pl.BlockSpec((1, POOL_WIDTH), const2),
                  pl.BlockSpec((D_MODEL, D_MODEL), const2),
                  pl.BlockSpec((1, D_MODEL), const2),
                  pl.BlockSpec((1, D_MODEL), const2)],
        out_specs=pl.BlockSpec((tm, D_MODEL), row),
        out_shape=jax.ShapeDtypeStruct((rows, D_MODEL), F32),
        compiler_params=_cparams(("arbitrary",)),
        name="mix_ln1",
    )(pooled, a, x, w_pool, pool_scale, w_out, g1, b1)


PA_TQ = 256
PA_GROUP = 1
TF = 512
TF_SUB = 256
N_FF_CHUNKS = D_FF // TF
CONV_HALO = 8


def _silu(x):
    return x * (0.5 * jnp.tanh(0.5 * x) + 0.5)


def _ffn_accumulate(gate, val, wd_ref, x1_ref, g_ref, b_ref, o_ref, acc_ref):
    c = pl.program_id(1)
    h = (_silu(gate) * val).astype(BF16)
    part = jnp.dot(h, wd_ref[...], preferred_element_type=F32)

    @pl.when(c == 0)
    def _():
        acc_ref[...] = part

    @pl.when(c > 0)
    def _():
        acc_ref[...] += part

    @pl.when(c == N_FF_CHUNKS - 1)
    def _():
        o_ref[...] = _layer_norm(DEEPNORM_ALPHA * x1_ref[...] + acc_ref[...], g_ref, b_ref)


def _ffn_fused_kernel(x1_ref, wg_ref, wv_ref, fg_ref, fv_ref, cwg_ref, cwv_ref, cbg_ref, cbv_ref,
                      wd_ref, g_ref, b_ref, o_ref, tg_ref, tv_ref, xb_scr, carry_g, carry_v, ext_scr,
                      *, tiles_per_batch):
    i = pl.program_id(0)
    c = pl.program_id(1)
    tm = x1_ref.shape[0]
    first = i % tiles_per_batch == 0

    @pl.when(c == 0)
    def _():
        xb_scr[...] = x1_ref[...].astype(BF16)
        o_ref[...] = jnp.zeros(o_ref.shape, F32)

    @pl.when(i == 0)
    def _():
        carry_g[c] = jnp.zeros((CONV_HALO, TF), F32)
        carry_v[c] = jnp.zeros((CONV_HALO, TF), F32)

    part = None
    for s in range(TF // TF_SUB):
        cols = slice(s * TF_SUB, (s + 1) * TF_SUB)

        def conv(w_ref, f_ref, carry, tail_ref, cw_ref, cb_ref, ext):
            u = jnp.dot(xb_scr[...], w_ref[:, cols], preferred_element_type=F32)
            ext[0:CONV_HALO, :] = jnp.where(first, f_ref[:, cols], carry[c, :, cols])
            ext[CONV_HALO:, :] = u
            tail = u[tm - CONV_HALO:, :]
            carry[c, :, cols] = tail
            tail_ref[0, :, cols] = tail
            return (cb_ref[:, cols] + cw_ref[0:1, cols] * ext[CONV_HALO - 2:CONV_HALO - 2 + tm, :]
                    + cw_ref[1:2, cols] * ext[CONV_HALO - 1:CONV_HALO - 1 + tm, :]
                    + cw_ref[2:3, cols] * u)

        gate = conv(wg_ref, fg_ref, carry_g, tg_ref, cwg_ref, cbg_ref, ext_scr.at[2 * s])
        val = conv(wv_ref, fv_ref, carry_v, tv_ref, cwv_ref, cbv_ref, ext_scr.at[2 * s + 1])
        h = (_silu(gate) * val).astype(BF16)
        d = jnp.dot(h, wd_ref[cols, :], preferred_element_type=F32)
        part = d if part is None else part + d
    o_ref[...] += part

    @pl.when(c == N_FF_CHUNKS - 1)
    def _():
        o_ref[...] = _layer_norm(DEEPNORM_ALPHA * x1_ref[...] + o_ref[...], g_ref, b_ref)


def _ffn_prev_kernel(ug_ref, uv_ref, p1g_ref, p1v_ref, p2g_ref, p2v_ref, cwg_ref, cwv_ref, cbg_ref, cbv_ref,
                     wd_ref, x1_ref, g_ref, b_ref, o_ref, acc_ref):
    def conv(u_ref, p1_ref, p2_ref, cw_ref, cb_ref):
        return (cb_ref[...] + cw_ref[0:1, :] * p2_ref[...] + cw_ref[1:2, :] * p1_ref[...]
                + cw_ref[2:3, :] * u_ref[...])

    gate = conv(ug_ref, p1g_ref, p2g_ref, cwg_ref, cbg_ref)
    val = conv(uv_ref, p1v_ref, p2v_ref, cwv_ref, cbv_ref)
    _ffn_accumulate(gate, val, wd_ref, x1_ref, g_ref, b_ref, o_ref, acc_ref)


def _ffn_common_specs(tm):
    gate_c = lambda i, c: (0, c)
    val_c = lambda i, c: (0, N_FF_CHUNKS + c)
    return [pl.BlockSpec((3, TF), gate_c), pl.BlockSpec((3, TF), val_c),
            pl.BlockSpec((1, TF), gate_c), pl.BlockSpec((1, TF), val_c),
            pl.BlockSpec((TF, D_MODEL), lambda i, c: (c, 0)),
            pl.BlockSpec((tm, D_MODEL), lambda i, c: (i, 0)),
            pl.BlockSpec((1, D_MODEL), lambda i, c: (0, 0)),
            pl.BlockSpec((1, D_MODEL), lambda i, c: (0, 0))]


def _ffn_fused(x1, w_up, u_first, first_row_block, conv_w, conv_b, w_down, g2, b2, *, tm, tiles_per_batch):
    rows = x1.shape[0]
    kern = functools.partial(_ffn_fused_kernel, tiles_per_batch=tiles_per_batch)
    gate_c = lambda i, c: (0, c)
    val_c = lambda i, c: (0, N_FF_CHUNKS + c)
    const = lambda i, c: (0, 0)
    tail_spec = pl.BlockSpec((1, CONV_HALO, TF), lambda i, c: (i, 0, c))
    tail_shape = jax.ShapeDtypeStruct((rows // tm, CONV_HALO, D_FF), F32)
    y, tail_g, tail_v = pl.pallas_call(
        kern,
        grid=(rows // tm, N_FF_CHUNKS),
        in_specs=[pl.BlockSpec((tm, D_MODEL), lambda i, c: (i, 0), pipeline_mode=pl.Buffered(1)),
                  pl.BlockSpec((D_MODEL, TF), gate_c), pl.BlockSpec((D_MODEL, TF), val_c),
                  pl.BlockSpec((CONV_HALO, TF), lambda i, c: (first_row_block, c)),
                  pl.BlockSpec((CONV_HALO, TF), lambda i, c: (first_row_block, N_FF_CHUNKS + c)),
                  pl.BlockSpec((3, TF), gate_c), pl.BlockSpec((3, TF), val_c),
                  pl.BlockSpec((1, TF), gate_c), pl.BlockSpec((1, TF), val_c),
                  pl.BlockSpec((TF, D_MODEL), lambda i, c: (c, 0)),
                  pl.BlockSpec((1, D_MODEL), const), pl.BlockSpec((1, D_MODEL), const)],
        out_specs=[pl.BlockSpec((tm, D_MODEL), lambda i, c: (i, 0)), tail_spec, tail_spec],
        out_shape=[jax.ShapeDtypeStruct((rows, D_MODEL), F32), tail_shape, tail_shape],
        scratch_shapes=[pltpu.VMEM((tm, D_MODEL), BF16),
                        pltpu.VMEM((N_FF_CHUNKS, CONV_HALO, TF), F32),
                        pltpu.VMEM((N_FF_CHUNKS, CONV_HALO, TF), F32),
                        pltpu.VMEM((2 * (TF // TF_SUB), tm + CONV_HALO, TF_SUB), F32)],
        compiler_params=_cparams(("arbitrary", "arbitrary"), vmem_limit=FFN_VMEM_LIMIT),
        name="ffn",
    )(x1, w_up, w_up, u_first, u_first, conv_w, conv_w, conv_b, conv_b, w_down, g2, b2)
    last = slice(tiles_per_batch - 1, None, tiles_per_batch)
    return y, tail_g[last], tail_v[last]


def _ffn_tail_prev(u, prev1, prev2, conv_w, conv_b, w_down, x1, g2, b2, *, tm):
    rows = u.shape[0]
    gate_t = lambda i, c: (i, c)
    val_t = lambda i, c: (i, N_FF_CHUNKS + c)
    return pl.pallas_call(
        _ffn_prev_kernel,
        grid=(rows // tm, N_FF_CHUNKS),
        in_specs=[pl.BlockSpec((tm, TF), gate_t), pl.BlockSpec((tm, TF), val_t),
                  pl.BlockSpec((tm, TF), gate_t), pl.BlockSpec((tm, TF), val_t),
                  pl.BlockSpec((tm, TF), gate_t), pl.BlockSpec((tm, TF), val_t)]
                 + _ffn_common_specs(tm),
        out_specs=pl.BlockSpec((tm, D_MODEL), lambda i, c: (i, 0)),
        out_shape=jax.ShapeDtypeStruct((rows, D_MODEL), F32),
        scratch_shapes=[pltpu.VMEM((tm, D_MODEL), F32)],
        compiler_params=_cparams(("arbitrary", "arbitrary")),
        name="ffn_tail_small",
    )(u, u, prev1, prev1, prev2, prev2, conv_w, conv_w, conv_b, conv_b, w_down, x1, g2, b2)


def kernel(x_prompt, x_sample, cache_k, cache_v, cache_idx_k, state_pool, state_conv, page_table, meta_tokens,
           w_in, w_pool, pool_scale, w_out, ln1_g, ln1_b, w_up, conv_w, conv_b, w_down, ln2_g, ln2_b):
    n_phys = cache_k.shape[1]

    wq, wk, wv, wiq, wik, wiw, wp = jnp.split(w_in[0], IN_OFFSETS, axis=1)
    zcols = lambda n: jnp.zeros((D_MODEL, n), F32)
    w_all = jnp.concatenate([wq, wp, wk, wv, wiq, wik, zcols(64), wiw, zcols(120)], axis=1).astype(BF16)
    w_pool_b = w_pool[0].astype(BF16)
    w_out_b = w_out[0].astype(BF16)
    w_up_b = w_up[0].astype(BF16)
    w_down_b = w_down[0].astype(BF16)
    row2 = lambda a: a.reshape(1, -1)

    pos_prompt = N_META + np.arange(SEQ)
    pos_small = np.concatenate([np.tile(PAST_LEN + np.arange(DEC_SEQ), DEC_BATCH), np.arange(N_META),
                                np.zeros((SMALL_ROWS - N_SAMPLE_ROWS - N_META,), np.int64)])
    tabs_p = _rope_tables(pos_prompt, ROT_DIM, HEAD_DIM) + _rope_tables(pos_prompt, IDX_ROT_DIM, IDX_DIM)
    tabs_s = _rope_tables(pos_small, ROT_DIM, HEAD_DIM) + _rope_tables(pos_small, IDX_ROT_DIM, IDX_DIM)

    xp = x_prompt.reshape(N_PROMPT_ROWS, D_MODEL)
    xs = jnp.concatenate([x_sample.reshape(N_SAMPLE_ROWS, D_MODEL), meta_tokens.astype(F32),
                          jnp.zeros((SMALL_ROWS - N_SAMPLE_ROWS - N_META, D_MODEL), F32)], axis=0)

    tm_p = 1024
    hp = _project(xp, w_all, tabs_p, SEQ // tm_p, tm_p)
    hs = _project(xs, w_all, tabs_s, 1, SMALL_ROWS)

    meta = slice(META_ROW0, META_ROW0 + N_META)

    def with_meta(c0, width):
        m = jnp.broadcast_to(hs[meta, c0:c0 + width][None], (BATCH, N_META, width))
        return jnp.concatenate([m, hp[:, c0:c0 + width].reshape(BATCH, SEQ, width)], axis=1)

    k_prompt = with_meta(C_K, KV_WIDTH)
    v_prompt = with_meta(C_V, KV_WIDTH)
    ik_prompt = with_meta(C_IK, IDX_DIM)
    pad_keys = lambda a: jnp.pad(a.astype(BF16), ((0, 0), (0, NK_PROMPT - T_PROMPT), (0, 0)))
    kb, vb, ikb = pad_keys(k_prompt), pad_keys(v_prompt), pad_keys(ik_prompt)

    blocks_per_batch = SEQ // PA_TQ
    a_groups = []
    for j0 in range(0, blocks_per_batch, PA_GROUP):
        nk = N_META + (j0 + PA_GROUP) * PA_TQ
        nk = -(-nk // LANES) * LANES
        a_groups.append(_prompt_attention(hp, kb, vb, ikb, tq=PA_TQ, n_batch=BATCH, n_blocks=PA_GROUP,
                                          row_block0=j0, blocks_per_batch=blocks_per_batch,
                                          pos_first=N_META + j0 * PA_TQ, nk=nk))
    a_p = jnp.concatenate(a_groups, axis=1).reshape(N_PROMPT_ROWS, ATTN_WIDTH)
    a_m = _prompt_attention(hs, kb, vb, ikb, tq=LANES, n_batch=1, n_blocks=1, row_block0=META_ROW0 // LANES,
                            blocks_per_batch=1, pos_first=0, nk=LANES)[0]

    hs_s = hs[:N_SAMPLE_ROWS]
    iq_s = hs_s[:, C_IQ:C_IQ + N_IDX_HEADS * IDX_DIM].reshape(SS_STEPS, SS_PAIR, DEC_SEQ, N_IDX_HEADS, IDX_DIM)
    iq_s = iq_s.transpose(0, 1, 3, 2, 4)
    iq2 = jnp.stack([jnp.pad(iq_s[:, e], ((0, 0), (0, 0), (DEC_SEQ * e, DEC_SEQ * (SS_PAIR - 1 - e)), (0, 0)))
                     for e in range(SS_PAIR)], axis=1).reshape(SS_STEPS, SS_PAIR, N_IDX_HEADS * 8, IDX_DIM)
    w_s = hs_s[:, C_IW:C_IW + N_IDX_HEADS].reshape(SS_STEPS, SS_PAIR * DEC_SEQ, N_IDX_HEADS)
    w2 = w_s.transpose(0, 2, 1).reshape(SS_STEPS, N_IDX_HEADS * 8, 1)
    ik_new_t = jnp.pad(hs_s[:, C_IK:C_IK + IDX_DIM].reshape(DEC_BATCH, DEC_SEQ, IDX_DIM).transpose(0, 2, 1),
                       ((0, 0), (0, 0), (0, LANES - DEC_SEQ))).reshape(SS_STEPS, SS_PAIR, IDX_DIM, LANES)

    pt_pad = jnp.pad(page_table, ((0, 0), (0, N_MROWS - N_PAGES)))
    pt_hi = jnp.broadcast_to((pt_pad // 64).astype(F32)[:, None, :], (DEC_BATCH, 8, N_MROWS))
    pt_lo = jnp.broadcast_to((pt_pad % 64).astype(F32)[:, None, :], (DEC_BATCH, 8, N_MROWS))
    rows_sel, gbias, nbias = _sample_select(page_table, iq2, w2, ik_new_t, pt_hi, pt_lo,
                                            jnp.swapaxes(cache_idx_k[0], 1, 2))

    def new_rows(c0):
        a = hs_s[:, c0:c0 + KV_WIDTH].reshape(DEC_BATCH, DEC_SEQ * N_KV_HEADS, HEAD_DIM)
        return jnp.pad(a, ((0, 0), (0, LANES - DEC_SEQ * N_KV_HEADS), (0, 0)))

    a_s4 = _sample_attention(rows_sel.reshape(-1), gbias.reshape(DEC_BATCH, DEC_SEQ, SA_COLS),
                             nbias.reshape(DEC_BATCH, DEC_SEQ, LANES),
                             hs_s[:, C_Q:C_Q + ATTN_WIDTH].reshape(DEC_BATCH, DEC_SEQ, N_HEADS, HEAD_DIM),
                             new_rows(C_K), new_rows(C_V),
                             cache_k[0].reshape(n_phys * PAGE_SIZE * N_KV_HEADS, HEAD_DIM),
                             cache_v[0].reshape(n_phys * PAGE_SIZE * N_KV_HEADS, HEAD_DIM))
    a_s = a_s4.reshape(N_SAMPLE_ROWS, ATTN_WIDTH).astype(BF16)
    a_small = jnp.concatenate([a_s, a_m[:N_META], jnp.zeros((SMALL_ROWS - N_SAMPLE_ROWS - N_META, ATTN_WIDTH),
                                                            BF16)], axis=0)

    win = jnp.repeat(jnp.asarray(POOL_WINDOWS, F32), POOL_GROUP)
    p_meta = hs[meta, C_P:C_P + POOL_WIDTH]
    tm_pool = 256
    pooled_p = _pool(hp, C_P // POOL_WIDTH, hp, hs, (META_ROW0 // N_META, C_P // POOL_WIDTH), row2(win),
                     rows=N_PROMPT_ROWS, tm=tm_pool, tiles_per_batch=SEQ // tm_pool)
    p_s = hs_s[:, C_P:C_P + POOL_WIDTH].reshape(DEC_BATCH, DEC_SEQ, POOL_WIDTH)
    grp = 24
    ext_s = jnp.concatenate([jnp.zeros((DEC_BATCH, 1, POOL_WIDTH), F32), state_pool[0], p_s,
                             jnp.zeros((DEC_BATCH, grp - 1 - POOL_CTX - DEC_SEQ, POOL_WIDTH), F32)], axis=1)
    ext_small = jnp.concatenate([ext_s.reshape(DEC_BATCH * grp, POOL_WIDTH),
                                 jnp.zeros((N_META, POOL_WIDTH), F32), p_meta], axis=0)
    n_ext = DEC_BATCH * grp + 2 * N_META
    cnt_meta = jnp.minimum(win[None, :], (jnp.arange(N_META, dtype=F32) + 1.0)[:, None])
    cnt_small = jnp.concatenate([jnp.broadcast_to(win[None], (n_ext - N_META, POOL_WIDTH)), cnt_meta], axis=0)
    zeros_halo = jnp.zeros((N_META, POOL_WIDTH), F32)
    pooled_ext = _pool(ext_small, 0, zeros_halo, zeros_halo, (0, 0), cnt_small,
                       rows=n_ext, tm=n_ext, tiles_per_batch=1)
    pooled_small = jnp.concatenate(
        [pooled_ext[:DEC_BATCH * grp].reshape(DEC_BATCH, grp, POOL_WIDTH)[:, 16:16 + DEC_SEQ].reshape(
            N_SAMPLE_ROWS, POOL_WIDTH),
         pooled_ext[n_ext - N_META:],
         jnp.zeros((SMALL_ROWS - N_SAMPLE_ROWS - N_META, POOL_WIDTH), BF16)], axis=0)

    mix_args = (w_pool_b, row2(pool_scale[0]), w_out_b, row2(ln1_g[0]), row2(ln1_b[0]))
    x1_p = _mix(pooled_p, a_p, xp, *mix_args, tm=512)
    x1_s = _mix(pooled_small, a_small, xs, *mix_args, tm=SMALL_ROWS)

    u_s = _matmul(x1_s.astype(BF16), w_up_b, SMALL_ROWS, "ffn_up_small")
    ffn_args = (conv_w[0], row2(conv_b[0]), w_down_b)
    ln2 = (row2(ln2_g[0]), row2(ln2_b[0]))
    tm_f = 1024
    y_p, tail_g, tail_v = _ffn_fused(x1_p, w_up_b, u_s, (META_ROW0 + N_META) // CONV_HALO - 1, *ffn_args, *ln2,
                                     tm=tm_f, tiles_per_batch=SEQ // tm_f)
    u_s3 = u_s[:N_SAMPLE_ROWS].reshape(DEC_BATCH, DEC_SEQ, 2 * D_FF)
    ext_u = jnp.concatenate([state_conv[0], u_s3], axis=1)
    u_m = u_s[meta]
    ext_m = jnp.concatenate([jnp.zeros((2, 2 * D_FF), F32), u_m], axis=0)
    tail0 = jnp.zeros((SMALL_ROWS - N_SAMPLE_ROWS - N_META, 2 * D_FF), F32)
    prev1 = jnp.concatenate([ext_u[:, 1:1 + DEC_SEQ].reshape(N_SAMPLE_ROWS, -1), ext_m[1:1 + N_META], tail0], 0)
    prev2 = jnp.concatenate([ext_u[:, 0:DEC_SEQ].reshape(N_SAMPLE_ROWS, -1), ext_m[0:N_META], tail0], 0)
    y_s = _ffn_tail_prev(u_s, prev1, prev2, *ffn_args, x1_s, *ln2, tm=SMALL_ROWS)

    y_prompt = y_p.reshape(BATCH, SEQ, D_MODEL)
    y_sample = y_s[:N_SAMPLE_ROWS].reshape(DEC_BATCH, DEC_SEQ, D_MODEL)
    pool_prompt = hp.reshape(BATCH, SEQ, PROJ_WIDTH)[:, SEQ - POOL_CTX:, C_P:C_P + POOL_WIDTH]
    conv_prompt = jnp.concatenate([tail_g[:, CONV_HALO - 2:], tail_v[:, CONV_HALO - 2:]], axis=-1)
    k_sample = hs_s[:, C_K:C_K + KV_WIDTH].reshape(DEC_BATCH, DEC_SEQ, N_KV_HEADS, HEAD_DIM)
    v_sample = hs_s[:, C_V:C_V + KV_WIDTH].reshape(DEC_BATCH, DEC_SEQ, N_KV_HEADS, HEAD_DIM)
    ik_sample = hs_s[:, C_IK:C_IK + IDX_DIM].reshape(DEC_BATCH, DEC_SEQ, IDX_DIM)
    pool_sample = jnp.concatenate([state_pool[0], p_s], axis=1)[:, DEC_SEQ:]
    conv_sample = ext_u[:, DEC_SEQ:]
    return (y_prompt, y_sample,
            k_prompt.reshape(1, BATCH, T_PROMPT, N_KV_HEADS, HEAD_DIM),
            v_prompt.reshape(1, BATCH, T_PROMPT, N_KV_HEADS, HEAD_DIM),
            ik_prompt[None], pool_prompt[None], conv_prompt[None],
            k_sample[None], v_sample[None], ik_sample[None], pool_sample[None], conv_sample[None])
```

```python
import functools

import numpy as np
import jax
import jax.numpy as jnp
from jax import lax
from jax.experimental import pallas as pl
from jax.experimental.pallas import tpu as pltpu

F32 = jnp.float32
BF16 = jnp.bfloat16
I32 = jnp.int32

D_MODEL = 2048
BATCH = 8
SEQ = 2048
DEC_BATCH = 32
DEC_SEQ = 4
PAST_LEN = 16384
PAGE_SIZE = 128
N_META = 16
ATTN_WIDTH = 1024
N_HEADS = 8
HEAD_DIM = 128
N_KV_HEADS = 4
KV_WIDTH = N_KV_HEADS * HEAD_DIM
ROT_DIM = 32
N_IDX_HEADS = 8
IDX_DIM = 64
IDX_ROT_DIM = 16
TOPK = 256
POOL_WIDTH = 1024
POOL_WINDOWS = (2, 4, 8, 16)
POOL_GROUP = 256
POOL_CTX = 15
D_FF = 5632
ROPE_THETA = 500000.0
LN_EPS = 1e-5
DEEPNORM_ALPHA = 2.0 ** 0.25
IN_OFFSETS = (1024, 1536, 2048, 2560, 2624, 2632)

LANES = 128
N_PROMPT_ROWS = BATCH * SEQ
N_SAMPLE_ROWS = DEC_BATCH * DEC_SEQ
SMALL_ROWS = 256
META_ROW0 = N_SAMPLE_ROWS
T_PROMPT = N_META + SEQ
NK_PROMPT = 2176
NK_SAMPLE = PAST_LEN + LANES
N_PAGES = PAST_LEN // PAGE_SIZE

C_Q, C_P, C_K, C_V, C_IQ, C_IK, C_IW = 0, 1024, 2048, 2560, 3072, 3584, 3712
PROJ_WIDTH = 4096
TN = 512
ROPE_A_TILES = (0, 1, 4)
ROPE_B_TILES = (6,)
ROPE_BH_TILE = 7

VMEM_LIMIT = 56 * 1024 * 1024
FFN_VMEM_LIMIT = 62 * 1024 * 1024
INT_MIN = -2 ** 31
NEG_INF = float("-inf")
LOG2_E = 1.4426950408889634


def _cparams(sem, vmem_limit=VMEM_LIMIT, **kw):
    return pltpu.CompilerParams(dimension_semantics=sem, vmem_limit_bytes=vmem_limit, **kw)


def _rope_tables(pos, rot_dim, head_dim):
    half = rot_dim // 2
    inv_freq = ROPE_THETA ** (-np.arange(half, dtype=np.float64) / half)
    ang = np.asarray(pos, np.float64)[:, None] * inv_freq[None, :]
    cos, sin = np.cos(ang), np.sin(ang)
    rows = len(pos)
    zh = np.zeros((rows, half))
    rest0 = np.zeros((rows, head_dim - rot_dim))
    c = np.concatenate([cos, cos, rest0 + 1.0], axis=1)
    s1 = np.concatenate([-sin, zh, rest0], axis=1)
    s2 = np.concatenate([zh, sin, rest0], axis=1)
    reps = LANES // head_dim
    return tuple(jnp.asarray(np.tile(t, (1, reps)), F32) for t in (c, s1, s2))


def _rope(x, c_ref, s1_ref, s2_ref, half):
    return (x * c_ref[...] + pltpu.roll(x, LANES - half, 1) * s1_ref[...]
            + pltpu.roll(x, half, 1) * s2_ref[...])


def _proj_kernel(x_ref, w_ref, ca, sa1, sa2, cb, sb1, sb2, o_ref, xb_ref):
    j = pl.program_id(1)

    @pl.when(j == 0)
    def _():
        xb_ref[...] = x_ref[...].astype(BF16)

    acc = jnp.dot(xb_ref[...], w_ref[...], preferred_element_type=F32)
    is_a = functools.reduce(jnp.logical_or, [j == t for t in ROPE_A_TILES])
    is_b = functools.reduce(jnp.logical_or, [j == t for t in ROPE_B_TILES])
    is_bh = j == ROPE_BH_TILE

    blocks = [slice(t * LANES, (t + 1) * LANES) for t in range(TN // LANES)]

    @pl.when(is_a)
    def _():
        for cols in blocks:
            o_ref[:, cols] = _rope(acc[:, cols], ca, sa1, sa2, ROT_DIM // 2)

    @pl.when(is_b)
    def _():
        for cols in blocks:
            o_ref[:, cols] = _rope(acc[:, cols], cb, sb1, sb2, IDX_ROT_DIM // 2)

    @pl.when(is_bh)
    def _():
        o_ref[:, :LANES] = _rope(acc[:, :LANES], cb, sb1, sb2, IDX_ROT_DIM // 2)
        o_ref[:, LANES:] = acc[:, LANES:]

    @pl.when(jnp.logical_not(is_a | is_b | is_bh))
    def _():
        o_ref[...] = acc


def _project(x, w_all, tabs, tab_blocks, tm):
    rows = x.shape[0]
    tab_spec = pl.BlockSpec((tm, LANES), lambda i, j: (i % tab_blocks, 0))
    return pl.pallas_call(
        _proj_kernel,
        grid=(rows // tm, PROJ_WIDTH // TN),
        in_specs=[pl.BlockSpec((tm, D_MODEL), lambda i, j: (i, 0)),
                  pl.BlockSpec((D_MODEL, TN), lambda i, j: (0, j))] + [tab_spec] * 6,
        out_specs=pl.BlockSpec((tm, TN), lambda i, j: (i, j)),
        out_shape=jax.ShapeDtypeStruct((rows, PROJ_WIDTH), F32),
        scratch_shapes=[pltpu.VMEM((tm, D_MODEL), BF16)],
        compiler_params=_cparams(("arbitrary", "arbitrary")),
        name="in_proj",
    )(x, w_all, *tabs)


def _mm_kernel(x_ref, w_ref, o_ref):
    o_ref[...] = jnp.dot(x_ref[...], w_ref[...], preferred_element_type=F32)


def _matmul(x, w, tm, name):
    rows, kdim = x.shape
    n = w.shape[1]
    return pl.pallas_call(
        _mm_kernel,
        grid=(rows // tm, n // TN),
        in_specs=[pl.BlockSpec((tm, kdim), lambda i, j: (i, 0)),
                  pl.BlockSpec((kdim, TN), lambda i, j: (0, j))],
        out_specs=pl.BlockSpec((tm, TN), lambda i, j: (i, j)),
        out_shape=jax.ShapeDtypeStruct((rows, n), F32),
        compiler_params=_cparams(("arbitrary", "arbitrary")),
        name=name,
    )(x, w)


def _ordinal_to_float(key):
    bits = key ^ ((key >> 31) & 0x7FFFFFFF)
    return lax.bitcast_convert_type(bits, F32)


def _select_topk(sc_ref, col, valid, n_col_bits, cut_scr):
    rows = sc_ref.shape[0]
    k_f = float(TOPK)

    def count(pred):
        return jnp.sum(jnp.where(pred, 1.0, 0.0), axis=1, keepdims=True)

    few = count(valid) <= k_f
    key0 = jnp.where(count(sc_ref[...] >= 0.0) >= k_f, 0, INT_MIN).astype(I32)

    def value_step(i, key):
        cand = key + jnp.left_shift(jnp.int32(1), 30 - i)
        cnt = count(sc_ref[...] >= _ordinal_to_float(cand))
        return jnp.where(cnt >= k_f, cand, key)

    key = lax.fori_loop(0, 31, value_step, key0)
    thr = jnp.where(few, NEG_INF, _ordinal_to_float(key))
    need = k_f - count(sc_ref[...] > thr)
    surplus = jnp.where(few, 0.0, count(sc_ref[...] >= thr) - k_f)

    cut_scr[...] = jnp.full(cut_scr.shape, 2 ** n_col_bits - 1, I32)

    @pl.when(jnp.max(surplus) > 0.0)
    def _():
        def col_step(i, cut):
            cand = cut + jnp.left_shift(jnp.int32(1), n_col_bits - 1 - i)
            cnt = count((sc_ref[...] == thr) & (col < cand))
            return jnp.where(cnt < need, cand, cut)

        cut_scr[...] = lax.fori_loop(0, n_col_bits, col_step, jnp.zeros((rows, 1), I32))

    sc = sc_ref[...]
    return valid & ((sc > thr) | ((sc == thr) & (col <= cut_scr[...])))


def _select_topk_tiled(sc_ref, valid_fn, n_col_bits, cut_scr, out_ref):
    rows, n = sc_ref.shape
    k_f = float(TOPK)
    lane = lax.broadcasted_iota(I32, (rows, LANES), 1)
    tiles = [(t * LANES, slice(t * LANES, (t + 1) * LANES)) for t in range(n // LANES)]
    wide = lambda v: jnp.broadcast_to(v, (rows, LANES))

    def count(pred_fn):
        acc = jnp.zeros((rows, LANES), F32)
        for c0, cols in tiles:
            acc = acc + jnp.where(pred_fn(sc_ref[:, cols], lane + c0), 1.0, 0.0)
        return jnp.sum(acc, axis=1, keepdims=True)

    few = count(lambda sc, col: valid_fn(col)) <= k_f
    key0 = jnp.where(count(lambda sc, col: sc >= 0.0) >= k_f, 0, INT_MIN).astype(I32)

    def value_step(i, key):
        cand = wide(_ordinal_to_float(key + jnp.left_shift(jnp.int32(1), 30 - i)))
        cnt = count(lambda sc, col: sc >= cand)
        return jnp.where(cnt >= k_f, key + jnp.left_shift(jnp.int32(1), 30 - i), key)

    key = lax.fori_loop(0, 31, value_step, key0)
    thr = jnp.where(few, NEG_INF, _ordinal_to_float(key))
    thr_w = wide(thr)
    need = k_f - count(lambda sc, col: sc > thr_w)
    surplus = jnp.where(few, 0.0, count(lambda sc, col: sc >= thr_w) - k_f)

    cut_scr[...] = jnp.full(cut_scr.shape, 2 ** n_col_bits - 1, I32)

    @pl.when(jnp.max(surplus) > 0.0)
    def _():
        def col_step(i, cut):
            cand = cut + jnp.left_shift(jnp.int32(1), n_col_bits - 1 - i)
            cand_w = wide(cand)
            cnt = count(lambda sc, col: (sc == thr_w) & (col < cand_w))
            return jnp.where(cnt < need, cand, cut)

        cut_scr[...] = lax.fori_loop(0, n_col_bits, col_step, jnp.zeros((rows, 1), I32))

    cut_w = wide(cut_scr[...])
    for c0, cols in tiles:
        sc, col = sc_ref[:, cols], lane + c0
        keep = valid_fn(col) & ((sc > thr_w) | ((sc == thr_w) & (col <= cut_w)))
        out_ref[:, cols] = jnp.where(keep, 1.0, 0.0)


def _pattn_kernel(q_ref, iq_ref, misc_ref, kb_ref, vb_ref, ikb_ref, o_ref, sc_scr, bias_scr, cut_scr,
                  *, pos_first, nk):
    tq = q_ref.shape[0]
    pos0 = pos_first + tq * pl.program_id(1)

    ik = ikb_ref[0]
    w = misc_ref[:, LANES:LANES + N_IDX_HEADS] * (N_IDX_HEADS ** -0.5)
    score = jnp.zeros((tq, nk), F32)
    for h in range(N_IDX_HEADS):
        iq_h = (iq_ref[:, h * IDX_DIM:(h + 1) * IDX_DIM] * (IDX_DIM ** -0.5)).astype(BF16)
        s_h = lax.dot_general(iq_h, ik, (((1,), (1,)), ((), ())), preferred_element_type=F32)
        score = score + jnp.maximum(s_h, 0.0) * w[:, h:h + 1]

    qpos = pos0 + lax.broadcasted_iota(I32, (tq, nk), 0)
    col = lax.broadcasted_iota(I32, (tq, nk), 1)
    causal = col <= qpos
    sc_scr[...] = jnp.where(causal, score, NEG_INF)
    sel = _select_topk(sc_scr, col, causal, 12, cut_scr)
    bias_scr[...] = jnp.where(sel, 0.0, NEG_INF)

    q_scale = (HEAD_DIM ** -0.5) * LOG2_E
    for h in range(N_HEADS):
        kv = h // (N_HEADS // N_KV_HEADS)
        q_h = (q_ref[:, h * HEAD_DIM:(h + 1) * HEAD_DIM] * q_scale).astype(BF16)
        k_g = kb_ref[0, :, kv * HEAD_DIM:(kv + 1) * HEAD_DIM]
        s = lax.dot_general(q_h, k_g, (((1,), (1,)), ((), ())), preferred_element_type=F32)
        s = s + bias_scr[...]
        m = jnp.max(s, axis=1, keepdims=True)
        p = jnp.exp2(s - m)
        l = jnp.sum(p, axis=1, keepdims=True)
        v_g = vb_ref[0, :, kv * HEAD_DIM:(kv + 1) * HEAD_DIM]
        o = jnp.dot(p.astype(BF16), v_g, preferred_element_type=F32) / l
        o_ref[0, :, h * HEAD_DIM:(h + 1) * HEAD_DIM] = o.astype(o_ref.dtype)


def _prompt_attention(h, kb, vb, ikb, *, tq, n_batch, n_blocks, row_block0, blocks_per_batch, pos_first, nk):
    kern = functools.partial(_pattn_kernel, pos_first=pos_first, nk=nk)
    qrow = lambda b, j: row_block0 + b * blocks_per_batch + j
    return pl.pallas_call(
        kern,
        grid=(n_batch, n_blocks),
        in_specs=[pl.BlockSpec((tq, ATTN_WIDTH), lambda b, j: (qrow(b, j), C_Q // ATTN_WIDTH)),
                  pl.BlockSpec((tq, 512), lambda b, j: (qrow(b, j), C_IQ // 512)),
                  pl.BlockSpec((tq, 256), lambda b, j: (qrow(b, j), C_IK // 256)),
                  pl.BlockSpec((1, nk, KV_WIDTH), lambda b, j: (b, 0, 0)),
                  pl.BlockSpec((1, nk, KV_WIDTH), lambda b, j: (b, 0, 0)),
                  pl.BlockSpec((1, nk, IDX_DIM), lambda b, j: (b, 0, 0))],
        out_specs=pl.BlockSpec((1, tq, ATTN_WIDTH), lambda b, j: (b, j, 0)),
        out_shape=jax.ShapeDtypeStruct((n_batch, n_blocks * tq, ATTN_WIDTH), BF16),
        scratch_shapes=[pltpu.VMEM((tq, nk), F32), pltpu.VMEM((tq, nk), F32),
                        pltpu.VMEM((tq, 1), I32)],
        compiler_params=_cparams(("arbitrary", "arbitrary")),
        name="prompt_attention",
    )(h, h, h, kb, vb, ikb)


KV_SHIFT = 2
N_MROWS = 256


SS_PAIR = 2
SS_STEPS = DEC_BATCH // SS_PAIR
SS_CHUNK = 2048


def _ssel_kernel(pt_ref, iq_ref, w_ref, iknew_ref, pthi_ref, ptlo_ref, cache_ref,
                 rows_ref, gbias_ref, nbias_ref, ikbuf, sem, sc_scr, sel_scr, m_scr, cut_scr):
    s = pl.program_id(0)

    def page_copy(step, slot, e, p):
        return pltpu.make_async_copy(cache_ref.at[pt_ref[SS_PAIR * step + e, p]],
                                     ikbuf.at[slot, e, :, pl.ds(pl.multiple_of(p * PAGE_SIZE, PAGE_SIZE), PAGE_SIZE)],
                                     sem.at[slot])

    def fetch(step, slot):
        def issue(p, carry):
            for e in range(SS_PAIR):
                page_copy(step, slot, e, p).start(priority=e)
            return carry
        lax.fori_loop(0, N_PAGES, issue, 0)

    slot = s % 2

    @pl.when(s == 0)
    def _():
        fetch(0, 0)

    @pl.when(s + 1 < SS_STEPS)
    def _():
        fetch(s + 1, 1 - slot)

    pltpu.make_async_copy(ikbuf.at[slot], ikbuf.at[slot], sem.at[slot]).wait()

    w = w_ref[0] * (N_IDX_HEADS ** -0.5)
    iq = [iq_ref[0, e].astype(BF16) for e in range(SS_PAIR)]

    def score_of(keys_of):
        r = None
        for e in range(SS_PAIR):
            s_e = jnp.dot(iq[e], keys_of(e).astype(BF16), preferred_element_type=F32)
            r_e = jnp.maximum(s_e * (IDX_DIM ** -0.5), 0.0)
            r = r_e if r is None else r + r_e
        r = r * w
        acc = r[0:8]
        for h in range(1, N_IDX_HEADS):
            acc = acc + r[8 * h:8 * h + 8]
        return acc

    row0 = pl.multiple_of(s * 8, 8)
    for c in range(PAST_LEN // SS_CHUNK):
        cols = slice(c * SS_CHUNK, (c + 1) * SS_CHUNK)
        sc_scr[pl.ds(row0, 8), cols] = score_of(lambda e: ikbuf[slot, e, :, cols])
    sc_scr[pl.ds(row0, 8), PAST_LEN:] = score_of(lambda e: iknew_ref[0, e])

    @pl.when(s == SS_STEPS - 1)
    def _():
        _ssel_finish(pthi_ref, ptlo_ref, rows_ref, gbias_ref, nbias_ref, sc_scr, sel_scr, m_scr, cut_scr)


def _ssel_finish(pthi_ref, ptlo_ref, rows_ref, gbias_ref, nbias_ref, sc_scr, sel_scr, m_scr, cut_scr):
    n_q = sc_scr.shape[0]
    qpos = PAST_LEN + (lax.broadcasted_iota(I32, (n_q, LANES), 0) & (DEC_SEQ - 1))
    valid_fn = lambda col: col <= qpos
    lane = lax.broadcasted_iota(I32, (n_q, LANES), 1)
    for t in range(NK_SAMPLE // LANES):
        cols = slice(t * LANES, (t + 1) * LANES)
        sc_scr[:, cols] = jnp.where(valid_fn(lane + t * LANES), sc_scr[:, cols], NEG_INF)
    _select_topk_tiled(sc_scr, valid_fn, 15, cut_scr, sel_scr)

    nt = (((1,), (1,)), ((), ()))
    ones_b = jnp.ones((8, LANES), BF16)
    upper_pages = (lax.broadcasted_iota(I32, (N_MROWS, N_MROWS), 0)
                   < lax.broadcasted_iota(I32, (N_MROWS, N_MROWS), 1)).astype(BF16)
    upper_lanes = (lax.broadcasted_iota(I32, (LANES, LANES), 0)
                   < lax.broadcasted_iota(I32, (LANES, LANES), 1)).astype(BF16)
    lane_id = lax.broadcasted_iota(I32, (8, LANES), 1).astype(BF16)
    page_id = lax.broadcasted_iota(I32, (8, N_MROWS), 1).astype(BF16)
    jcol = lax.broadcasted_iota(I32, (TOPK, 1), 0).astype(F32)
    spread_picks = (lax.broadcasted_iota(I32, (TOPK, N_KV_HEADS * TOPK), 0)
                    == lax.broadcasted_iota(I32, (TOPK, N_KV_HEADS * TOPK), 1) >> KV_SHIFT).astype(BF16)
    spread_new = (lax.broadcasted_iota(I32, (LANES, LANES), 0)
                  == lax.broadcasted_iota(I32, (LANES, LANES), 1) >> KV_SHIFT).astype(BF16)
    m_scr[...] = jnp.zeros(m_scr.shape, F32)

    def compact8(batch_of, new_tile):
        each = lambda f: [f(k) for k in range(8)]
        f32dot = lambda a, b: jnp.dot(a, b, preferred_element_type=F32)
        ntdot = lambda a, b: lax.dot_general(a, b, nt, preferred_element_type=F32)
        m_b = each(lambda k: m_scr[k].astype(BF16))
        cnt = each(lambda k: ntdot(ones_b, m_b[k]))
        start = each(lambda k: f32dot(cnt[k].astype(BF16), upper_pages))
        in_page = each(lambda k: (start[k][0:1] <= jcol) & (jcol < start[k][0:1] + cnt[k][0:1]))
        a_b = each(lambda k: jnp.where(in_page[k], 1.0, 0.0).astype(BF16))
        m_j = each(lambda k: f32dot(a_b[k], m_b[k]))
        rank = each(lambda k: f32dot(m_j[k].astype(BF16), upper_lanes))
        start_j = each(lambda k: jnp.sum(jnp.where(in_page[k], start[k][0:1], 0.0), axis=1, keepdims=True))
        onehot = each(lambda k: jnp.where((m_j[k] > 0.5) & (rank[k] == jcol - start_j[k]), 1.0, 0.0).astype(BF16))
        off = each(lambda k: ntdot(lane_id, onehot[k])[0:1])
        page = each(lambda k: ntdot(page_id, a_b[k])[0:1])
        phys = each(lambda k: ntdot(pthi_ref[batch_of(k)].astype(BF16), a_b[k])[0:1] * 64.0
                    + ntdot(ptlo_ref[batch_of(k)].astype(BF16), a_b[k])[0:1])
        in_past = each(lambda k: page[k] < float(N_PAGES))
        row = each(lambda k: jnp.where(in_past[k], phys[k] * float(PAGE_SIZE) + off[k], 0.0).astype(I32))
        past4 = each(lambda k: f32dot(jnp.broadcast_to(jnp.where(in_past[k], 1.0, 0.0), (8, TOPK)).astype(BF16),
                                      spread_picks)[0:1])
        new4 = each(lambda k: f32dot(jnp.broadcast_to(new_tile[k:k + 1, :], (8, LANES)).astype(BF16),
                                     spread_new)[0:1])
        return (row, each(lambda k: jnp.where(past4[k] > 0.5, 0.0, NEG_INF)),
                each(lambda k: jnp.where(new4[k] > 0.5, 0.0, NEG_INF)))

    def compact_group(g, carry):
        r8 = pl.multiple_of(g * 8, 8)
        for c in range(N_PAGES + 1):
            tile = sel_scr[pl.ds(r8, 8), c * LANES:(c + 1) * LANES]
            for k in range(8):
                m_scr[k, c:c + 1, :] = tile[k:k + 1, :]
        new_tile = sel_scr[pl.ds(r8, 8), PAST_LEN:]
        rowid = lax.broadcasted_iota(I32, (8, 1), 0)
        rows_t = jnp.zeros((8, TOPK), I32)
        gb_t = jnp.zeros((8, N_KV_HEADS * TOPK), F32)
        nb_t = jnp.zeros((8, LANES), F32)
        rows8, gb8, nb8 = compact8(lambda k: g * (8 // DEC_SEQ) + k // DEC_SEQ, new_tile)
        for k, (row, gb, nb) in enumerate(zip(rows8, gb8, nb8)):
            rows_t = jnp.where(rowid == k, row, rows_t)
            gb_t = jnp.where(rowid == k, gb, gb_t)
            nb_t = jnp.where(rowid == k, nb, nb_t)
        rows_ref[pl.ds(r8, 8), :] = rows_t
        gbias_ref[pl.ds(r8, 8), :] = gb_t
        nbias_ref[pl.ds(r8, 8), :] = nb_t
        return carry

    lax.fori_loop(0, n_q // 8, compact_group, 0)


def _sample_select(page_table, iq2, w2, ik_new_t, pt_hi, pt_lo, cache_idx_t):
    n_q = N_SAMPLE_ROWS
    whole = lambda shape: pl.BlockSpec(shape, lambda s, pt: (0,) * len(shape))
    grid_spec = pltpu.PrefetchScalarGridSpec(
        num_scalar_prefetch=1,
        grid=(SS_STEPS,),
        in_specs=[pl.BlockSpec((1, SS_PAIR, 64, IDX_DIM), lambda s, pt: (s, 0, 0, 0)),
                  pl.BlockSpec((1, 64, 1), lambda s, pt: (s, 0, 0)),
                  pl.BlockSpec((1, SS_PAIR, IDX_DIM, LANES), lambda s, pt: (s, 0, 0, 0)),
                  whole((DEC_BATCH, 8, N_MROWS)),
                  whole((DEC_BATCH, 8, N_MROWS)),
                  pl.BlockSpec(memory_space=pl.ANY)],
        out_specs=[whole((n_q, TOPK)), whole((n_q, N_KV_HEADS * TOPK)), whole((n_q, LANES))],
        scratch_shapes=[pltpu.VMEM((2, SS_PAIR, IDX_DIM, PAST_LEN), F32),
                        pltpu.SemaphoreType.DMA((2,)),
                        pltpu.VMEM((n_q, NK_SAMPLE), F32),
                        pltpu.VMEM((n_q, NK_SAMPLE), F32),
                        pltpu.VMEM((8, N_MROWS, LANES), F32),
                        pltpu.VMEM((n_q, 1), I32)])
    return pl.pallas_call(
        _ssel_kernel,
        grid_spec=grid_spec,
        out_shape=[jax.ShapeDtypeStruct((n_q, TOPK), I32),
                   jax.ShapeDtypeStruct((n_q, N_KV_HEADS * TOPK), F32),
                   jax.ShapeDtypeStruct((n_q, LANES), F32)],
        compiler_params=_cparams(("arbitrary",)),
        name="sample_select",
    )(page_table, iq2, w2, ik_new_t, pt_hi, pt_lo, cache_idx_t)


SA_COLS = N_KV_HEADS * TOPK


def _sattn_kernel(rows_ref, gbias_ref, nbias_ref, q_ref, knew_ref, vnew_ref, ck_ref, cv_ref, o_ref,
                  kbuf, vbuf, sem):
    b = pl.program_id(0)
    n_slabs = DEC_SEQ * TOPK

    def issue(t, carry):
        row = pl.multiple_of(rows_ref[b * n_slabs + t] * N_KV_HEADS, N_KV_HEADS)
        dst = pl.ds(pl.multiple_of(t * N_KV_HEADS, N_KV_HEADS), N_KV_HEADS)
        pltpu.make_async_copy(ck_ref.at[pl.ds(row, N_KV_HEADS)], kbuf.at[dst], sem.at[0]).start(priority=0)
        pltpu.make_async_copy(cv_ref.at[pl.ds(row, N_KV_HEADS)], vbuf.at[dst], sem.at[1]).start(priority=1)
        return carry

    lax.fori_loop(0, n_slabs, issue, 0, unroll=8)
    pltpu.make_async_copy(kbuf, kbuf, sem.at[0]).wait()
    pltpu.make_async_copy(vbuf, vbuf, sem.at[1]).wait()

    nt = (((1,), (1,)), ((), ()))
    scale = HEAD_DIM ** -0.5
    k_new = knew_ref[0].astype(BF16)
    v_new = vnew_ref[0].astype(BF16)
    heads_per_kv = N_HEADS // N_KV_HEADS

    def own_kv(width):
        head = lax.broadcasted_iota(I32, (N_HEADS, width), 0)
        col = lax.broadcasted_iota(I32, (N_HEADS, width), 1)
        return (col & (N_KV_HEADS - 1)) == (head >> (heads_per_kv.bit_length() - 1))

    for q in range(DEC_SEQ):
        q_h = q_ref[0, q].astype(BF16)
        rows = slice(q * SA_COLS, (q + 1) * SA_COLS)
        s = lax.dot_general(q_h, kbuf[rows, :].astype(BF16), nt, preferred_element_type=F32) * scale
        s = jnp.where(own_kv(SA_COLS), s + gbias_ref[0, q:q + 1, :], NEG_INF)
        sn = lax.dot_general(q_h, k_new, nt, preferred_element_type=F32) * scale
        sn = jnp.where(own_kv(LANES), sn + nbias_ref[0, q:q + 1, :], NEG_INF)
        m = jnp.maximum(jnp.max(s, axis=1, keepdims=True), jnp.max(sn, axis=1, keepdims=True))
        p = jnp.exp(s - m)
        pn = jnp.exp(sn - m)
        l = jnp.sum(p, axis=1, keepdims=True) + jnp.sum(pn, axis=1, keepdims=True)
        o = (jnp.dot(p.astype(BF16), vbuf[rows, :].astype(BF16), preferred_element_type=F32)
             + jnp.dot(pn.astype(BF16), v_new, preferred_element_type=F32)) / l
        o_ref[0, q] = o


def _sample_attention(rows_flat, gbias, nbias, q_s, k_new, v_new, cache_k2d, cache_v2d):
    grid_spec = pltpu.PrefetchScalarGridSpec(
        num_scalar_prefetch=1,
        grid=(DEC_BATCH,),
        in_specs=[pl.BlockSpec((1, DEC_SEQ, SA_COLS), lambda b, r: (b, 0, 0)),
                  pl.BlockSpec((1, DEC_SEQ, LANES), lambda b, r: (b, 0, 0)),
                  pl.BlockSpec((1, DEC_SEQ, N_HEADS, HEAD_DIM), lambda b, r: (b, 0, 0, 0)),
                  pl.BlockSpec((1, LANES, HEAD_DIM), lambda b, r: (b, 0, 0)),
                  pl.BlockSpec((1, LANES, HEAD_DIM), lambda b, r: (b, 0, 0)),
                  pl.BlockSpec(memory_space=pl.ANY),
                  pl.BlockSpec(memory_space=pl.ANY)],
        out_specs=pl.BlockSpec((1, DEC_SEQ, N_HEADS, HEAD_DIM), lambda b, r: (b, 0, 0, 0)),
        scratch_shapes=[pltpu.VMEM((DEC_SEQ * SA_COLS, HEAD_DIM), F32),
                        pltpu.VMEM((DEC_SEQ * SA_COLS, HEAD_DIM), F32),
                        pltpu.SemaphoreType.DMA((2,))])
    return pl.pallas_call(
        _sattn_kernel,
        grid_spec=grid_spec,
        out_shape=jax.ShapeDtypeStruct((DEC_BATCH, DEC_SEQ, N_HEADS, HEAD_DIM), F32),
        compiler_params=_cparams(("arbitrary",), disable_bounds_checks=True),
        name="sample_attention",
    )(rows_flat, gbias, nbias, q_s, k_new, v_new, cache_k2d, cache_v2d)


def _pool_kernel(p_ref, halo_ref, first_ref, cnt_ref, o_ref, ext_scr, *, tiles_per_batch):
    i = pl.program_id(0)
    tm = p_ref.shape[0]
    ext_scr[0:N_META, :] = jnp.where(i % tiles_per_batch == 0, first_ref[...], halo_ref[...])
    ext_scr[N_META:, :] = p_ref[...]
    for g, win in enumerate(POOL_WINDOWS):
        cols = slice(g * POOL_GROUP, (g + 1) * POOL_GROUP)
        x_self = ext_scr[N_META:N_META + tm, cols]
        acc = x_self
        for d in range(1, win):
            acc = acc + ext_scr[N_META - d:N_META - d + tm, cols]
        o_ref[:, cols] = (acc / cnt_ref[:, cols] - x_self).astype(o_ref.dtype)


def _pool(p_src, p_col_block, halo_src, first_src, first_block, cnt, *, rows, tm, tiles_per_batch):
    kern = functools.partial(_pool_kernel, tiles_per_batch=tiles_per_batch)
    halo_per_tile = tm // N_META
    cnt_rows = cnt.shape[0]
    cnt_map = (lambda i: (0, 0)) if cnt_rows == 1 else (lambda i: (i, 0))
    return pl.pallas_call(
        kern,
        grid=(rows // tm,),
        in_specs=[pl.BlockSpec((tm, POOL_WIDTH), lambda i: (i, p_col_block)),
                  pl.BlockSpec((N_META, POOL_WIDTH),
                               lambda i: (jnp.maximum(i * halo_per_tile - 1, 0), p_col_block)),
                  pl.BlockSpec((N_META, POOL_WIDTH), lambda i: first_block),
                  pl.BlockSpec((1 if cnt_rows == 1 else tm, POOL_WIDTH), cnt_map)],
        out_specs=pl.BlockSpec((tm, POOL_WIDTH), lambda i: (i, 0)),
        out_shape=jax.ShapeDtypeStruct((rows, POOL_WIDTH), BF16),
        scratch_shapes=[pltpu.VMEM((tm + N_META, POOL_WIDTH), F32)],
        compiler_params=_cparams(("arbitrary",)),
        name="pool",
    )(p_src, halo_src, first_src, cnt)


def _layer_norm(y, g_ref, b_ref):
    mu = jnp.mean(y, axis=-1, keepdims=True)
    var = jnp.mean(jnp.square(y - mu), axis=-1, keepdims=True)
    return (y - mu) * lax.rsqrt(var + LN_EPS) * g_ref[...] + b_ref[...]


def _mix_kernel(pooled_ref, a_ref, x_ref, wp_ref, ps_ref, wo_ref, g_ref, b_ref, x1_ref):
    parts = []
    for g in range(len(POOL_WINDOWS)):
        cols = slice(g * POOL_GROUP, (g + 1) * POOL_GROUP)
        parts.append(jnp.dot(pooled_ref[:, cols], wp_ref[g], preferred_element_type=F32))
    m = jnp.concatenate(parts, axis=1) * ps_ref[...]
    mix = (jnp.dot(a_ref[...], wo_ref[:ATTN_WIDTH, :], preferred_element_type=F32)
           + jnp.dot(m.astype(BF16), wo_ref[ATTN_WIDTH:, :], preferred_element_type=F32))
    x1_ref[...] = _layer_norm(DEEPNORM_ALPHA * x_ref[...] + mix, g_ref, b_ref)


def _mix(pooled, a, x, w_pool, pool_scale, w_out, g1, b1, tm):
    rows = x.shape[0]
    row = lambda i: (i, 0)
    const2 = lambda i: (0, 0)
    return pl.pallas_call(
        _mix_kernel,
        grid=(rows // tm,),
        in_specs=[pl.BlockSpec((tm, POOL_WIDTH), row),
                  pl.BlockSpec((tm, ATTN_WIDTH), row),
                  pl.BlockSpec((tm, D_MODEL), row),
                  pl.BlockSpec((len(POOL_WINDOWS), POOL_GROUP, POOL_GROUP), lambda i: (0, 0, 0)),
                  pl.BlockSpec((1, POOL_WIDTH), const2),
                  pl.BlockSpec((D_MODEL, D_MODEL), const2),
                  pl.BlockSpec((1, D_MODEL), const2),
                  pl.BlockSpec((1, D_MODEL), const2)],
        out_specs=pl.BlockSpec((tm, D_MODEL), row),
        out_shape=jax.ShapeDtypeStruct((rows, D_MODEL), F32),
        compiler_params=_cparams(("arbitrary",)),
        name="mix_ln1",
    )(pooled, a, x, w_pool, pool_scale, w_out, g1, b1)


PA_TQ = 256
PA_GROUP = 1
TF = 512
TF_SUB = 256
N_FF_CHUNKS = D_FF // TF
CONV_HALO = 8


def _silu(x):
    return x * (0.5 * jnp.tanh(0.5 * x) + 0.5)


def _ffn_accumulate(gate, val, wd_ref, x1_ref, g_ref, b_ref, o_ref, acc_ref):
    c = pl.program_id(1)
    h = (_silu(gate) * val).astype(BF16)
    part = jnp.dot(h, wd_ref[...], preferred_element_type=F32)

    @pl.when(c == 0)
    def _():
        acc_ref[...] = part

    @pl.when(c > 0)
    def _():
        acc_ref[...] += part

    @pl.when(c == N_FF_CHUNKS - 1)
    def _():
        o_ref[...] = _layer_norm(DEEPNORM_ALPHA * x1_ref[...] + acc_ref[...], g_ref, b_ref)


def _ffn_fused_kernel(x1_ref, wg_ref, wv_ref, fg_ref, fv_ref, cwg_ref, cwv_ref, cbg_ref, cbv_ref,
                      wd_ref, g_ref, b_ref, o_ref, tg_ref, tv_ref, xb_scr, carry_g, carry_v, ext_scr,
                      *, tiles_per_batch):
    i = pl.program_id(0)
    c = pl.program_id(1)
    tm = x1_ref.shape[0]
    first = i % tiles_per_batch == 0

    @pl.when(c == 0)
    def _():
        xb_scr[...] = x1_ref[...].astype(BF16)
        o_ref[...] = jnp.zeros(o_ref.shape, F32)

    @pl.when(i == 0)
    def _():
        carry_g[c] = jnp.zeros((CONV_HALO, TF), F32)
        carry_v[c] = jnp.zeros((CONV_HALO, TF), F32)

    part = None
    for s in range(TF // TF_SUB):
        cols = slice(s * TF_SUB, (s + 1) * TF_SUB)

        def conv(w_ref, f_ref, carry, tail_ref, cw_ref, cb_ref, ext):
            u = jnp.dot(xb_scr[...], w_ref[:, cols], preferred_element_type=F32)
            ext[0:CONV_HALO, :] = jnp.where(first, f_ref[:, cols], carry[c, :, cols])
            ext[CONV_HALO:, :] = u
            tail = u[tm - CONV_HALO:, :]
            carry[c, :, cols] = tail
            tail_ref[0, :, cols] = tail
            return (cb_ref[:, cols] + cw_ref[0:1, cols] * ext[CONV_HALO - 2:CONV_HALO - 2 + tm, :]
                    + cw_ref[1:2, cols] * ext[CONV_HALO - 1:CONV_HALO - 1 + tm, :]
                    + cw_ref[2:3, cols] * u)

        gate = conv(wg_ref, fg_ref, carry_g, tg_ref, cwg_ref, cbg_ref, ext_scr.at[2 * s])
        val = conv(wv_ref, fv_ref, carry_v, tv_ref, cwv_ref, cbv_ref, ext_scr.at[2 * s + 1])
        h = (_silu(gate) * val).astype(BF16)
        d = jnp.dot(h, wd_ref[cols, :], preferred_element_type=F32)
        part = d if part is None else part + d
    o_ref[...] += part

    @pl.when(c == N_FF_CHUNKS - 1)
    def _():
        o_ref[...] = _layer_norm(DEEPNORM_ALPHA * x1_ref[...] + o_ref[...], g_ref, b_ref)


def _ffn_prev_kernel(ug_ref, uv_ref, p1g_ref, p1v_ref, p2g_ref, p2v_ref, cwg_ref, cwv_ref, cbg_ref, cbv_ref,
                     wd_ref, x1_ref, g_ref, b_ref, o_ref, acc_ref):
    def conv(u_ref, p1_ref, p2_ref, cw_ref, cb_ref):
        return (cb_ref[...] + cw_ref[0:1, :] * p2_ref[...] + cw_ref[1:2, :] * p1_ref[...]
                + cw_ref[2:3, :] * u_ref[...])

    gate = conv(ug_ref, p1g_ref, p2g_ref, cwg_ref, cbg_ref)
    val = conv(uv_ref, p1v_ref, p2v_ref, cwv_ref, cbv_ref)
    _ffn_accumulate(gate, val, wd_ref, x1_ref, g_ref, b_ref, o_ref, acc_ref)


def _ffn_common_specs(tm):
    gate_c = lambda i, c: (0, c)
    val_c = lambda i, c: (0, N_FF_CHUNKS + c)
    return [pl.BlockSpec((3, TF), gate_c), pl.BlockSpec((3, TF), val_c),
            pl.BlockSpec((1, TF), gate_c), pl.BlockSpec((1, TF), val_c),
            pl.BlockSpec((TF, D_MODEL), lambda i, c: (c, 0)),
            pl.BlockSpec((tm, D_MODEL), lambda i, c: (i, 0)),
            pl.BlockSpec((1, D_MODEL), lambda i, c: (0, 0)),
            pl.BlockSpec((1, D_MODEL), lambda i, c: (0, 0))]


def _ffn_fused(x1, w_up, u_first, first_row_block, conv_w, conv_b, w_down, g2, b2, *, tm, tiles_per_batch):
    rows = x1.shape[0]
    kern = functools.partial(_ffn_fused_kernel, tiles_per_batch=tiles_per_batch)
    gate_c = lambda i, c: (0, c)
    val_c = lambda i, c: (0, N_FF_CHUNKS + c)
    const = lambda i, c: (0, 0)
    tail_spec = pl.BlockSpec((1, CONV_HALO, TF), lambda i, c: (i, 0, c))
    tail_shape = jax.ShapeDtypeStruct((rows // tm, CONV_HALO, D_FF), F32)
    y, tail_g, tail_v = pl.pallas_call(
        kern,
        grid=(rows // tm, N_FF_CHUNKS),
        in_specs=[pl.BlockSpec((tm, D_MODEL), lambda i, c: (i, 0), pipeline_mode=pl.Buffered(1)),
                  pl.BlockSpec((D_MODEL, TF), gate_c), pl.BlockSpec((D_MODEL, TF), val_c),
                  pl.BlockSpec((CONV_HALO, TF), lambda i, c: (first_row_block, c)),
                  pl.BlockSpec((CONV_HALO, TF), lambda i, c: (first_row_block, N_FF_CHUNKS + c)),
                  pl.BlockSpec((3, TF), gate_c), pl.BlockSpec((3, TF), val_c),
                  pl.BlockSpec((1, TF), gate_c), pl.BlockSpec((1, TF), val_c),
                  pl.BlockSpec((TF, D_MODEL), lambda i, c: (c, 0)),
                  pl.BlockSpec((1, D_MODEL), const), pl.BlockSpec((1, D_MODEL), const)],
        out_specs=[pl.BlockSpec((tm, D_MODEL), lambda i, c: (i, 0)), tail_spec, tail_spec],
        out_shape=[jax.ShapeDtypeStruct((rows, D_MODEL), F32), tail_shape, tail_shape],
        scratch_shapes=[pltpu.VMEM((tm, D_MODEL), BF16),
                        pltpu.VMEM((N_FF_CHUNKS, CONV_HALO, TF), F32),
                        pltpu.VMEM((N_FF_CHUNKS, CONV_HALO, TF), F32),
                        pltpu.VMEM((2 * (TF // TF_SUB), tm + CONV_HALO, TF_SUB), F32)],
        compiler_params=_cparams(("arbitrary", "arbitrary"), vmem_limit=FFN_VMEM_LIMIT),
        name="ffn",
    )(x1, w_up, w_up, u_first, u_first, conv_w, conv_w, conv_b, conv_b, w_down, g2, b2)
    last = slice(tiles_per_batch - 1, None, tiles_per_batch)
    return y, tail_g[last], tail_v[last]


def _ffn_tail_prev(u, prev1, prev2, conv_w, conv_b, w_down, x1, g2, b2, *, tm):
    rows = u.shape[0]
    gate_t = lambda i, c: (i, c)
    val_t = lambda i, c: (i, N_FF_CHUNKS + c)
    return pl.pallas_call(
        _ffn_prev_kernel,
        grid=(rows // tm, N_FF_CHUNKS),
        in_specs=[pl.BlockSpec((tm, TF), gate_t), pl.BlockSpec((tm, TF), val_t),
                  pl.BlockSpec((tm, TF), gate_t), pl.BlockSpec((tm, TF), val_t),
                  pl.BlockSpec((tm, TF), gate_t), pl.BlockSpec((tm, TF), val_t)]
                 + _ffn_common_specs(tm),
        out_specs=pl.BlockSpec((tm, D_MODEL), lambda i, c: (i, 0)),
        out_shape=jax.ShapeDtypeStruct((rows, D_MODEL), F32),
        scratch_shapes=[pltpu.VMEM((tm, D_MODEL), F32)],
        compiler_params=_cparams(("arbitrary", "arbitrary")),
        name="ffn_tail_small",
    )(u, u, prev1, prev1, prev2, prev2, conv_w, conv_w, conv_b, conv_b, w_down, x1, g2, b2)


def kernel(x_prompt, x_sample, cache_k, cache_v, cache_idx_k, state_pool, state_conv, page_table, meta_tokens,
           w_in, w_pool, pool_scale, w_out, ln1_g, ln1_b, w_up, conv_w, conv_b, w_down, ln2_g, ln2_b):
    n_phys = cache_k.shape[1]

    wq, wk, wv, wiq, wik, wiw, wp = jnp.split(w_in[0], IN_OFFSETS, axis=1)
    zcols = lambda n: jnp.zeros((D_MODEL, n), F32)
    w_all = jnp.concatenate([wq, wp, wk, wv, wiq, wik, zcols(64), wiw, zcols(PROJ_WIDTH - C_IW - N_IDX_HEADS)],
                            axis=1).astype(BF16)
    w_pool_b = w_pool[0].astype(BF16)
    w_out_b = w_out[0].astype(BF16)
    w_up_b = w_up[0].astype(BF16)
    w_down_b = w_down[0].astype(BF16)
    row2 = lambda a: a.reshape(1, -1)

    pos_prompt = N_META + np.arange(SEQ)
    pos_small = np.concatenate([np.tile(PAST_LEN + np.arange(DEC_SEQ), DEC_BATCH), np.arange(N_META),
                                np.zeros((SMALL_ROWS - N_SAMPLE_ROWS - N_META,), np.int64)])
    tabs_p = _rope_tables(pos_prompt, ROT_DIM, HEAD_DIM) + _rope_tables(pos_prompt, IDX_ROT_DIM, IDX_DIM)
    tabs_s = _rope_tables(pos_small, ROT_DIM, HEAD_DIM) + _rope_tables(pos_small, IDX_ROT_DIM, IDX_DIM)

    xp = x_prompt.reshape(N_PROMPT_ROWS, D_MODEL)
    xs = jnp.concatenate([x_sample.reshape(N_SAMPLE_ROWS, D_MODEL), meta_tokens.astype(F32),
                          jnp.zeros((SMALL_ROWS - N_SAMPLE_ROWS - N_META, D_MODEL), F32)], axis=0)

    tm_p = 1024
    hp = _project(xp, w_all, tabs_p, SEQ // tm_p, tm_p)
    hs = _project(xs, w_all, tabs_s, 1, SMALL_ROWS)

    meta = slice(META_ROW0, META_ROW0 + N_META)

    def with_meta(c0, width):
        m = jnp.broadcast_to(hs[meta, c0:c0 + width][None], (BATCH, N_META, width))
        return jnp.concatenate([m, hp[:, c0:c0 + width].reshape(BATCH, SEQ, width)], axis=1)

    k_prompt = with_meta(C_K, KV_WIDTH)
    v_prompt = with_meta(C_V, KV_WIDTH)
    ik_prompt = with_meta(C_IK, IDX_DIM)
    pad_keys = lambda a: jnp.pad(a.astype(BF16), ((0, 0), (0, NK_PROMPT - T_PROMPT), (0, 0)))
    kb, vb, ikb = pad_keys(k_prompt), pad_keys(v_prompt), pad_keys(ik_prompt)

    blocks_per_batch = SEQ // PA_TQ
    a_groups = []
    for j0 in range(0, blocks_per_batch, PA_GROUP):
        nk = N_META + (j0 + PA_GROUP) * PA_TQ
        nk = -(-nk // LANES) * LANES
        a_groups.append(_prompt_attention(hp, kb, vb, ikb, tq=PA_TQ, n_batch=BATCH, n_blocks=PA_GROUP,
                                          row_block0=j0, blocks_per_batch=blocks_per_batch,
                                          pos_first=N_META + j0 * PA_TQ, nk=nk))
    a_p = jnp.concatenate(a_groups, axis=1).reshape(N_PROMPT_ROWS, ATTN_WIDTH)
    a_m = _prompt_attention(hs, kb, vb, ikb, tq=LANES, n_batch=1, n_blocks=1, row_block0=META_ROW0 // LANES,
                            blocks_per_batch=1, pos_first=0, nk=LANES)[0]

    hs_s = hs[:N_SAMPLE_ROWS]
    iq_s = hs_s[:, C_IQ:C_IQ + N_IDX_HEADS * IDX_DIM].reshape(SS_STEPS, SS_PAIR, DEC_SEQ, N_IDX_HEADS, IDX_DIM)
    iq_s = iq_s.transpose(0, 1, 3, 2, 4)
    iq2 = jnp.stack([jnp.pad(iq_s[:, e], ((0, 0), (0, 0), (DEC_SEQ * e, DEC_SEQ * (SS_PAIR - 1 - e)), (0, 0)))
                     for e in range(SS_PAIR)], axis=1).reshape(SS_STEPS, SS_PAIR, N_IDX_HEADS * 8, IDX_DIM)
    w_s = hs_s[:, C_IW:C_IW + N_IDX_HEADS].reshape(SS_STEPS, SS_PAIR * DEC_SEQ, N_IDX_HEADS)
    w2 = w_s.transpose(0, 2, 1).reshape(SS_STEPS, N_IDX_HEADS * 8, 1)
    ik_new_t = jnp.pad(hs_s[:, C_IK:C_IK + IDX_DIM].reshape(DEC_BATCH, DEC_SEQ, IDX_DIM).transpose(0, 2, 1),
                       ((0, 0), (0, 0), (0, LANES - DEC_SEQ))).reshape(SS_STEPS, SS_PAIR, IDX_DIM, LANES)

    pt_pad = jnp.pad(page_table, ((0, 0), (0, N_MROWS - N_PAGES)))
    pt_hi = jnp.broadcast_to((pt_pad // 64).astype(F32)[:, None, :], (DEC_BATCH, 8, N_MROWS))
    pt_lo = jnp.broadcast_to((pt_pad % 64).astype(F32)[:, None, :], (DEC_BATCH, 8, N_MROWS))
    rows_sel, gbias, nbias = _sample_select(page_table, iq2, w2, ik_new_t, pt_hi, pt_lo,
                                            jnp.swapaxes(cache_idx_k[0], 1, 2))

    def new_rows(c0):
        a = hs_s[:, c0:c0 + KV_WIDTH].reshape(DEC_BATCH, DEC_SEQ * N_KV_HEADS, HEAD_DIM)
        return jnp.pad(a, ((0, 0), (0, LANES - DEC_SEQ * N_KV_HEADS), (0, 0)))

    a_s4 = _sample_attention(rows_sel.reshape(-1), gbias.reshape(DEC_BATCH, DEC_SEQ, SA_COLS),
                             nbias.reshape(DEC_BATCH, DEC_SEQ, LANES),
                             hs_s[:, C_Q:C_Q + ATTN_WIDTH].reshape(DEC_BATCH, DEC_SEQ, N_HEADS, HEAD_DIM),
                             new_rows(C_K), new_rows(C_V),
                             cache_k[0].reshape(n_phys * PAGE_SIZE * N_KV_HEADS, HEAD_DIM),
                             cache_v[0].reshape(n_phys * PAGE_SIZE * N_KV_HEADS, HEAD_DIM))
    a_s = a_s4.reshape(N_SAMPLE_ROWS, ATTN_WIDTH).astype(BF16)
    a_small = jnp.concatenate([a_s, a_m[:N_META], jnp.zeros((SMALL_ROWS - N_SAMPLE_ROWS - N_META, ATTN_WIDTH),
                                                            BF16)], axis=0)

    win = jnp.repeat(jnp.asarray(POOL_WINDOWS, F32), POOL_GROUP)
    p_meta = hs[meta, C_P:C_P + POOL_WIDTH]
    tm_pool = 256
    pooled_p = _pool(hp, C_P // POOL_WIDTH, hp, hs, (META_ROW0 // N_META, C_P // POOL_WIDTH), row2(win),
                     rows=N_PROMPT_ROWS, tm=tm_pool, tiles_per_batch=SEQ // tm_pool)
    p_s = hs_s[:, C_P:C_P + POOL_WIDTH].reshape(DEC_BATCH, DEC_SEQ, POOL_WIDTH)
    grp = 24
    ext_s = jnp.concatenate([jnp.zeros((DEC_BATCH, 1, POOL_WIDTH), F32), state_pool[0], p_s,
                             jnp.zeros((DEC_BATCH, grp - 1 - POOL_CTX - DEC_SEQ, POOL_WIDTH), F32)], axis=1)
    ext_small = jnp.concatenate([ext_s.reshape(DEC_BATCH * grp, POOL_WIDTH),
                                 jnp.zeros((N_META, POOL_WIDTH), F32), p_meta], axis=0)
    n_ext = DEC_BATCH * grp + 2 * N_META
    cnt_meta = jnp.minimum(win[None, :], (jnp.arange(N_META, dtype=F32) + 1.0)[:, None])
    cnt_small = jnp.concatenate([jnp.broadcast_to(win[None], (n_ext - N_META, POOL_WIDTH)), cnt_meta], axis=0)
    zeros_halo = jnp.zeros((N_META, POOL_WIDTH), F32)
    pooled_ext = _pool(ext_small, 0, zeros_halo, zeros_halo, (0, 0), cnt_small,
                       rows=n_ext, tm=n_ext, tiles_per_batch=1)
    pooled_small = jnp.concatenate(
        [pooled_ext[:DEC_BATCH * grp].reshape(DEC_BATCH, grp, POOL_WIDTH)[:, 16:16 + DEC_SEQ].reshape(
            N_SAMPLE_ROWS, POOL_WIDTH),
         pooled_ext[n_ext - N_META:],
         jnp.zeros((SMALL_ROWS - N_SAMPLE_ROWS - N_META, POOL_WIDTH), BF16)], axis=0)

    mix_args = (w_pool_b, row2(pool_scale[0]), w_out_b, row2(ln1_g[0]), row2(ln1_b[0]))
    x1_p = _mix(pooled_p, a_p, xp, *mix_args, tm=512)
    x1_s = _mix(pooled_small, a_small, xs, *mix_args, tm=SMALL_ROWS)

    u_s = _matmul(x1_s.astype(BF16), w_up_b, SMALL_ROWS, "ffn_up_small")
    ffn_args = (conv_w[0], row2(conv_b[0]), w_down_b)
    ln2 = (row2(ln2_g[0]), row2(ln2_b[0]))
    tm_f = 1024
    y_p, tail_g, tail_v = _ffn_fused(x1_p, w_up_b, u_s, (META_ROW0 + N_META) // CONV_HALO - 1, *ffn_args, *ln2,
                                     tm=tm_f, tiles_per_batch=SEQ // tm_f)
    u_s3 = u_s[:N_SAMPLE_ROWS].reshape(DEC_BATCH, DEC_SEQ, 2 * D_FF)
    ext_u = jnp.concatenate([state_conv[0], u_s3], axis=1)
    u_m = u_s[meta]
    ext_m = jnp.concatenate([jnp.zeros((2, 2 * D_FF), F32), u_m], axis=0)
    tail0 = jnp.zeros((SMALL_ROWS - N_SAMPLE_ROWS - N_META, 2 * D_FF), F32)
    prev1 = jnp.concatenate([ext_u[:, 1:1 + DEC_SEQ].reshape(N_SAMPLE_ROWS, -1), ext_m[1:1 + N_META], tail0], 0)
    prev2 = jnp.concatenate([ext_u[:, 0:DEC_SEQ].reshape(N_SAMPLE_ROWS, -1), ext_m[0:N_META], tail0], 0)
    y_s = _ffn_tail_prev(u_s, prev1, prev2, *ffn_args, x1_s, *ln2, tm=SMALL_ROWS)

    y_prompt = y_p.reshape(BATCH, SEQ, D_MODEL)
    y_sample = y_s[:N_SAMPLE_ROWS].reshape(DEC_BATCH, DEC_SEQ, D_MODEL)
    pool_prompt = hp.reshape(BATCH, SEQ, PROJ_WIDTH)[:, SEQ - POOL_CTX:, C_P:C_P + POOL_WIDTH]
    conv_prompt = jnp.concatenate([tail_g[:, CONV_HALO - 2:], tail_v[:, CONV_HALO - 2:]], axis=-1)
    k_sample = hs_s[:, C_K:C_K + KV_WIDTH].reshape(DEC_BATCH, DEC_SEQ, N_KV_HEADS, HEAD_DIM)
    v_sample = hs_s[:, C_V:C_V + KV_WIDTH].reshape(DEC_BATCH, DEC_SEQ, N_KV_HEADS, HEAD_DIM)
    ik_sample = hs_s[:, C_IK:C_IK + IDX_DIM].reshape(DEC_BATCH, DEC_SEQ, IDX_DIM)
    pool_sample = jnp.concatenate([state_pool[0], p_s], axis=1)[:, DEC_SEQ:]
    conv_sample = ext_u[:, DEC_SEQ:]
    return (y_prompt, y_sample,
            k_prompt.reshape(1, BATCH, T_PROMPT, N_KV_HEADS, HEAD_DIM),
            v_prompt.reshape(1, BATCH, T_PROMPT, N_KV_HEADS, HEAD_DIM),
            ik_prompt[None], pool_prompt[None], conv_prompt[None],
            k_sample[None], v_sample[None], ik_sample[None], pool_sample[None], conv_sample[None])
```
